```python
import math
import jax
import jax.numpy as jnp
from jax import lax
import numpy as np

D_MODEL = 2048
BATCH = 2
SEQ = 16384
DEPTH = 2
DEC_BATCH = 2
DEC_SEQ = 4096
PAST_LEN = 128

HEAD_DIM = 128
BLOCK = 128
GRID_W = 64
WIN_Q_HEADS = D_MODEL // (2 * HEAD_DIM)
WIN_KV_HEADS = WIN_Q_HEADS // 4
WINDOW = 128
DIFF_HEADS = D_MODEL // (4 * HEAD_DIM)
DIFF_V_DIM = 2 * HEAD_DIM
AX_Q_HEADS = D_MODEL // (2 * HEAD_DIM)
AX_KV_HEADS = AX_Q_HEADS // 4
ROPE_THETA = 10000.0
NA_HEADS = D_MODEL // (2 * HEAD_DIM)
NA_ROWS = 8
NA_COLS = 16
N_GROUPS = 4
EXPERTS_PER_GROUP = 4
N_EXPERTS = N_GROUPS * EXPERTS_PER_GROUP
TOP_K_IN_GROUP = 2
D_EXPERT = D_MODEL // 4
LN_EPS = 1e-5
RMS_EPS = 1e-6
NEG_INF = -1e30
DEEPNORM_ALPHA = (2 * DEPTH) ** 0.25
DEEPNORM_BETA = (8 * DEPTH) ** -0.25
N_EVEN = (DEPTH + 1) // 2
N_ODD = DEPTH // 2
AB_SIZES = (WIN_Q_HEADS * HEAD_DIM, WIN_KV_HEADS * HEAD_DIM, WIN_KV_HEADS * HEAD_DIM,
            DIFF_HEADS * 2 * HEAD_DIM, DIFF_HEADS * 2 * HEAD_DIM, DIFF_HEADS * DIFF_V_DIM)
CD_SIZES = (AX_Q_HEADS * HEAD_DIM, AX_KV_HEADS * HEAD_DIM, AX_KV_HEADS * HEAD_DIM,
            NA_HEADS * HEAD_DIM, NA_HEADS * HEAD_DIM, NA_HEADS * HEAD_DIM)
AB_IN = sum(AB_SIZES)
CD_IN = sum(CD_SIZES)
AB_OUT = WIN_Q_HEADS * HEAD_DIM + DIFF_HEADS * DIFF_V_DIM
CD_OUT = AX_Q_HEADS * HEAD_DIM + NA_HEADS * HEAD_DIM

kernel_name = 'hybrid_bidir_encoder_two_groups'


def _split_points(sizes):
    return [int(v) for v in np.cumsum(sizes)[:-1]]


def layer_norm(x, g, b):
    xf = x.astype(jnp.float32)
    mu = jnp.mean(xf, axis=-1, keepdims=True)
    var = jnp.mean(jnp.square(xf - mu), axis=-1, keepdims=True)
    return ((xf - mu) * lax.rsqrt(var + LN_EPS) * g + b).astype(x.dtype)


def rms_norm(x, g):
    xf = x.astype(jnp.float32)
    return (xf * lax.rsqrt(jnp.mean(xf * xf, axis=-1, keepdims=True) + RMS_EPS) * g).astype(x.dtype)


def alibi_slopes(n):
    return jnp.asarray(2.0 ** (-8.0 * np.arange(1, n + 1) / n), dtype=jnp.float32)


def windowed_sink_attention(q, k, v, sink):
    bsz, seq, hq, dh = q.shape
    hkv = k.shape[2]
    grp = hq // hkv
    nb = seq // BLOCK
    qb = q.reshape(bsz, nb, BLOCK, hkv, grp, dh)
    pad = ((0, 0), (BLOCK, BLOCK), (0, 0), (0, 0))
    kp = jnp.pad(k, pad).reshape(bsz, nb + 2, BLOCK, hkv, dh)
    vp = jnp.pad(v, pad).reshape(bsz, nb + 2, BLOCK, hkv, dh)

    def band(a):
        return jnp.concatenate([a[:, :-2], a[:, 1:-1], a[:, 2:]], axis=2)

    kb, vb = band(kp), band(vp)
    qpos = jnp.arange(seq).reshape(nb, BLOCK)
    kpos = jnp.arange(nb)[:, None] * BLOCK - BLOCK + jnp.arange(3 * BLOCK)[None, :]
    dist = jnp.abs(qpos[:, :, None] - kpos[:, None, :])
    valid = (dist <= WINDOW) & (kpos[:, None, :] >= 0) & (kpos[:, None, :] < seq)
    slopes = alibi_slopes(hq).reshape(hkv, grp)
    s = jnp.einsum('bnqkgd,bnskd->bnkgqs', qb, kb).astype(jnp.float32) * (dh ** -0.5)
    s = s - slopes[None, None, :, :, None, None] * dist[None, :, None, None].astype(jnp.float32)
    s = jnp.where(valid[None, :, None, None], s, NEG_INF)
    sink_col = jnp.broadcast_to(sink.astype(jnp.float32).reshape(hkv, grp)[None, None, :, :, None, None],
                                s.shape[:-1] + (1,))
    p = jax.nn.softmax(jnp.concatenate([s, sink_col], axis=-1), axis=-1)[..., :-1]
    o = jnp.einsum('bnkgqs,bnskd->bnqkgd', p.astype(v.dtype), vb)
    return o.reshape(bsz, seq, hq * dh)


def diff_attention(q1, q2, k1, k2, v, lam, subln_g, lam_init):
    bsz, seq, heads, dh = q1.shape
    nb = seq // BLOCK
    slopes = alibi_slopes(heads)
    kpos = jnp.arange(seq)
    scale = dh ** -0.5

    def to_blocks(a):
        return jnp.moveaxis(a.reshape(bsz, nb, BLOCK, heads, a.shape[-1]), 1, 0)

    def block(args):
        qb1, qb2, start = args
        qpos = start + jnp.arange(BLOCK)
        bias = -slopes[:, None, None] * jnp.abs(qpos[:, None] - kpos[None, :]).astype(jnp.float32)[None]
        s1 = jnp.einsum('bqhd,bshd->bhqs', qb1, k1).astype(jnp.float32) * scale + bias
        s2 = jnp.einsum('bqhd,bshd->bhqs', qb2, k2).astype(jnp.float32) * scale + bias
        a = jax.nn.softmax(s1, axis=-1) - lam * jax.nn.softmax(s2, axis=-1)
        return jnp.einsum('bhqs,bshe->bqhe', a.astype(v.dtype), v)

    o = lax.map(block, (to_blocks(q1), to_blocks(q2), jnp.arange(nb) * BLOCK))
    o = jnp.moveaxis(o, 0, 1).reshape(bsz, seq, heads, DIFF_V_DIM)
    o = rms_norm(o, subln_g) * (1.0 - lam_init)
    return o.reshape(bsz, seq, heads * DIFF_V_DIM)


def axial_rope(x, rows, cols):
    half = x.shape[-1] // 2
    inv = jnp.asarray(1.0 / (ROPE_THETA ** (np.arange(0, half, 2) / half)), dtype=jnp.float32)

    def rot(xp, pos):
        ang = pos.astype(jnp.float32)[:, None] * inv[None, :]
        cos = jnp.cos(ang)[None, :, None, :]
        sin = jnp.sin(ang)[None, :, None, :]
        x1, x2 = jnp.split(xp.astype(jnp.float32), 2, axis=-1)
        return jnp.concatenate([x1 * cos - x2 * sin, x1 * sin + x2 * cos], axis=-1)

    xr, xc = jnp.split(x, 2, axis=-1)
    return jnp.concatenate([rot(xr, rows), rot(xc, cols)], axis=-1).astype(x.dtype)


def dense_gqa_blocks(q, k, v):
    bsz, seq, hq, dh = q.shape
    hkv = k.shape[2]
    grp = hq // hkv
    nb = seq // BLOCK
    qb = jnp.moveaxis(q.reshape(bsz, nb, BLOCK, hkv, grp, dh), 1, 0)
    scale = dh ** -0.5

    def block(qblk):
        s = jnp.einsum('bqkgd,bskd->bkgqs', qblk, k).astype(jnp.float32) * scale
        p = jax.nn.softmax(s, axis=-1)
        return jnp.einsum('bkgqs,bskd->bqkgd', p.astype(v.dtype), v)

    o = lax.map(block, qb)
    return jnp.moveaxis(o, 0, 1).reshape(bsz, seq, hq * dh)


def neighborhood_attention(q, k, v, rpb):
    bsz, seq, heads, dh = q.shape
    rows = seq // GRID_W
    kr = min(NA_ROWS, rows)
    qg = q.reshape(bsz, rows, GRID_W, heads, dh)
    kg = k.reshape(bsz, rows, GRID_W, heads, dh)
    vg = v.reshape(bsz, rows, GRID_W, heads, dh)
    col = jnp.arange(GRID_W)
    c_start = jnp.clip(col - NA_COLS // 2, 0, GRID_W - NA_COLS)
    col_mask = (col[None, :] >= c_start[:, None]) & (col[None, :] < c_start[:, None] + NA_COLS)
    rel_c_idx = jnp.clip(col[None, :] - col[:, None] + NA_COLS - 1, 0, 2 * NA_COLS - 2)
    scale = dh ** -0.5

    def row_block(r):
        r_start = jnp.clip(r - kr // 2, 0, rows - kr)
        k_rows = lax.dynamic_slice_in_dim(kg, r_start, kr, axis=1)
        v_rows = lax.dynamic_slice_in_dim(vg, r_start, kr, axis=1)
        q_row = lax.dynamic_index_in_dim(qg, r, axis=1, keepdims=False)
        rel_r_idx = r_start + jnp.arange(kr) - r + NA_ROWS - 1
        bias = rpb[:, rel_r_idx[None, :, None], rel_c_idx[:, None, :]]
        s = jnp.einsum('bqhd,bjkhd->bhqjk', q_row, k_rows).astype(jnp.float32) * scale
        s = s + bias.astype(jnp.float32)
        s = jnp.where(col_mask[:, None, :], s, NEG_INF)
        p = jax.nn.softmax(s.reshape(bsz, heads, GRID_W, kr * GRID_W), axis=-1)
        p = p.reshape(bsz, heads, GRID_W, kr, GRID_W).astype(v.dtype)
        return jnp.einsum('bhqjk,bjkhd->bqhd', p, v_rows)

    o = lax.map(row_block, jnp.arange(rows))
    return jnp.moveaxis(o, 0, 1).reshape(bsz, seq, heads * dh)


def mixer_ab(h, w_in, w_out, sink, lq1, lk1, lq2, lk2, subln_g, lam_init):
    bsz, seq, _ = h.shape
    proj = h @ w_in
    qa, ka, va, qb, kb, vb = jnp.split(proj, _split_points(AB_SIZES), axis=-1)
    oa = windowed_sink_attention(qa.reshape(bsz, seq, WIN_Q_HEADS, HEAD_DIM),
                                 ka.reshape(bsz, seq, WIN_KV_HEADS, HEAD_DIM),
                                 va.reshape(bsz, seq, WIN_KV_HEADS, HEAD_DIM), sink)
    qb = qb.reshape(bsz, seq, DIFF_HEADS, 2, HEAD_DIM)
    kb = kb.reshape(bsz, seq, DIFF_HEADS, 2, HEAD_DIM)
    vb = vb.reshape(bsz, seq, DIFF_HEADS, DIFF_V_DIM)
    f32 = jnp.float32
    lam = (jnp.exp(jnp.sum(lq1.astype(f32) * lk1.astype(f32)))
           - jnp.exp(jnp.sum(lq2.astype(f32) * lk2.astype(f32))) + lam_init)
    ob = diff_attention(qb[..., 0, :], qb[..., 1, :], kb[..., 0, :], kb[..., 1, :], vb, lam, subln_g, lam_init)
    return jnp.concatenate([oa, ob], axis=-1) @ w_out


def mixer_cd(h, w_in, w_out, qnorm_g, knorm_g, rpb):
    bsz, seq, _ = h.shape
    proj = h @ w_in
    qc, kc, vc, qd, kd, vd = jnp.split(proj, _split_points(CD_SIZES), axis=-1)
    t = jnp.arange(seq)
    rows, cols = t // GRID_W, t % GRID_W
    qc = axial_rope(rms_norm(qc.reshape(bsz, seq, AX_Q_HEADS, HEAD_DIM), qnorm_g), rows, cols)
    kc = axial_rope(rms_norm(kc.reshape(bsz, seq, AX_KV_HEADS, HEAD_DIM), knorm_g), rows, cols)
    oc = dense_gqa_blocks(qc, kc, vc.reshape(bsz, seq, AX_KV_HEADS, HEAD_DIM))
    od = neighborhood_attention(qd.reshape(bsz, seq, NA_HEADS, HEAD_DIM),
                                kd.reshape(bsz, seq, NA_HEADS, HEAD_DIM),
                                vd.reshape(bsz, seq, NA_HEADS, HEAD_DIM), rpb)
    return jnp.concatenate([oc, od], axis=-1) @ w_out


def hier_moe(h, w_rg, b_rg, w_re, b_re, w_gate, w_up, w_down):
    bsz, seq, dm = h.shape
    xt = h.reshape(-1, dm)
    n_tok = xt.shape[0]
    tok = jnp.arange(n_tok)
    lg = (xt @ w_rg).astype(jnp.float32) + b_rg.astype(jnp.float32)
    g_idx = jnp.argmax(lg, axis=-1)
    p_g = jax.nn.softmax(lg, axis=-1)[tok, g_idx]
    le = ((xt @ w_re).astype(jnp.float32) + b_re.astype(jnp.float32)).reshape(n_tok, N_GROUPS, EXPERTS_PER_GROUP)
    le_sel = le[tok, g_idx]
    top_v, top_i = lax.top_k(le_sel, TOP_K_IN_GROUP)
    w_sel = jax.nn.softmax(top_v, axis=-1) * p_g[:, None]
    expert_id = g_idx[:, None] * EXPERTS_PER_GROUP + top_i
    gates = jnp.sum(jax.nn.one_hot(expert_id, N_EXPERTS, dtype=jnp.float32) * w_sel[..., None], axis=1)
    out = jnp.zeros((n_tok, dm), jnp.float32)
    for e in range(N_EXPERTS):
        y = (jax.nn.silu(xt @ w_gate[e]) * (xt @ w_up[e])) @ w_down[e]
        out = out + gates[:, e:e + 1] * y
    return out.astype(h.dtype).reshape(bsz, seq, dm)


def trunk(x, c, p):
    for l in range(DEPTH):
        ada = jax.nn.silu(c) @ p['w_ada'][l] + p['b_ada'][l]
        sh1, sc1, g1, sh2, sc2, g2 = jnp.split(ada[:, None, :], 6, axis=-1)
        i = l // 2
        h = x * (1.0 + sc1) + sh1
        if l % 2 == 0:
            lam_init = 0.8 - 0.6 * math.exp(-0.3 * l)
            m = mixer_ab(h, p['w_in_ab'][i], p['w_out_ab'][i], p['sink_a'][i],
                         p['lam_q1'][i], p['lam_k1'][i], p['lam_q2'][i], p['lam_k2'][i],
                         p['subln_g'][i], lam_init)
        else:
            m = mixer_cd(h, p['w_in_cd'][i], p['w_out_cd'][i], p['qnorm_c'][i], p['knorm_c'][i], p['rpb_d'][i])
        x = layer_norm(DEEPNORM_ALPHA * x + g1 * m, p['ln_g'][l, 0], p['ln_b'][l, 0])
        h = x * (1.0 + sc2) + sh2
        f = hier_moe(h, p['w_rg'][l], p['b_rg'][l], p['w_re'][l], p['b_re'][l],
                     p['w_gate'][l], p['w_up'][l], p['w_down'][l])
        x = layer_norm(DEEPNORM_ALPHA * x + g2 * f, p['ln_g'][l, 1], p['ln_b'][l, 1])
    return x


def setup_inputs(seed: int = 0) -> dict:
    key = jax.random.key(seed)
    ks = iter(jax.random.split(key, 40))

    def nrm(shape, s):
        return jax.random.normal(next(ks), shape, jnp.float32) * s

    d = D_MODEL
    return {
        'x_prompt': nrm((BATCH, SEQ, d), 1.0),
        'x_sample': nrm((DEC_BATCH, DEC_SEQ, d), 1.0),
        'c_prompt': nrm((BATCH, d), 1.0),
        'c_sample': nrm((DEC_BATCH, d), 1.0),
        'w_ada': nrm((DEPTH, d, 6 * d), 0.5 * d ** -0.5),
        'b_ada': nrm((DEPTH, 6 * d), 0.02),
        'ln_g': 1.0 + nrm((DEPTH, 2, d), 0.02),
        'ln_b': nrm((DEPTH, 2, d), 0.02),
        'w_in_ab': nrm((N_EVEN, d, AB_IN), d ** -0.5),
        'w_out_ab': nrm((N_EVEN, AB_OUT, d), DEEPNORM_BETA * AB_OUT ** -0.5),
        'sink_a': nrm((N_EVEN, WIN_Q_HEADS), 1.0),
        'lam_q1': nrm((N_EVEN, HEAD_DIM), 0.1),
        'lam_k1': nrm((N_EVEN, HEAD_DIM), 0.1),
        'lam_q2': nrm((N_EVEN, HEAD_DIM), 0.1),
        'lam_k2': nrm((N_EVEN, HEAD_DIM), 0.1),
        'subln_g': 1.0 + nrm((N_EVEN, DIFF_V_DIM), 0.02),
        'w_in_cd': nrm((N_ODD, d, CD_IN), d ** -0.5),
        'w_out_cd': nrm((N_ODD, CD_OUT, d), DEEPNORM_BETA * CD_OUT ** -0.5),
        'qnorm_c': 1.0 + nrm((N_ODD, HEAD_DIM), 0.02),
        'knorm_c': 1.0 + nrm((N_ODD, HEAD_DIM), 0.02),
        'rpb_d': nrm((N_ODD, NA_HEADS, 2 * NA_ROWS - 1, 2 * NA_COLS - 1), 0.1),
        'w_rg': nrm((DEPTH, d, N_GROUPS), d ** -0.5),
        'b_rg': nrm((DEPTH, N_GROUPS), 0.01),
        'w_re': nrm((DEPTH, d, N_EXPERTS), d ** -0.5),
        'b_re': nrm((DEPTH, N_EXPERTS), 0.01),
        'w_gate': nrm((DEPTH, N_EXPERTS, d, D_EXPERT), d ** -0.5),
        'w_up': nrm((DEPTH, N_EXPERTS, d, D_EXPERT), d ** -0.5),
        'w_down': nrm((DEPTH, N_EXPERTS, D_EXPERT, d), DEEPNORM_BETA * D_EXPERT ** -0.5),
    }


def reference(x_prompt, x_sample, c_prompt, c_sample, w_ada, b_ada, ln_g, ln_b,
              w_in_ab, w_out_ab, sink_a, lam_q1, lam_k1, lam_q2, lam_k2, subln_g,
              w_in_cd, w_out_cd, qnorm_c, knorm_c, rpb_d,
              w_rg, b_rg, w_re, b_re, w_gate, w_up, w_down):
    p = dict(w_ada=w_ada, b_ada=b_ada, ln_g=ln_g, ln_b=ln_b,
             w_in_ab=w_in_ab, w_out_ab=w_out_ab, sink_a=sink_a,
             lam_q1=lam_q1, lam_k1=lam_k1, lam_q2=lam_q2, lam_k2=lam_k2, subln_g=subln_g,
             w_in_cd=w_in_cd, w_out_cd=w_out_cd, qnorm_c=qnorm_c, knorm_c=knorm_c, rpb_d=rpb_d,
             w_rg=w_rg, b_rg=b_rg, w_re=w_re, b_re=b_re, w_gate=w_gate, w_up=w_up, w_down=w_down)
    y_prompt = trunk(x_prompt, c_prompt, p)
    y_sample = trunk(x_sample, c_sample, p)
    return (y_prompt, y_sample)
```

```python
import functools
import math

import numpy as np
import jax
import jax.numpy as jnp
from jax import lax
from jax.experimental import pallas as pl
from jax.experimental.pallas import tpu as pltpu

F32 = jnp.float32
BF16 = jnp.bfloat16
HIGHEST = lax.Precision.HIGHEST

D_MODEL = 2048
DEPTH = 2
HEAD_DIM = 128
WINDOW = 128
GRID_W = 64
WIN_Q_HEADS = 8
WIN_KV_HEADS = 2
GQA_GROUP = 4
DIFF_HEADS = 4
DIFF_V_DIM = 256
AX_Q_HEADS = 8
AX_KV_HEADS = 2
NA_HEADS = 8
NA_ROWS = 8
NA_COLS = 16
ROPE_THETA = 10000.0
N_GROUPS = 4
EXPERTS_PER_GROUP = 4
N_EXPERTS = 16
D_EXPERT = 512
LN_EPS = 1e-5
RMS_EPS = 1e-6
NEG_INF = -1e30
DEEPNORM_ALPHA = (2 * DEPTH) ** 0.25
PROJ_N = 4608
ATTN_SCALE = HEAD_DIM ** -0.5
LANES = 128
ROUTER_LANES = LANES
VMEM_LIMIT = 56 * 1024 * 1024

NA_TQ = NA_ROWS * GRID_W
NA_WIN = 2 * NA_ROWS * GRID_W


def _cparams(sem):
    return pltpu.CompilerParams(dimension_semantics=sem, vmem_limit_bytes=VMEM_LIMIT)


def _pick(n, cands):
    for c in cands:
        if n % c == 0:
            return c
    raise ValueError(f"no tile in {cands} divides {n}")


def _layer_norm(y, g, b):
    mu = jnp.mean(y, axis=-1, keepdims=True)
    yc = y - mu
    var = jnp.mean(yc * yc, axis=-1, keepdims=True)
    return yc * lax.rsqrt(var + LN_EPS) * g + b


def _silu(a):
    return a / (1.0 + jnp.exp(-a))


def _ada_kernel(c_ref, w_ref, b_ref, o_ref):
    a = _silu(c_ref[...])
    o_ref[0] = jnp.dot(a, w_ref[0], precision=HIGHEST, preferred_element_type=F32) + b_ref[0]


def _ada(c_all, w_ada, b_ada):
    n = 6 * D_MODEL
    tn = 1024
    return pl.pallas_call(
        _ada_kernel,
        grid=(DEPTH, n // tn),
        in_specs=[pl.BlockSpec((8, D_MODEL), lambda l, j: (0, 0)),
                  pl.BlockSpec((1, D_MODEL, tn), lambda l, j: (l, 0, j)),
                  pl.BlockSpec((1, 1, tn), lambda l, j: (l, 0, j))],
        out_specs=pl.BlockSpec((1, 8, tn), lambda l, j: (l, 0, j)),
        out_shape=jax.ShapeDtypeStruct((DEPTH, 8, n), F32),
        compiler_params=_cparams(("parallel", "parallel")),
        name="ada",
    )(c_all, w_ada, b_ada.reshape(DEPTH, 1, n))


def _proj_kernel(x_ref, sc_ref, sh_ref, w_ref, cs_ref, o_ref, h_ref):
    @pl.when(pl.program_id(2) == 0)
    def _():
        h_ref[...] = (x_ref[0] * (1.0 + sc_ref[0]) + sh_ref[0]).astype(BF16)

    acc = jnp.dot(h_ref[...], w_ref[...], preferred_element_type=F32)
    o_ref[0] = (acc * cs_ref[...]).astype(o_ref.dtype)


def _proj(x, sc, sh, w, colscale):
    bsz, seq, d = x.shape
    n = w.shape[1]
    tm = _pick(seq, (512, 256))
    tn = 1536
    return pl.pallas_call(
        _proj_kernel,
        grid=(bsz, seq // tm, n // tn),
        in_specs=[pl.BlockSpec((1, tm, d), lambda b, i, j: (b, i, 0)),
                  pl.BlockSpec((1, 1, d), lambda b, i, j: (b, 0, 0)),
                  pl.BlockSpec((1, 1, d), lambda b, i, j: (b, 0, 0)),
                  pl.BlockSpec((d, tn), lambda b, i, j: (0, j)),
                  pl.BlockSpec((1, tn), lambda b, i, j: (0, j))],
        out_specs=pl.BlockSpec((1, tm, tn), lambda b, i, j: (b, i, j)),
        out_shape=jax.ShapeDtypeStruct((bsz, seq, n), BF16),
        scratch_shapes=[pltpu.VMEM((tm, d), BF16)],
        compiler_params=_cparams(("parallel", "parallel", "arbitrary")),
        name="proj",
    )(x, sc, sh, w, colscale)


def _win_kernel(sink_ref, slope_ref, q_ref, kp_ref, km_ref, kn_ref, vp_ref, vm_ref, vn_ref, o_ref, *, tq, seq):
    i = pl.program_id(1)
    kvh = pl.program_id(2)
    k = jnp.concatenate([kp_ref[0], km_ref[0], kn_ref[0]], axis=0)
    v = jnp.concatenate([vp_ref[0], vm_ref[0], vn_ref[0]], axis=0)
    w = tq + 2 * WINDOW
    qrel = lax.broadcasted_iota(jnp.int32, (tq, w), 0)
    krel = lax.broadcasted_iota(jnp.int32, (tq, w), 1) - WINDOW
    dist = jnp.abs(qrel - krel)
    kabs = krel + i * tq
    valid = (dist <= WINDOW) & (kabs >= 0) & (kabs < seq)
    distf = dist.astype(F32)
    for g in range(GQA_GROUP):
        h = kvh * GQA_GROUP + g
        qg = q_ref[0, :, g * HEAD_DIM:(g + 1) * HEAD_DIM]
        s = lax.dot_general(qg, k, (((1,), (1,)), ((), ())), preferred_element_type=F32)
        s = jnp.where(valid, s - slope_ref[h] * distf, NEG_INF)
        snk = sink_ref[h]
        m = jnp.maximum(jnp.max(s, axis=-1, keepdims=True), snk)
        p = jnp.exp(s - m)
        l = jnp.sum(p, axis=-1, keepdims=True) + jnp.exp(snk - m)
        o = jnp.dot(p.astype(BF16), v, preferred_element_type=F32) / l
        o_ref[0, :, g * HEAD_DIM:(g + 1) * HEAD_DIM] = o.astype(o_ref.dtype)


def _win_attention(proj, sink, slopes):
    bsz, seq, _ = proj.shape
    tq = 256
    r = tq // WINDOW
    nb = seq // WINDOW
    kcol = WIN_Q_HEADS
    vcol = WIN_Q_HEADS + WIN_KV_HEADS

    def prev(col):
        return pl.BlockSpec((1, WINDOW, HEAD_DIM), lambda b, i, k: (b, jnp.maximum(i * r - 1, 0), col + k))

    def main(col):
        return pl.BlockSpec((1, tq, HEAD_DIM), lambda b, i, k: (b, i, col + k))

    def nxt(col):
        return pl.BlockSpec((1, WINDOW, HEAD_DIM), lambda b, i, k: (b, jnp.minimum((i + 1) * r, nb - 1), col + k))

    smem = pl.BlockSpec(memory_space=pltpu.SMEM)
    return pl.pallas_call(
        functools.partial(_win_kernel, tq=tq, seq=seq),
        grid=(bsz, seq // tq, WIN_KV_HEADS),
        in_specs=[smem, smem,
                  pl.BlockSpec((1, tq, GQA_GROUP * HEAD_DIM), lambda b, i, k: (b, i, k)),
                  prev(kcol), main(kcol), nxt(kcol), prev(vcol), main(vcol), nxt(vcol)],
        out_specs=pl.BlockSpec((1, tq, GQA_GROUP * HEAD_DIM), lambda b, i, k: (b, i, k)),
        out_shape=jax.ShapeDtypeStruct((bsz, seq, WIN_Q_HEADS * HEAD_DIM), BF16),
        compiler_params=_cparams(("parallel", "parallel", "parallel")),
        name="win_attn",
    )(sink, slopes, proj, proj, proj, proj, proj, proj, proj)


def _online_update(s, v, m_ref, l_ref, acc_ref):
    m_old = m_ref[...]
    m_new = jnp.maximum(m_old, jnp.max(s, axis=-1, keepdims=True))
    alpha = jnp.exp(m_old - m_new)
    p = jnp.exp(s - m_new)
    l_ref[...] = alpha * l_ref[...] + jnp.sum(p, axis=-1, keepdims=True)
    acc_ref[...] = alpha * acc_ref[...] + jnp.dot(p.astype(BF16), v, preferred_element_type=F32)
    m_ref[...] = m_new


def _diff_kernel(slope_ref, q1_ref, q2_ref, k1_ref, k2_ref, v_ref, lam_ref, g_ref, o_ref,
                 m1, l1, a1, m2, l2, a2, *, tq, tk, nk, lam_init):
    h = pl.program_id(1)
    i = pl.program_id(2)
    j = pl.program_id(3)

    @pl.when(j == 0)
    def _():
        for m, l, a in ((m1, l1, a1), (m2, l2, a2)):
            m[...] = jnp.full_like(m, NEG_INF)
            l[...] = jnp.zeros_like(l)
            a[...] = jnp.zeros_like(a)

    qpos = lax.broadcasted_iota(jnp.int32, (tq, tk), 0) + i * tq
    kpos = lax.broadcasted_iota(jnp.int32, (tq, tk), 1) + j * tk
    bias = jnp.abs(qpos - kpos).astype(F32) * (-slope_ref[h])
    dn = (((1,), (1,)), ((), ()))
    v = v_ref[0]
    s1 = lax.dot_general(q1_ref[0], k1_ref[0], dn, preferred_element_type=F32) + bias
    _online_update(s1, v, m1, l1, a1)
    s2 = lax.dot_general(q2_ref[0], k2_ref[0], dn, preferred_element_type=F32) + bias
    _online_update(s2, v, m2, l2, a2)

    @pl.when(j == nk - 1)
    def _():
        lv = lam_ref[...]
        lam = (jnp.exp(jnp.sum(lv[0:1] * lv[1:2], axis=-1, keepdims=True))
               - jnp.exp(jnp.sum(lv[2:3] * lv[3:4], axis=-1, keepdims=True)) + lam_init)
        o = a1[...] / l1[...] - lam * (a2[...] / l2[...])
        r = lax.rsqrt(jnp.mean(o * o, axis=-1, keepdims=True) + RMS_EPS)
        o_ref[0] = (o * r * g_ref[...] * (1.0 - lam_init)).astype(o_ref.dtype)


def _diff_attention(proj, slopes, lamvec, subln_g, lam_init):
    bsz, seq, _ = proj.shape
    tq = _pick(seq, (512,))
    tk = _pick(seq, (512,))
    nk = seq // tk
    qcol, kcol = 1536 // HEAD_DIM, 2560 // HEAD_DIM
    vcol = 3584 // DIFF_V_DIM
    smem = pl.BlockSpec(memory_space=pltpu.SMEM)

    def qspec(off):
        return pl.BlockSpec((1, tq, HEAD_DIM), lambda b, h, i, j: (b, i, qcol + 2 * h + off))

    def kspec(off):
        return pl.BlockSpec((1, tk, HEAD_DIM), lambda b, h, i, j: (b, j, kcol + 2 * h + off))

    return pl.pallas_call(
        functools.partial(_diff_kernel, tq=tq, tk=tk, nk=nk, lam_init=lam_init),
        grid=(bsz, DIFF_HEADS, seq // tq, nk),
        in_specs=[smem, qspec(0), qspec(1), kspec(0), kspec(1),
                  pl.BlockSpec((1, tk, DIFF_V_DIM), lambda b, h, i, j: (b, j, vcol + h)),
                  pl.BlockSpec((4, HEAD_DIM), lambda b, h, i, j: (0, 0)),
                  pl.BlockSpec((1, DIFF_V_DIM), lambda b, h, i, j: (0, 0))],
        out_specs=pl.BlockSpec((1, tq, DIFF_V_DIM), lambda b, h, i, j: (b, i, h)),
        out_shape=jax.ShapeDtypeStruct((bsz, seq, DIFF_HEADS * DIFF_V_DIM), BF16),
        scratch_shapes=[pltpu.VMEM((tq, 1), F32), pltpu.VMEM((tq, 1), F32), pltpu.VMEM((tq, DIFF_V_DIM), F32),
                        pltpu.VMEM((tq, 1), F32), pltpu.VMEM((tq, 1), F32), pltpu.VMEM((tq, DIFF_V_DIM), F32)],
        compiler_params=_cparams(("parallel", "parallel", "parallel", "arbitrary")),
        name="diff_attn",
    )(slopes, proj, proj, proj, proj, proj, lamvec, subln_g)


def _normrope_kernel(x_ref, cos_ref, sin_ref, gq_ref, gk_ref, o_ref, *, tm):
    j = pl.program_id(2)
    is_q = j < AX_Q_HEADS // 2
    g = jnp.where(is_q, gq_ref[...], gk_ref[...])
    post = jnp.where(is_q, ATTN_SCALE, 1.0).astype(F32)
    lane = lax.broadcasted_iota(jnp.int32, (tm, HEAD_DIM), 1)
    first = (lane % 64) < 32
    cos = cos_ref[...]
    sin = sin_ref[...]
    for hh in range(2):
        x = x_ref[0, :, hh * HEAD_DIM:(hh + 1) * HEAD_DIM].astype(F32)
        xn = x * lax.rsqrt(jnp.mean(x * x, axis=-1, keepdims=True) + RMS_EPS) * g
        partner = jnp.where(first, pltpu.roll(xn, 96, 1), pltpu.roll(xn, 32, 1))
        y = (xn * cos + partner * sin) * post
        o_ref[0, :, hh * HEAD_DIM:(hh + 1) * HEAD_DIM] = y.astype(o_ref.dtype)


def _rope_tables(seq):
    half = HEAD_DIM // 2
    inv = jnp.asarray(1.0 / (ROPE_THETA ** (np.arange(0, half, 2) / half)), dtype=F32)
    t = jnp.arange(seq)
    ang_r = (t // GRID_W).astype(F32)[:, None] * inv[None, :]
    ang_c = (t % GRID_W).astype(F32)[:, None] * inv[None, :]
    cr, sr, cc, sn = jnp.cos(ang_r), jnp.sin(ang_r), jnp.cos(ang_c), jnp.sin(ang_c)
    return (jnp.concatenate([cr, cr, cc, cc], axis=-1), jnp.concatenate([-sr, sr, -sn, sn], axis=-1))


def _normrope(proj, qnorm_g, knorm_g):
    bsz, seq, _ = proj.shape
    tm = _pick(seq, (512,))
    cos, sin = _rope_tables(seq)
    nw = (AX_Q_HEADS + AX_KV_HEADS) * HEAD_DIM
    return pl.pallas_call(
        functools.partial(_normrope_kernel, tm=tm),
        grid=(bsz, seq // tm, nw // 256),
        in_specs=[pl.BlockSpec((1, tm, 256), lambda b, i, j: (b, i, j)),
                  pl.BlockSpec((tm, HEAD_DIM), lambda b, i, j: (i, 0)),
                  pl.BlockSpec((tm, HEAD_DIM), lambda b, i, j: (i, 0)),
                  pl.BlockSpec((1, HEAD_DIM), lambda b, i, j: (0, 0)),
                  pl.BlockSpec((1, HEAD_DIM), lambda b, i, j: (0, 0))],
        out_specs=pl.BlockSpec((1, tm, 256), lambda b, i, j: (b, i, j)),
        out_shape=jax.ShapeDtypeStruct((bsz, seq, nw), BF16),
        compiler_params=_cparams(("parallel", "parallel", "parallel")),
        name="normrope",
    )(proj, cos, sin, qnorm_g.reshape(1, HEAD_DIM), knorm_g.reshape(1, HEAD_DIM))


def _gqa_kernel(q_ref, k_ref, v_ref, o_ref, qs, m_ref, l_ref, acc_ref, *, tq, nk):
    j = pl.program_id(3)

    @pl.when(j == 0)
    def _():
        for g in range(GQA_GROUP):
            qs[g * tq:(g + 1) * tq, :] = q_ref[0, :, g * HEAD_DIM:(g + 1) * HEAD_DIM]
        m_ref[...] = jnp.full_like(m_ref, NEG_INF)
        l_ref[...] = jnp.zeros_like(l_ref)
        acc_ref[...] = jnp.zeros_like(acc_ref)

    s = lax.dot_general(qs[...], k_ref[0], (((1,), (1,)), ((), ())), preferred_element_type=F32)
    _online_update(s, v_ref[0], m_ref, l_ref, acc_ref)

    @pl.when(j == nk - 1)
    def _():
        o = acc_ref[...] / l_ref[...]
        for g in range(GQA_GROUP):
            o_ref[0, :, g * HEAD_DIM:(g + 1) * HEAD_DIM] = o[g * tq:(g + 1) * tq].astype(o_ref.dtype)


def _gqa_attention(qk, proj):
    bsz, seq, _ = qk.shape
    tq = 256
    tk = _pick(seq, (512,))
    nk = seq // tk
    gw = GQA_GROUP * HEAD_DIM
    vcol = 1280 // HEAD_DIM
    return pl.pallas_call(
        functools.partial(_gqa_kernel, tq=tq, nk=nk),
        grid=(bsz, AX_KV_HEADS, seq // tq, nk),
        in_specs=[pl.BlockSpec((1, tq, gw), lambda b, h, i, j: (b, i, h)),
                  pl.BlockSpec((1, tk, HEAD_DIM), lambda b, h, i, j: (b, j, AX_Q_HEADS + h)),
                  pl.BlockSpec((1, tk, HEAD_DIM), lambda b, h, i, j: (b, j, vcol + h))],
        out_specs=pl.BlockSpec((1, tq, gw), lambda b, h, i, j: (b, i, h)),
        out_shape=jax.ShapeDtypeStruct((bsz, seq, AX_Q_HEADS * HEAD_DIM), BF16),
        scratch_shapes=[pltpu.VMEM((GQA_GROUP * tq, HEAD_DIM), BF16),
                        pltpu.VMEM((GQA_GROUP * tq, 1), F32), pltpu.VMEM((GQA_GROUP * tq, 1), F32),
                        pltpu.VMEM((GQA_GROUP * tq, HEAD_DIM), F32)],
        compiler_params=_cparams(("parallel", "parallel", "parallel", "arbitrary")),
        name="gqa_attn",
    )(qk, qk, proj)


def _na_bias_table(rpb):
    qi = np.arange(NA_ROWS)[:, None, None, None]
    qc = np.arange(GRID_W)[None, :, None, None]
    kj = np.arange(2 * NA_ROWS)[None, None, :, None]
    kc = np.arange(GRID_W)[None, None, None, :]
    c_start = np.clip(qc - NA_COLS // 2, 0, GRID_W - NA_COLS)
    col_ok = (kc >= c_start) & (kc < c_start + NA_COLS)
    rel_c = np.clip(kc - qc + NA_COLS - 1, 0, 2 * NA_COLS - 2)
    shape = (NA_ROWS, GRID_W, 2 * NA_ROWS, GRID_W)
    tabs = []
    for off, clamp in ((0, np.maximum), (-NA_ROWS // 2, None), (-NA_ROWS, np.minimum)):
        rs = qi - NA_ROWS // 2
        if clamp is not None:
            rs = clamp(rs, 0)
        kr = off + kj
        ok = np.broadcast_to((kr >= rs) & (kr < rs + NA_ROWS) & col_ok, shape)
        rel_r = np.broadcast_to(np.clip(kr - qi + NA_ROWS - 1, 0, 2 * NA_ROWS - 2), shape)
        bias = rpb.astype(F32)[:, rel_r, np.broadcast_to(rel_c, shape)]
        tabs.append(jnp.where(ok[None], bias, NEG_INF).reshape(NA_HEADS, NA_TQ, NA_WIN))
    return jnp.stack(tabs)


def _na_kernel(q_ref, k_ref, v_ref, bias_ref, o_ref, *, rows):
    i = pl.program_id(2)
    ws = jnp.clip(i * NA_ROWS - NA_ROWS // 2, 0, rows - 2 * NA_ROWS)
    start = pl.multiple_of(ws * GRID_W, GRID_W)
    k = k_ref[0, pl.ds(start, NA_WIN), :]
    v = v_ref[0, pl.ds(start, NA_WIN), :]
    s = lax.dot_general(q_ref[0], k, (((1,), (1,)), ((), ())), preferred_element_type=F32) + bias_ref[0, 0]
    m = jnp.max(s, axis=-1, keepdims=True)
    p = jnp.exp(s - m)
    l = jnp.sum(p, axis=-1, keepdims=True)
    o_ref[0] = (jnp.dot(p.astype(BF16), v, preferred_element_type=F32) / l).astype(o_ref.dtype)


def _na_attention(proj, bias_tab):
    bsz, seq, _ = proj.shape
    rows = seq // GRID_W
    nq = seq // NA_TQ
    assert rows >= 3 * NA_ROWS and seq % NA_TQ == 0
    qcol, kcol, vcol = 1536 // HEAD_DIM, 2560 // HEAD_DIM, 3584 // HEAD_DIM

    def case(i):
        return jnp.where(i == 0, 0, jnp.where(i == nq - 1, 2, 1))

    return pl.pallas_call(
        functools.partial(_na_kernel, rows=rows),
        grid=(bsz, NA_HEADS, nq),
        in_specs=[pl.BlockSpec((1, NA_TQ, HEAD_DIM), lambda b, h, i: (b, i, qcol + h)),
                  pl.BlockSpec((1, seq, HEAD_DIM), lambda b, h, i: (b, 0, kcol + h)),
                  pl.BlockSpec((1, seq, HEAD_DIM), lambda b, h, i: (b, 0, vcol + h)),
                  pl.BlockSpec((1, 1, NA_TQ, NA_WIN), lambda b, h, i: (case(i), h, 0, 0))],
        out_specs=pl.BlockSpec((1, NA_TQ, HEAD_DIM), lambda b, h, i: (b, i, h)),
        out_shape=jax.ShapeDtypeStruct((bsz, seq, NA_HEADS * HEAD_DIM), BF16),
        compiler_params=_cparams(("parallel", "parallel", "arbitrary")),
        name="na_attn",
    )(proj, proj, proj, bias_tab)


def _outproj_kernel(oa_ref, ob_ref, wa_ref, wb_ref, x_ref, g_ref, lng_ref, lnb_ref, o_ref):
    m = (jnp.dot(oa_ref[0], wa_ref[...], preferred_element_type=F32)
         + jnp.dot(ob_ref[0], wb_ref[...], preferred_element_type=F32))
    y = DEEPNORM_ALPHA * x_ref[0] + g_ref[0] * m
    o_ref[0] = _layer_norm(y, lng_ref[...], lnb_ref[...])


def _outproj_ln(oa, ob, w_out, x, gate, ln_g, ln_b):
    bsz, seq, d = x.shape
    half = oa.shape[-1]
    tm = _pick(seq, (512, 256))
    vec = pl.BlockSpec((1, d), lambda b, i: (0, 0))
    return pl.pallas_call(
        _outproj_kernel,
        grid=(bsz, seq // tm),
        in_specs=[pl.BlockSpec((1, tm, half), lambda b, i: (b, i, 0)),
                  pl.BlockSpec((1, tm, half), lambda b, i: (b, i, 0)),
                  pl.BlockSpec((half, d), lambda b, i: (0, 0)),
                  pl.BlockSpec((half, d), lambda b, i: (1, 0)),
                  pl.BlockSpec((1, tm, d), lambda b, i: (b, i, 0)),
                  pl.BlockSpec((1, 1, d), lambda b, i: (b, 0, 0)),
                  vec, vec],
        out_specs=pl.BlockSpec((1, tm, d), lambda b, i: (b, i, 0)),
        out_shape=jax.ShapeDtypeStruct((bsz, seq, d), F32),
        compiler_params=_cparams(("parallel", "parallel")),
        name="outproj_ln",
    )(oa, ob, w_out, w_out, x, gate, ln_g.reshape(1, d), ln_b.reshape(1, d))


def _route(logits):
    lane = lax.broadcasted_iota(jnp.int32, logits.shape, 1)
    big = jnp.int32(2 ** 30)
    is_g = lane < N_GROUPS
    mg = jnp.max(jnp.where(is_g, logits, -jnp.inf), axis=-1, keepdims=True)
    g_idx = jnp.min(jnp.where(is_g & (logits == mg), lane, big), axis=-1, keepdims=True)
    p_g = 1.0 / jnp.sum(jnp.where(is_g, jnp.exp(logits - mg), 0.0), axis=-1, keepdims=True)
    lo = N_GROUPS + EXPERTS_PER_GROUP * g_idx
    in_grp = (lane >= lo) & (lane < lo + EXPERTS_PER_GROUP)
    v1 = jnp.max(jnp.where(in_grp, logits, -jnp.inf), axis=-1, keepdims=True)
    i1 = jnp.min(jnp.where(in_grp & (logits == v1), lane, big), axis=-1, keepdims=True)
    rest = in_grp & (lane != i1)
    v2 = jnp.max(jnp.where(rest, logits, -jnp.inf), axis=-1, keepdims=True)
    i2 = jnp.min(jnp.where(rest & (logits == v2), lane, big), axis=-1, keepdims=True)
    e2 = jnp.exp(v2 - v1)
    w1 = p_g / (1.0 + e2)
    w2 = p_g * e2 / (1.0 + e2)
    return jnp.where(lane == i1, w1, 0.0) + jnp.where(lane == i2, w2, 0.0)


def _moe_kernel(x_ref, sc_ref, sh_ref, g2_ref, wr_ref, br_ref, wg_ref, wu_ref, wd_ref, lng_ref, lnb_ref,
                o_ref, h_ref, gate_ref, acc_ref):
    e = pl.program_id(2)

    @pl.when(e == 0)
    def _():
        h = x_ref[0] * (1.0 + sc_ref[0]) + sh_ref[0]
        h_ref[...] = h.astype(BF16)
        logits = jnp.dot(h, wr_ref[...], precision=HIGHEST, preferred_element_type=F32) + br_ref[...]
        gate_ref[...] = _route(logits)
        acc_ref[...] = jnp.zeros_like(acc_ref)

    hb = h_ref[...]
    a = jnp.dot(hb, wg_ref[0], preferred_element_type=F32)
    u = jnp.dot(hb, wu_ref[0], preferred_element_type=F32)
    lane = lax.broadcasted_iota(jnp.int32, gate_ref.shape, 1)
    ge = jnp.sum(jnp.where(lane == e + N_GROUPS, gate_ref[...], 0.0), axis=-1, keepdims=True)
    act = (_silu(a) * u * ge).astype(BF16)
    acc_ref[...] += jnp.dot(act, wd_ref[0], preferred_element_type=F32)

    @pl.when(e == N_EXPERTS - 1)
    def _():
        y = DEEPNORM_ALPHA * x_ref[0] + g2_ref[0] * acc_ref[...]
        o_ref[0] = _layer_norm(y, lng_ref[...], lnb_ref[...])


def _moe_ln(x, sc, sh, gate, w_router, b_router, w_gate, w_up, w_down, ln_g, ln_b):
    bsz, seq, d = x.shape
    tm = _pick(seq, (512, 256))
    vec = pl.BlockSpec((1, d), lambda b, i, e: (0, 0))
    mod = pl.BlockSpec((1, 1, d), lambda b, i, e: (b, 0, 0))
    return pl.pallas_call(
        _moe_kernel,
        grid=(bsz, seq // tm, N_EXPERTS),
        in_specs=[pl.BlockSpec((1, tm, d), lambda b, i, e: (b, i, 0)), mod, mod, mod,
                  pl.BlockSpec((d, ROUTER_LANES), lambda b, i, e: (0, 0)),
                  pl.BlockSpec((1, ROUTER_LANES), lambda b, i, e: (0, 0)),
                  pl.BlockSpec((1, d, D_EXPERT), lambda b, i, e: (e, 0, 0)),
                  pl.BlockSpec((1, d, D_EXPERT), lambda b, i, e: (e, 0, 0)),
                  pl.BlockSpec((1, D_EXPERT, d), lambda b, i, e: (e, 0, 0)),
                  vec, vec],
        out_specs=pl.BlockSpec((1, tm, d), lambda b, i, e: (b, i, 0)),
        out_shape=jax.ShapeDtypeStruct((bsz, seq, d), F32),
        scratch_shapes=[pltpu.VMEM((tm, d), BF16), pltpu.VMEM((tm, ROUTER_LANES), F32), pltpu.VMEM((tm, d), F32)],
        compiler_params=_cparams(("parallel", "parallel", "arbitrary")),
        name="moe_ln",
    )(x, sc, sh, gate, w_router, b_router, w_gate, w_up, w_down, ln_g.reshape(1, d), ln_b.reshape(1, d))


def _alibi_slopes(n):
    return jnp.asarray(2.0 ** (-8.0 * np.arange(1, n + 1) / n), dtype=F32)


def _colscale(q_ranges):
    cs = np.ones((1, PROJ_N), np.float32)
    for lo, hi in q_ranges:
        cs[:, lo:hi] = ATTN_SCALE
    return jnp.asarray(cs)


def _prep_params(p):
    q = dict(p)
    for name in ("w_in_ab", "w_out_ab", "w_in_cd", "w_out_cd", "w_gate", "w_up", "w_down"):
        q[name] = p[name].astype(BF16)
    pad = ROUTER_LANES - N_GROUPS - N_EXPERTS
    q["w_router"] = jnp.pad(jnp.concatenate([p["w_rg"], p["w_re"]], axis=-1), ((0, 0), (0, 0), (0, pad)))
    q["b_router"] = jnp.pad(jnp.concatenate([p["b_rg"], p["b_re"]], axis=-1), ((0, 0), (0, pad)))[:, None, :]
    q["na_bias"] = [_na_bias_table(p["rpb_d"][i]) for i in range(p["rpb_d"].shape[0])]
    return q


def _trunk(x, ada, p):
    for l in range(DEPTH):
        sh1, sc1, g1, sh2, sc2, g2 = jnp.split(ada[l][:, None, :], 6, axis=-1)
        i = l // 2
        if l % 2 == 0:
            lam_init = 0.8 - 0.6 * math.exp(-0.3 * l)
            proj = _proj(x, sc1, sh1, p["w_in_ab"][i], _colscale(((0, 1024), (1536, 2560))))
            oa = _win_attention(proj, p["sink_a"][i], _alibi_slopes(WIN_Q_HEADS))
            lamvec = jnp.stack([p["lam_q1"][i], p["lam_k1"][i], p["lam_q2"][i], p["lam_k2"][i]])
            ob = _diff_attention(proj, _alibi_slopes(DIFF_HEADS), lamvec, p["subln_g"][i][None, :], lam_init)
            w_out = p["w_out_ab"][i]
        else:
            proj = _proj(x, sc1, sh1, p["w_in_cd"][i], _colscale(((1536, 2560),)))
            qk = _normrope(proj, p["qnorm_c"][i], p["knorm_c"][i])
            oa = _gqa_attention(qk, proj)
            ob = _na_attention(proj, p["na_bias"][i])
            w_out = p["w_out_cd"][i]
        x = _outproj_ln(oa, ob, w_out, x, g1, p["ln_g"][l, 0], p["ln_b"][l, 0])
        x = _moe_ln(x, sc2, sh2, g2, p["w_router"][l], p["b_router"][l],
                    p["w_gate"][l], p["w_up"][l], p["w_down"][l], p["ln_g"][l, 1], p["ln_b"][l, 1])
    return x


def kernel(x_prompt, x_sample, c_prompt, c_sample, w_ada, b_ada, ln_g, ln_b, w_in_ab, w_out_ab, sink_a,
           lam_q1, lam_k1, lam_q2, lam_k2, subln_g, w_in_cd, w_out_cd, qnorm_c, knorm_c, rpb_d,
           w_rg, b_rg, w_re, b_re, w_gate, w_up, w_down):
    p = _prep_params(dict(
        ln_g=ln_g, ln_b=ln_b, w_in_ab=w_in_ab, w_out_ab=w_out_ab, sink_a=sink_a,
        lam_q1=lam_q1, lam_k1=lam_k1, lam_q2=lam_q2, lam_k2=lam_k2, subln_g=subln_g,
        w_in_cd=w_in_cd, w_out_cd=w_out_cd, qnorm_c=qnorm_c, knorm_c=knorm_c, rpb_d=rpb_d,
        w_rg=w_rg, b_rg=b_rg, w_re=w_re, b_re=b_re, w_gate=w_gate, w_up=w_up, w_down=w_down))
    nb_p, nb_s = c_prompt.shape[0], c_sample.shape[0]
    c_all = jnp.concatenate([c_prompt, c_sample, jnp.zeros((8 - nb_p - nb_s, D_MODEL), F32)], axis=0)
    ada = _ada(c_all, w_ada, b_ada)
    y_prompt = _trunk(x_prompt, ada[:, :nb_p], p)
    y_sample = _trunk(x_sample, ada[:, nb_p:nb_p + nb_s], p)
    return (y_prompt, y_sample)
```

```python
import functools
import math

import numpy as np
import jax
import jax.numpy as jnp
from jax import lax
from jax.experimental import pallas as pl
from jax.experimental.pallas import tpu as pltpu

F32 = jnp.float32
BF16 = jnp.bfloat16
HIGHEST = lax.Precision.HIGHEST

D_MODEL = 2048
DEPTH = 2
HEAD_DIM = 128
WINDOW = 128
GRID_W = 64
WIN_Q_HEADS = 8
WIN_KV_HEADS = 2
GQA_GROUP = 4
DIFF_HEADS = 4
DIFF_V_DIM = 256
AX_Q_HEADS = 8
AX_KV_HEADS = 2
NA_HEADS = 8
NA_ROWS = 8
NA_COLS = 16
ROPE_THETA = 10000.0
N_GROUPS = 4
EXPERTS_PER_GROUP = 4
N_EXPERTS = 16
D_EXPERT = 512
LN_EPS = 1e-5
RMS_EPS = 1e-6
NEG_INF = -1e30
DEEPNORM_ALPHA = (2 * DEPTH) ** 0.25
PROJ_N = 4608
ATTN_SCALE = HEAD_DIM ** -0.5
LOG2E = math.log2(math.e)
LANES = 128
ROUTER_LANES = LANES
VMEM_LIMIT = 56 * 1024 * 1024

NA_TQ = NA_ROWS * GRID_W
NA_WIN = 2 * NA_ROWS * GRID_W


def _cparams(sem):
    return pltpu.CompilerParams(dimension_semantics=sem, vmem_limit_bytes=VMEM_LIMIT)


def _pick(n, cands):
    for c in cands:
        if n % c == 0:
            return c
    raise ValueError(f"no tile in {cands} divides {n}")


def _layer_norm(y, g, b):
    mu = jnp.mean(y, axis=-1, keepdims=True)
    yc = y - mu
    var = jnp.mean(yc * yc, axis=-1, keepdims=True)
    return yc * lax.rsqrt(var + LN_EPS) * g + b


def _silu(a):
    return a / (1.0 + jnp.exp(-a))


def _ada_kernel(c_ref, w_ref, b_ref, o_ref):
    a = _silu(c_ref[...])
    o_ref[0] = jnp.dot(a, w_ref[0], precision=HIGHEST, preferred_element_type=F32) + b_ref[0]


def _ada(c_all, w_ada, b_ada):
    n = 6 * D_MODEL
    tn = 1024
    return pl.pallas_call(
        _ada_kernel,
        grid=(DEPTH, n // tn),
        in_specs=[pl.BlockSpec((8, D_MODEL), lambda l, j: (0, 0)),
                  pl.BlockSpec((1, D_MODEL, tn), lambda l, j: (l, 0, j)),
                  pl.BlockSpec((1, 1, tn), lambda l, j: (l, 0, j))],
        out_specs=pl.BlockSpec((1, 8, tn), lambda l, j: (l, 0, j)),
        out_shape=jax.ShapeDtypeStruct((DEPTH, 8, n), F32),
        compiler_params=_cparams(("parallel", "parallel")),
        name="ada",
    )(c_all, w_ada, b_ada.reshape(DEPTH, 1, n))


def _proj_kernel(x_ref, sc_ref, sh_ref, w_ref, cs_ref, o_ref, h_ref):
    @pl.when(pl.program_id(2) == 0)
    def _():
        h_ref[...] = (x_ref[0] * (1.0 + sc_ref[0]) + sh_ref[0]).astype(BF16)

    acc = jnp.dot(h_ref[...], w_ref[...], preferred_element_type=F32)
    o_ref[0] = (acc * cs_ref[...]).astype(o_ref.dtype)


def _proj(x, sc, sh, w, colscale):
    bsz, seq, d = x.shape
    n = w.shape[1]
    tm = _pick(seq, (512, 256))
    tn = 1536
    return pl.pallas_call(
        _proj_kernel,
        grid=(bsz, seq // tm, n // tn),
        in_specs=[pl.BlockSpec((1, tm, d), lambda b, i, j: (b, i, 0)),
                  pl.BlockSpec((1, 1, d), lambda b, i, j: (b, 0, 0)),
                  pl.BlockSpec((1, 1, d), lambda b, i, j: (b, 0, 0)),
                  pl.BlockSpec((d, tn), lambda b, i, j: (0, j)),
                  pl.BlockSpec((1, tn), lambda b, i, j: (0, j))],
        out_specs=pl.BlockSpec((1, tm, tn), lambda b, i, j: (b, i, j)),
        out_shape=jax.ShapeDtypeStruct((bsz, seq, n), BF16),
        scratch_shapes=[pltpu.VMEM((tm, d), BF16)],
        compiler_params=_cparams(("parallel", "parallel", "arbitrary")),
        name="proj",
    )(x, sc, sh, w, colscale)


def _win_kernel(sink_ref, slope_ref, q_ref, kp_ref, km_ref, kn_ref, vp_ref, vm_ref, vn_ref, o_ref, *, tq, seq):
    i = pl.program_id(1)
    kvh = pl.program_id(2)
    k = jnp.concatenate([kp_ref[0], km_ref[0], kn_ref[0]], axis=0)
    v = jnp.concatenate([vp_ref[0], vm_ref[0], vn_ref[0]], axis=0)
    w = tq + 2 * WINDOW
    qrel = lax.broadcasted_iota(jnp.int32, (tq, w), 0)
    krel = lax.broadcasted_iota(jnp.int32, (tq, w), 1) - WINDOW
    dist = jnp.abs(qrel - krel)
    kabs = krel + i * tq
    valid = (dist <= WINDOW) & (kabs >= 0) & (kabs < seq)
    distf = dist.astype(F32)
    for g in range(GQA_GROUP):
        h = kvh * GQA_GROUP + g
        qg = q_ref[0, :, g * HEAD_DIM:(g + 1) * HEAD_DIM]
        s = lax.dot_general(qg, k, (((1,), (1,)), ((), ())), preferred_element_type=F32)
        s = jnp.where(valid, s - slope_ref[h] * distf, NEG_INF)
        snk = sink_ref[h]
        m = jnp.maximum(jnp.max(s, axis=-1, keepdims=True), snk)
        p = jnp.exp(s - m)
        l = jnp.sum(p, axis=-1, keepdims=True) + jnp.exp(snk - m)
        o = jnp.dot(p.astype(BF16), v, preferred_element_type=F32) / l
        o_ref[0, :, g * HEAD_DIM:(g + 1) * HEAD_DIM] = o.astype(o_ref.dtype)


def _win_attention(proj, sink, slopes):
    bsz, seq, _ = proj.shape
    tq = 256
    r = tq // WINDOW
    nb = seq // WINDOW
    kcol = WIN_Q_HEADS
    vcol = WIN_Q_HEADS + WIN_KV_HEADS

    def prev(col):
        return pl.BlockSpec((1, WINDOW, HEAD_DIM), lambda b, i, k: (b, jnp.maximum(i * r - 1, 0), col + k))

    def main(col):
        return pl.BlockSpec((1, tq, HEAD_DIM), lambda b, i, k: (b, i, col + k))

    def nxt(col):
        return pl.BlockSpec((1, WINDOW, HEAD_DIM), lambda b, i, k: (b, jnp.minimum((i + 1) * r, nb - 1), col + k))

    smem = pl.BlockSpec(memory_space=pltpu.SMEM)
    return pl.pallas_call(
        functools.partial(_win_kernel, tq=tq, seq=seq),
        grid=(bsz, seq // tq, WIN_KV_HEADS),
        in_specs=[smem, smem,
                  pl.BlockSpec((1, tq, GQA_GROUP * HEAD_DIM), lambda b, i, k: (b, i, k)),
                  prev(kcol), main(kcol), nxt(kcol), prev(vcol), main(vcol), nxt(vcol)],
        out_specs=pl.BlockSpec((1, tq, GQA_GROUP * HEAD_DIM), lambda b, i, k: (b, i, k)),
        out_shape=jax.ShapeDtypeStruct((bsz, seq, WIN_Q_HEADS * HEAD_DIM), BF16),
        compiler_params=_cparams(("parallel", "parallel", "parallel")),
        name="win_attn",
    )(sink, slopes, proj, proj, proj, proj, proj, proj, proj)


def _online_update_t(s, vt, m_ref, l_ref, acc_ref, idx):
    m_old = m_ref[idx]
    m_new = jnp.maximum(m_old, jnp.max(s, axis=0, keepdims=True))
    alpha = jnp.exp2(m_old - m_new)
    p = jnp.exp2(s - m_new)
    l_ref[idx] = alpha * l_ref[idx] + jnp.sum(p, axis=0, keepdims=True)
    acc_ref[idx] = alpha * acc_ref[idx] + jnp.dot(vt, p.astype(BF16), preferred_element_type=F32)
    m_ref[idx] = m_new


def _diff_kernel(slope_ref, q_ref, k1_ref, k2_ref, vt_ref, lam_ref, g_ref, o_ref, m_ref, l_ref, acc_ref,
                 *, tq, tk, cq, nk, lam_init):
    h = pl.program_id(1)
    i = pl.program_id(2)
    j = pl.program_id(3)

    @pl.when(j == 0)
    def _():
        m_ref[...] = jnp.full_like(m_ref, NEG_INF)
        l_ref[...] = jnp.zeros_like(l_ref)
        acc_ref[...] = jnp.zeros_like(acc_ref)

    neg_slope = -slope_ref[h] * LOG2E
    dbase = (lax.broadcasted_iota(jnp.int32, (tk, cq), 1) - lax.broadcasted_iota(jnp.int32, (tk, cq), 0)).astype(F32)
    vt = vt_ref[0, 0]
    for c in range(tq // cq):
        cols = slice(c * cq, (c + 1) * cq)
        off = (i * tq + c * cq - j * tk).astype(F32)
        bias = jnp.abs(dbase + off) * neg_slope
        for mi, k_ref in enumerate((k1_ref, k2_ref)):
            s = jnp.dot(k_ref[0], q_ref[0, 0, mi, :, cols], preferred_element_type=F32) + bias
            _online_update_t(s, vt, m_ref, l_ref, acc_ref, (mi, slice(None), cols))

    @pl.when(j == nk - 1)
    def _():
        lv = lam_ref[...]
        lam = (jnp.exp(jnp.sum(lv[0:1] * lv[1:2], axis=-1, keepdims=True))
               - jnp.exp(jnp.sum(lv[2:3] * lv[3:4], axis=-1, keepdims=True)) + lam_init)
        ot = acc_ref[0] / l_ref[0] - lam * (acc_ref[1] / l_ref[1])
        o = ot.T
        r = lax.rsqrt(jnp.mean(o * o, axis=-1, keepdims=True) + RMS_EPS)
        o_ref[0] = (o * r * g_ref[...] * (1.0 - lam_init)).astype(o_ref.dtype)


def _diff_attention(proj, slopes, lamvec, subln_g, lam_init):
    bsz, seq, _ = proj.shape
    tq = _pick(seq, (1024, 512))
    tk = _pick(seq, (1024, 512))
    cq = 256
    nk = seq // tk
    kcol = 2560 // HEAD_DIM
    qt = proj[:, :, 1536:2560].reshape(bsz, seq, DIFF_HEADS, 2, HEAD_DIM).transpose(0, 2, 3, 4, 1)
    vt = proj[:, :, 3584:4608].reshape(bsz, seq, DIFF_HEADS, DIFF_V_DIM).transpose(0, 2, 3, 1)
    smem = pl.BlockSpec(memory_space=pltpu.SMEM)

    def kspec(off):
        return pl.BlockSpec((1, tk, HEAD_DIM), lambda b, h, i, j: (b, j, kcol + 2 * h + off))

    return pl.pallas_call(
        functools.partial(_diff_kernel, tq=tq, tk=tk, cq=cq, nk=nk, lam_init=lam_init),
        grid=(bsz, DIFF_HEADS, seq // tq, nk),
        in_specs=[smem,
                  pl.BlockSpec((1, 1, 2, HEAD_DIM, tq), lambda b, h, i, j: (b, h, 0, 0, i)),
                  kspec(0), kspec(1),
                  pl.BlockSpec((1, 1, DIFF_V_DIM, tk), lambda b, h, i, j: (b, h, 0, j)),
                  pl.BlockSpec((4, HEAD_DIM), lambda b, h, i, j: (0, 0)),
                  pl.BlockSpec((1, DIFF_V_DIM), lambda b, h, i, j: (0, 0))],
        out_specs=pl.BlockSpec((1, tq, DIFF_V_DIM), lambda b, h, i, j: (b, i, h)),
        out_shape=jax.ShapeDtypeStruct((bsz, seq, DIFF_HEADS * DIFF_V_DIM), BF16),
        scratch_shapes=[pltpu.VMEM((2, 1, tq), F32), pltpu.VMEM((2, 1, tq), F32),
                        pltpu.VMEM((2, DIFF_V_DIM, tq), F32)],
        compiler_params=_cparams(("parallel", "parallel", "parallel", "arbitrary")),
        name="diff_attn",
    )(slopes, qt, proj, proj, vt, lamvec, subln_g)


def _normrope_kernel(x_ref, cos_ref, sin_ref, gq_ref, gk_ref, o_ref, *, tm):
    j = pl.program_id(2)
    is_q = j < AX_Q_HEADS // 2
    g = jnp.where(is_q, gq_ref[...], gk_ref[...])
    post = jnp.where(is_q, ATTN_SCALE * LOG2E, 1.0).astype(F32)
    lane = lax.broadcasted_iota(jnp.int32, (tm, HEAD_DIM), 1)
    first = (lane % 64) < 32
    cos = cos_ref[...]
    sin = sin_ref[...]
    for hh in range(2):
        x = x_ref[0, :, hh * HEAD_DIM:(hh + 1) * HEAD_DIM].astype(F32)
        xn = x * lax.rsqrt(jnp.mean(x * x, axis=-1, keepdims=True) + RMS_EPS) * g
        partner = jnp.where(first, pltpu.roll(xn, 96, 1), pltpu.roll(xn, 32, 1))
        y = (xn * cos + partner * sin) * post
        o_ref[0, :, hh * HEAD_DIM:(hh + 1) * HEAD_DIM] = y.astype(o_ref.dtype)


def _rope_tables(seq):
    half = HEAD_DIM // 2
    inv = jnp.asarray(1.0 / (ROPE_THETA ** (np.arange(0, half, 2) / half)), dtype=F32)
    t = jnp.arange(seq)
    ang_r = (t // GRID_W).astype(F32)[:, None] * inv[None, :]
    ang_c = (t % GRID_W).astype(F32)[:, None] * inv[None, :]
    cr, sr, cc, sn = jnp.cos(ang_r), jnp.sin(ang_r), jnp.cos(ang_c), jnp.sin(ang_c)
    return (jnp.concatenate([cr, cr, cc, cc], axis=-1), jnp.concatenate([-sr, sr, -sn, sn], axis=-1))


def _normrope(proj, qnorm_g, knorm_g):
    bsz, seq, _ = proj.shape
    tm = _pick(seq, (512,))
    cos, sin = _rope_tables(seq)
    nw = (AX_Q_HEADS + AX_KV_HEADS) * HEAD_DIM
    return pl.pallas_call(
        functools.partial(_normrope_kernel, tm=tm),
        grid=(bsz, seq // tm, nw // 256),
        in_specs=[pl.BlockSpec((1, tm, 256), lambda b, i, j: (b, i, j)),
                  pl.BlockSpec((tm, HEAD_DIM), lambda b, i, j: (i, 0)),
                  pl.BlockSpec((tm, HEAD_DIM), lambda b, i, j: (i, 0)),
                  pl.BlockSpec((1, HEAD_DIM), lambda b, i, j: (0, 0)),
                  pl.BlockSpec((1, HEAD_DIM), lambda b, i, j: (0, 0))],
        out_specs=pl.BlockSpec((1, tm, 256), lambda b, i, j: (b, i, j)),
        out_shape=jax.ShapeDtypeStruct((bsz, seq, nw), BF16),
        compiler_params=_cparams(("parallel", "parallel", "parallel")),
        name="normrope",
    )(proj, cos, sin, qnorm_g.reshape(1, HEAD_DIM), knorm_g.reshape(1, HEAD_DIM))


def _gqa_kernel(q_ref, k_ref, vt_ref, o_ref, m_ref, l_ref, acc_ref, *, tq, cq, nk):
    j = pl.program_id(3)

    @pl.when(j == 0)
    def _():
        m_ref[...] = jnp.full_like(m_ref, NEG_INF)
        l_ref[...] = jnp.zeros_like(l_ref)
        acc_ref[...] = jnp.zeros_like(acc_ref)

    k = k_ref[0]
    vt = vt_ref[0, 0]
    per = tq // cq
    for c in range(GQA_GROUP * per):
        g, sub = divmod(c, per)
        s = jnp.dot(k, q_ref[0, g, :, sub * cq:(sub + 1) * cq], preferred_element_type=F32)
        _online_update_t(s, vt, m_ref, l_ref, acc_ref, (slice(None), slice(c * cq, (c + 1) * cq)))

    @pl.when(j == nk - 1)
    def _():
        ot = acc_ref[...] / l_ref[...]
        for g in range(GQA_GROUP):
            o_ref[0, :, g * HEAD_DIM:(g + 1) * HEAD_DIM] = ot[:, g * tq:(g + 1) * tq].T.astype(o_ref.dtype)


def _gqa_attention(qk, proj):
    bsz, seq, _ = qk.shape
    tq = _pick(seq, (512,))
    tk = _pick(seq, (1024, 512))
    cq = 256
    nk = seq // tk
    gw = GQA_GROUP * HEAD_DIM
    qt = qk[:, :, :AX_Q_HEADS * HEAD_DIM].reshape(bsz, seq, AX_Q_HEADS, HEAD_DIM).transpose(0, 2, 3, 1)
    vt = proj[:, :, 1280:1536].reshape(bsz, seq, AX_KV_HEADS, HEAD_DIM).transpose(0, 2, 3, 1)
    return pl.pallas_call(
        functools.partial(_gqa_kernel, tq=tq, cq=cq, nk=nk),
        grid=(bsz, AX_KV_HEADS, seq // tq, nk),
        in_specs=[pl.BlockSpec((1, GQA_GROUP, HEAD_DIM, tq), lambda b, h, i, j: (b, h, 0, i)),
                  pl.BlockSpec((1, tk, HEAD_DIM), lambda b, h, i, j: (b, j, AX_Q_HEADS + h)),
                  pl.BlockSpec((1, 1, HEAD_DIM, tk), lambda b, h, i, j: (b, h, 0, j))],
        out_specs=pl.BlockSpec((1, tq, gw), lambda b, h, i, j: (b, i, h)),
        out_shape=jax.ShapeDtypeStruct((bsz, seq, AX_Q_HEADS * HEAD_DIM), BF16),
        scratch_shapes=[pltpu.VMEM((1, GQA_GROUP * tq), F32), pltpu.VMEM((1, GQA_GROUP * tq), F32),
                        pltpu.VMEM((HEAD_DIM, GQA_GROUP * tq), F32)],
        compiler_params=_cparams(("parallel", "parallel", "parallel", "arbitrary")),
        name="gqa_attn",
    )(qt, qk, vt)


def _na_bias_table(rpb):
    qi = np.arange(NA_ROWS)[:, None, None, None]
    qc = np.arange(GRID_W)[None, :, None, None]
    kj = np.arange(2 * NA_ROWS)[None, None, :, None]
    kc = np.arange(GRID_W)[None, None, None, :]
    c_start = np.clip(qc - NA_COLS // 2, 0, GRID_W - NA_COLS)
    col_ok = (kc >= c_start) & (kc < c_start + NA_COLS)
    rel_c = np.clip(kc - qc + NA_COLS - 1, 0, 2 * NA_COLS - 2)
    onehot_c = jnp.asarray(rel_c[0, :, 0, :, None] == np.arange(2 * NA_COLS - 1), dtype=F32)
    tabs = []
    for off, clamp in ((0, np.maximum), (-NA_ROWS // 2, None), (-NA_ROWS, np.minimum)):
        rs = qi - NA_ROWS // 2
        if clamp is not None:
            rs = clamp(rs, 0)
        kr = off + kj
        ok = (kr >= rs) & (kr < rs + NA_ROWS) & col_ok
        rel_r = np.clip(kr - qi + NA_ROWS - 1, 0, 2 * NA_ROWS - 2)[:, 0, :, 0]
        onehot_r = jnp.asarray(rel_r[:, :, None] == np.arange(2 * NA_ROWS - 1), dtype=F32)
        bias = jnp.einsum("ija,hab,cdb->hicjd", onehot_r, rpb.astype(F32), onehot_c, precision=HIGHEST)
        tabs.append(jnp.where(ok[None], bias, NEG_INF).reshape(NA_HEADS, NA_TQ, NA_WIN))
    return jnp.stack(tabs)


def _na_kernel(q_ref, k_ref, v_ref, bias_ref, o_ref, *, rows):
    i = pl.program_id(2)
    ws = jnp.clip(i * NA_ROWS - NA_ROWS // 2, 0, rows - 2 * NA_ROWS)
    start = pl.multiple_of(ws * GRID_W, GRID_W)
    k = k_ref[0, pl.ds(start, NA_WIN), :]
    v = v_ref[0, pl.ds(start, NA_WIN), :]
    s = lax.dot_general(q_ref[0], k, (((1,), (1,)), ((), ())), preferred_element_type=F32) + bias_ref[0, 0]
    m = jnp.max(s, axis=-1, keepdims=True)
    p = jnp.exp(s - m)
    l = jnp.sum(p, axis=-1, keepdims=True)
    o_ref[0] = (jnp.dot(p.astype(BF16), v, preferred_element_type=F32) / l).astype(o_ref.dtype)


def _na_attention(proj, bias_tab):
    bsz, seq, _ = proj.shape
    rows = seq // GRID_W
    nq = seq // NA_TQ
    assert rows >= 3 * NA_ROWS and seq % NA_TQ == 0
    qcol, kcol, vcol = 1536 // HEAD_DIM, 2560 // HEAD_DIM, 3584 // HEAD_DIM

    def case(i):
        return jnp.where(i == 0, 0, jnp.where(i == nq - 1, 2, 1))

    return pl.pallas_call(
        functools.partial(_na_kernel, rows=rows),
        grid=(bsz, NA_HEADS, nq),
        in_specs=[pl.BlockSpec((1, NA_TQ, HEAD_DIM), lambda b, h, i: (b, i, qcol + h)),
                  pl.BlockSpec((1, seq, HEAD_DIM), lambda b, h, i: (b, 0, kcol + h)),
                  pl.BlockSpec((1, seq, HEAD_DIM), lambda b, h, i: (b, 0, vcol + h)),
                  pl.BlockSpec((1, 1, NA_TQ, NA_WIN), lambda b, h, i: (case(i), h, 0, 0))],
        out_specs=pl.BlockSpec((1, NA_TQ, HEAD_DIM), lambda b, h, i: (b, i, h)),
        out_shape=jax.ShapeDtypeStruct((bsz, seq, NA_HEADS * HEAD_DIM), BF16),
        compiler_params=_cparams(("parallel", "parallel", "arbitrary")),
        name="na_attn",
    )(proj, proj, proj, bias_tab)


def _outproj_kernel(oa_ref, ob_ref, wa_ref, wb_ref, x_ref, g_ref, lng_ref, lnb_ref, o_ref):
    m = (jnp.dot(oa_ref[0], wa_ref[...], preferred_element_type=F32)
         + jnp.dot(ob_ref[0], wb_ref[...], preferred_element_type=F32))
    y = DEEPNORM_ALPHA * x_ref[0] + g_ref[0] * m
    o_ref[0] = _layer_norm(y, lng_ref[...], lnb_ref[...])


def _outproj_ln(oa, ob, w_out, x, gate, ln_g, ln_b):
    bsz, seq, d = x.shape
    half = oa.shape[-1]
    tm = _pick(seq, (512, 256))
    vec = pl.BlockSpec((1, d), lambda b, i: (0, 0))
    return pl.pallas_call(
        _outproj_kernel,
        grid=(bsz, seq // tm),
        in_specs=[pl.BlockSpec((1, tm, half), lambda b, i: (b, i, 0)),
                  pl.BlockSpec((1, tm, half), lambda b, i: (b, i, 0)),
                  pl.BlockSpec((half, d), lambda b, i: (0, 0)),
                  pl.BlockSpec((half, d), lambda b, i: (1, 0)),
                  pl.BlockSpec((1, tm, d), lambda b, i: (b, i, 0)),
                  pl.BlockSpec((1, 1, d), lambda b, i: (b, 0, 0)),
                  vec, vec],
        out_specs=pl.BlockSpec((1, tm, d), lambda b, i: (b, i, 0)),
        out_shape=jax.ShapeDtypeStruct((bsz, seq, d), F32),
        compiler_params=_cparams(("parallel", "parallel")),
        name="outproj_ln",
    )(oa, ob, w_out, w_out, x, gate, ln_g.reshape(1, d), ln_b.reshape(1, d))


def _route(logits):
    lane = lax.broadcasted_iota(jnp.int32, logits.shape, 1)
    big = jnp.int32(2 ** 30)
    is_g = lane < N_GROUPS
    mg = jnp.max(jnp.where(is_g, logits, -jnp.inf), axis=-1, keepdims=True)
    g_idx = jnp.min(jnp.where(is_g & (logits == mg), lane, big), axis=-1, keepdims=True)
    p_g = 1.0 / jnp.sum(jnp.where(is_g, jnp.exp(logits - mg), 0.0), axis=-1, keepdims=True)
    lo = N_GROUPS + EXPERTS_PER_GROUP * g_idx
    in_grp = (lane >= lo) & (lane < lo + EXPERTS_PER_GROUP)
    v1 = jnp.max(jnp.where(in_grp, logits, -jnp.inf), axis=-1, keepdims=True)
    i1 = jnp.min(jnp.where(in_grp & (logits == v1), lane, big), axis=-1, keepdims=True)
    rest = in_grp & (lane != i1)
    v2 = jnp.max(jnp.where(rest, logits, -jnp.inf), axis=-1, keepdims=True)
    i2 = jnp.min(jnp.where(rest & (logits == v2), lane, big), axis=-1, keepdims=True)
    e2 = jnp.exp(v2 - v1)
    w1 = p_g / (1.0 + e2)
    w2 = p_g * e2 / (1.0 + e2)
    return jnp.where(lane == i1, w1, 0.0) + jnp.where(lane == i2, w2, 0.0)


def _moe_kernel(x_ref, sc_ref, sh_ref, g2_ref, wr_ref, br_ref, wg_ref, wu_ref, wd_ref, lng_ref, lnb_ref,
                o_ref, h_ref, gate_ref, acc_ref):
    e = pl.program_id(2)

    @pl.when(e == 0)
    def _():
        h = x_ref[0] * (1.0 + sc_ref[0]) + sh_ref[0]
        h_ref[...] = h.astype(BF16)
        logits = jnp.dot(h, wr_ref[...], precision=HIGHEST, preferred_element_type=F32) + br_ref[...]
        gate_ref[...] = _route(logits)
        acc_ref[...] = jnp.zeros_like(acc_ref)

    hb = h_ref[...]
    a = jnp.dot(hb, wg_ref[0], preferred_element_type=F32)
    u = jnp.dot(hb, wu_ref[0], preferred_element_type=F32)
    lane = lax.broadcasted_iota(jnp.int32, gate_ref.shape, 1)
    ge = jnp.sum(jnp.where(lane == e + N_GROUPS, gate_ref[...], 0.0), axis=-1, keepdims=True)
    act = (_silu(a) * u * ge).astype(BF16)
    acc_ref[...] += jnp.dot(act, wd_ref[0], preferred_element_type=F32)

    @pl.when(e == N_EXPERTS - 1)
    def _():
        y = DEEPNORM_ALPHA * x_ref[0] + g2_ref[0] * acc_ref[...]
        o_ref[0] = _layer_norm(y, lng_ref[...], lnb_ref[...])


def _moe_ln(x, sc, sh, gate, w_router, b_router, w_gate, w_up, w_down, ln_g, ln_b):
    bsz, seq, d = x.shape
    tm = _pick(seq, (512, 256))
    vec = pl.BlockSpec((1, d), lambda b, i, e: (0, 0))
    mod = pl.BlockSpec((1, 1, d), lambda b, i, e: (b, 0, 0))
    return pl.pallas_call(
        _moe_kernel,
        grid=(bsz, seq // tm, N_EXPERTS),
        in_specs=[pl.BlockSpec((1, tm, d), lambda b, i, e: (b, i, 0)), mod, mod, mod,
                  pl.BlockSpec((d, ROUTER_LANES), lambda b, i, e: (0, 0)),
                  pl.BlockSpec((1, ROUTER_LANES), lambda b, i, e: (0, 0)),
                  pl.BlockSpec((1, d, D_EXPERT), lambda b, i, e: (e, 0, 0)),
                  pl.BlockSpec((1, d, D_EXPERT), lambda b, i, e: (e, 0, 0)),
                  pl.BlockSpec((1, D_EXPERT, d), lambda b, i, e: (e, 0, 0)),
                  vec, vec],
        out_specs=pl.BlockSpec((1, tm, d), lambda b, i, e: (b, i, 0)),
        out_shape=jax.ShapeDtypeStruct((bsz, seq, d), F32),
        scratch_shapes=[pltpu.VMEM((tm, d), BF16), pltpu.VMEM((tm, ROUTER_LANES), F32), pltpu.VMEM((tm, d), F32)],
        compiler_params=_cparams(("parallel", "parallel", "arbitrary")),
        name="moe_ln",
    )(x, sc, sh, gate, w_router, b_router, w_gate, w_up, w_down, ln_g.reshape(1, d), ln_b.reshape(1, d))


def _alibi_slopes(n):
    return jnp.asarray(2.0 ** (-8.0 * np.arange(1, n + 1) / n), dtype=F32)


def _colscale(q_ranges):
    cs = np.ones((1, PROJ_N), np.float32)
    for lo, hi, scale in q_ranges:
        cs[:, lo:hi] = scale
    return jnp.asarray(cs)


def _prep_params(p):
    q = dict(p)
    for name in ("w_in_ab", "w_out_ab", "w_in_cd", "w_out_cd", "w_gate", "w_up", "w_down"):
        q[name] = p[name].astype(BF16)
    pad = ROUTER_LANES - N_GROUPS - N_EXPERTS
    q["w_router"] = jnp.pad(jnp.concatenate([p["w_rg"], p["w_re"]], axis=-1), ((0, 0), (0, 0), (0, pad)))
    q["b_router"] = jnp.pad(jnp.concatenate([p["b_rg"], p["b_re"]], axis=-1), ((0, 0), (0, pad)))[:, None, :]
    q["na_bias"] = [_na_bias_table(p["rpb_d"][i]) for i in range(p["rpb_d"].shape[0])]
    return q


def _trunk(x, ada, p):
    for l in range(DEPTH):
        sh1, sc1, g1, sh2, sc2, g2 = jnp.split(ada[l][:, None, :], 6, axis=-1)
        i = l // 2
        if l % 2 == 0:
            lam_init = 0.8 - 0.6 * math.exp(-0.3 * l)
            proj = _proj(x, sc1, sh1, p["w_in_ab"][i], _colscale(((0, 1024, ATTN_SCALE), (1536, 2560, ATTN_SCALE * LOG2E))))
            oa = _win_attention(proj, p["sink_a"][i], _alibi_slopes(WIN_Q_HEADS))
            lamvec = jnp.stack([p["lam_q1"][i], p["lam_k1"][i], p["lam_q2"][i], p["lam_k2"][i]])
            ob = _diff_attention(proj, _alibi_slopes(DIFF_HEADS), lamvec, p["subln_g"][i][None, :], lam_init)
            w_out = p["w_out_ab"][i]
        else:
            proj = _proj(x, sc1, sh1, p["w_in_cd"][i], _colscale(((1536, 2560, ATTN_SCALE),)))
            qk = _normrope(proj, p["qnorm_c"][i], p["knorm_c"][i])
            oa = _gqa_attention(qk, proj)
            ob = _na_attention(proj, p["na_bias"][i])
            w_out = p["w_out_cd"][i]
        x = _outproj_ln(oa, ob, w_out, x, g1, p["ln_g"][l, 0], p["ln_b"][l, 0])
        x = _moe_ln(x, sc2, sh2, g2, p["w_router"][l], p["b_router"][l],
                    p["w_gate"][l], p["w_up"][l], p["w_down"][l], p["ln_g"][l, 1], p["ln_b"][l, 1])
    return x


def kernel(x_prompt, x_sample, c_prompt, c_sample, w_ada, b_ada, ln_g, ln_b, w_in_ab, w_out_ab, sink_a,
           lam_q1, lam_k1, lam_q2, lam_k2, subln_g, w_in_cd, w_out_cd, qnorm_c, knorm_c, rpb_d,
           w_rg, b_rg, w_re, b_re, w_gate, w_up, w_down):
    p = _prep_params(dict(
        ln_g=ln_g, ln_b=ln_b, w_in_ab=w_in_ab, w_out_ab=w_out_ab, sink_a=sink_a,
        lam_q1=lam_q1, lam_k1=lam_k1, lam_q2=lam_q2, lam_k2=lam_k2, subln_g=subln_g,
        w_in_cd=w_in_cd, w_out_cd=w_out_cd, qnorm_c=qnorm_c, knorm_c=knorm_c, rpb_d=rpb_d,
        w_rg=w_rg, b_rg=b_rg, w_re=w_re, b_re=b_re, w_gate=w_gate, w_up=w_up, w_down=w_down))
    nb_p, nb_s = c_prompt.shape[0], c_sample.shape[0]
    c_all = jnp.concatenate([c_prompt, c_sample, jnp.zeros((8 - nb_p - nb_s, D_MODEL), F32)], axis=0)
    ada = _ada(c_all, w_ada, b_ada)
    y_prompt = _trunk(x_prompt, ada[:, :nb_p], p)
    y_sample = _trunk(x_sample, ada[:, nb_p:nb_p + nb_s], p)
    return (y_prompt, y_sample)
```

```python
import functools
import math

import numpy as np
import jax
import jax.numpy as jnp
from jax import lax
from jax.experimental import pallas as pl
from jax.experimental.pallas import tpu as pltpu

F32 = jnp.float32
BF16 = jnp.bfloat16
HIGHEST = lax.Precision.HIGHEST

D_MODEL = 2048
DEPTH = 2
HEAD_DIM = 128
WINDOW = 128
GRID_W = 64
WIN_Q_HEADS = 8
WIN_KV_HEADS = 2
GQA_GROUP = 4
DIFF_HEADS = 4
DIFF_V_DIM = 256
AX_Q_HEADS = 8
AX_KV_HEADS = 2
NA_HEADS = 8
NA_ROWS = 8
NA_COLS = 16
ROPE_THETA = 10000.0
N_GROUPS = 4
EXPERTS_PER_GROUP = 4
N_EXPERTS = 16
D_EXPERT = 512
LN_EPS = 1e-5
RMS_EPS = 1e-6
NEG_INF = -1e30
DEEPNORM_ALPHA = (2 * DEPTH) ** 0.25
PROJ_N = 4608
ATTN_SCALE = HEAD_DIM ** -0.5
LOG2E = math.log2(math.e)
LANES = 128
ROUTER_LANES = LANES
VMEM_LIMIT = 56 * 1024 * 1024

NA_TQ = NA_ROWS * GRID_W
NA_WIN = 2 * NA_ROWS * GRID_W


def _cparams(sem):
    return pltpu.CompilerParams(dimension_semantics=sem, vmem_limit_bytes=VMEM_LIMIT)


def _pick(n, cands):
    for c in cands:
        if n % c == 0:
            return c
    raise ValueError(f"no tile in {cands} divides {n}")


def _layer_norm(y, g, b):
    mu = jnp.mean(y, axis=-1, keepdims=True)
    yc = y - mu
    var = jnp.mean(yc * yc, axis=-1, keepdims=True)
    return yc * lax.rsqrt(var + LN_EPS) * g + b


def _silu(a):
    return a / (1.0 + jnp.exp(-a))


def _ada_kernel(c_ref, w_ref, b_ref, o_ref):
    a = _silu(c_ref[...])
    o_ref[0] = jnp.dot(a, w_ref[0], precision=HIGHEST, preferred_element_type=F32) + b_ref[0]


def _ada(c_all, w_ada, b_ada):
    n = 6 * D_MODEL
    tn = 1024
    return pl.pallas_call(
        _ada_kernel,
        grid=(DEPTH, n // tn),
        in_specs=[pl.BlockSpec((8, D_MODEL), lambda l, j: (0, 0)),
                  pl.BlockSpec((1, D_MODEL, tn), lambda l, j: (l, 0, j)),
                  pl.BlockSpec((1, 1, tn), lambda l, j: (l, 0, j))],
        out_specs=pl.BlockSpec((1, 8, tn), lambda l, j: (l, 0, j)),
        out_shape=jax.ShapeDtypeStruct((DEPTH, 8, n), F32),
        compiler_params=_cparams(("parallel", "parallel")),
        name="ada",
    )(c_all, w_ada, b_ada.reshape(DEPTH, 1, n))


def _proj_kernel(x_ref, sc_ref, sh_ref, w_ref, cs_ref, o_ref, h_ref):
    @pl.when(pl.program_id(2) == 0)
    def _():
        h_ref[...] = (x_ref[0] * (1.0 + sc_ref[0]) + sh_ref[0]).astype(BF16)

    acc = jnp.dot(h_ref[...], w_ref[...], preferred_element_type=F32)
    o_ref[0] = (acc * cs_ref[...]).astype(o_ref.dtype)


def _proj(x, sc, sh, w, colscale):
    bsz, seq, d = x.shape
    n = w.shape[1]
    tm = _pick(seq, (512, 256))
    tn = 1536
    return pl.pallas_call(
        _proj_kernel,
        grid=(bsz, seq // tm, n // tn),
        in_specs=[pl.BlockSpec((1, tm, d), lambda b, i, j: (b, i, 0)),
                  pl.BlockSpec((1, 1, d), lambda b, i, j: (b, 0, 0)),
                  pl.BlockSpec((1, 1, d), lambda b, i, j: (b, 0, 0)),
                  pl.BlockSpec((d, tn), lambda b, i, j: (0, j)),
                  pl.BlockSpec((1, tn), lambda b, i, j: (0, j))],
        out_specs=pl.BlockSpec((1, tm, tn), lambda b, i, j: (b, i, j)),
        out_shape=jax.ShapeDtypeStruct((bsz, seq, n), BF16),
        scratch_shapes=[pltpu.VMEM((tm, d), BF16)],
        compiler_params=_cparams(("parallel", "parallel", "arbitrary")),
        name="proj",
    )(x, sc, sh, w, colscale)


def _win_kernel(sink_ref, slope_ref, q_ref, kp_ref, km_ref, kn_ref, vp_ref, vm_ref, vn_ref, o_ref, *, tq, seq):
    i = pl.program_id(1)
    kvh = pl.program_id(2)
    k = jnp.concatenate([kp_ref[0], km_ref[0], kn_ref[0]], axis=0)
    v = jnp.concatenate([vp_ref[0], vm_ref[0], vn_ref[0]], axis=0)
    w = tq + 2 * WINDOW
    qrel = lax.broadcasted_iota(jnp.int32, (tq, w), 0)
    krel = lax.broadcasted_iota(jnp.int32, (tq, w), 1) - WINDOW
    dist = jnp.abs(qrel - krel)
    kabs = krel + i * tq
    valid = (dist <= WINDOW) & (kabs >= 0) & (kabs < seq)
    distf = dist.astype(F32)
    for g in range(GQA_GROUP):
        h = kvh * GQA_GROUP + g
        qg = q_ref[0, :, g * HEAD_DIM:(g + 1) * HEAD_DIM]
        s = lax.dot_general(qg, k, (((1,), (1,)), ((), ())), preferred_element_type=F32)
        s = jnp.where(valid, s - slope_ref[h] * distf, NEG_INF)
        snk = sink_ref[h]
        m = jnp.maximum(jnp.max(s, axis=-1, keepdims=True), snk)
        p = jnp.exp(s - m)
        l = jnp.sum(p, axis=-1, keepdims=True) + jnp.exp(snk - m)
        o = jnp.dot(p.astype(BF16), v, preferred_element_type=F32) / l
        o_ref[0, :, g * HEAD_DIM:(g + 1) * HEAD_DIM] = o.astype(o_ref.dtype)


def _win_attention(proj, sink, slopes):
    bsz, seq, _ = proj.shape
    tq = 256
    r = tq // WINDOW
    nb = seq // WINDOW
    kcol = WIN_Q_HEADS
    vcol = WIN_Q_HEADS + WIN_KV_HEADS

    def prev(col):
        return pl.BlockSpec((1, WINDOW, HEAD_DIM), lambda b, i, k: (b, jnp.maximum(i * r - 1, 0), col + k))

    def main(col):
        return pl.BlockSpec((1, tq, HEAD_DIM), lambda b, i, k: (b, i, col + k))

    def nxt(col):
        return pl.BlockSpec((1, WINDOW, HEAD_DIM), lambda b, i, k: (b, jnp.minimum((i + 1) * r, nb - 1), col + k))

    smem = pl.BlockSpec(memory_space=pltpu.SMEM)
    return pl.pallas_call(
        functools.partial(_win_kernel, tq=tq, seq=seq),
        grid=(bsz, seq // tq, WIN_KV_HEADS),
        in_specs=[smem, smem,
                  pl.BlockSpec((1, tq, GQA_GROUP * HEAD_DIM), lambda b, i, k: (b, i, k)),
                  prev(kcol), main(kcol), nxt(kcol), prev(vcol), main(vcol), nxt(vcol)],
        out_specs=pl.BlockSpec((1, tq, GQA_GROUP * HEAD_DIM), lambda b, i, k: (b, i, k)),
        out_shape=jax.ShapeDtypeStruct((bsz, seq, WIN_Q_HEADS * HEAD_DIM), BF16),
        compiler_params=_cparams(("parallel", "parallel", "parallel")),
        name="win_attn",
    )(sink, slopes, proj, proj, proj, proj, proj, proj, proj)


def _online_update_t(s, vt, m_ref, l_ref, acc_ref, idx):
    m_old = m_ref[idx]
    m_new = jnp.maximum(m_old, jnp.max(s, axis=0, keepdims=True))
    alpha = jnp.exp2(m_old - m_new)
    p = jnp.exp2(s - m_new)
    l_ref[idx] = alpha * l_ref[idx] + jnp.sum(p, axis=0, keepdims=True)
    acc_ref[idx] = alpha * acc_ref[idx] + jnp.dot(vt, p.astype(BF16), preferred_element_type=F32)
    m_ref[idx] = m_new


def _diff_kernel(slope_ref, q_ref, k1_ref, k2_ref, vt_ref, lam_ref, g_ref, o_ref, m_ref, l_ref, acc_ref,
                 *, tq, tk, cq, nk, lam_init):
    h = pl.program_id(1)
    i = pl.program_id(2)
    j = pl.program_id(3)

    @pl.when(j == 0)
    def _():
        m_ref[...] = jnp.full_like(m_ref, NEG_INF)
        l_ref[...] = jnp.zeros_like(l_ref)
        acc_ref[...] = jnp.zeros_like(acc_ref)

    neg_slope = -slope_ref[h] * LOG2E
    dbase = (lax.broadcasted_iota(jnp.int32, (tk, cq), 1) - lax.broadcasted_iota(jnp.int32, (tk, cq), 0)).astype(F32)
    vt = vt_ref[0, 0]
    for c in range(tq // cq):
        cols = slice(c * cq, (c + 1) * cq)
        off = (i * tq + c * cq - j * tk).astype(F32)
        bias = jnp.abs(dbase + off) * neg_slope
        for mi, k_ref in enumerate((k1_ref, k2_ref)):
            s = jnp.dot(k_ref[0], q_ref[0, 0, mi, :, cols], preferred_element_type=F32) + bias
            _online_update_t(s, vt, m_ref, l_ref, acc_ref, (mi, slice(None), cols))

    @pl.when(j == nk - 1)
    def _():
        lv = lam_ref[...]
        lam = (jnp.exp(jnp.sum(lv[0:1] * lv[1:2], axis=-1, keepdims=True))
               - jnp.exp(jnp.sum(lv[2:3] * lv[3:4], axis=-1, keepdims=True)) + lam_init)
        ot = acc_ref[0] / l_ref[0] - lam * (acc_ref[1] / l_ref[1])
        o = ot.T
        r = lax.rsqrt(jnp.mean(o * o, axis=-1, keepdims=True) + RMS_EPS)
        o_ref[0] = (o * r * g_ref[...] * (1.0 - lam_init)).astype(o_ref.dtype)


def _diff_attention(proj, slopes, lamvec, subln_g, lam_init):
    bsz, seq, _ = proj.shape
    tq = _pick(seq, (1024, 512))
    tk = _pick(seq, (1024, 512))
    cq = 256
    nk = seq // tk
    kcol = 2560 // HEAD_DIM
    qt = proj[:, :, 1536:2560].reshape(bsz, seq, DIFF_HEADS, 2, HEAD_DIM).transpose(0, 2, 3, 4, 1)
    vt = proj[:, :, 3584:4608].reshape(bsz, seq, DIFF_HEADS, DIFF_V_DIM).transpose(0, 2, 3, 1)
    smem = pl.BlockSpec(memory_space=pltpu.SMEM)

    def kspec(off):
        return pl.BlockSpec((1, tk, HEAD_DIM), lambda b, h, i, j: (b, j, kcol + 2 * h + off))

    return pl.pallas_call(
        functools.partial(_diff_kernel, tq=tq, tk=tk, cq=cq, nk=nk, lam_init=lam_init),
        grid=(bsz, DIFF_HEADS, seq // tq, nk),
        in_specs=[smem,
                  pl.BlockSpec((1, 1, 2, HEAD_DIM, tq), lambda b, h, i, j: (b, h, 0, 0, i)),
                  kspec(0), kspec(1),
                  pl.BlockSpec((1, 1, DIFF_V_DIM, tk), lambda b, h, i, j: (b, h, 0, j)),
                  pl.BlockSpec((4, HEAD_DIM), lambda b, h, i, j: (0, 0)),
                  pl.BlockSpec((1, DIFF_V_DIM), lambda b, h, i, j: (0, 0))],
        out_specs=pl.BlockSpec((1, tq, DIFF_V_DIM), lambda b, h, i, j: (b, i, h)),
        out_shape=jax.ShapeDtypeStruct((bsz, seq, DIFF_HEADS * DIFF_V_DIM), BF16),
        scratch_shapes=[pltpu.VMEM((2, 1, tq), F32), pltpu.VMEM((2, 1, tq), F32),
                        pltpu.VMEM((2, DIFF_V_DIM, tq), F32)],
        compiler_params=_cparams(("parallel", "parallel", "parallel", "arbitrary")),
        name="diff_attn",
    )(slopes, qt, proj, proj, vt, lamvec, subln_g)


def _normrope_kernel(x_ref, cos_ref, sin_ref, gq_ref, gk_ref, o_ref, *, tm):
    j = pl.program_id(2)
    is_q = j < AX_Q_HEADS // 2
    g = jnp.where(is_q, gq_ref[...], gk_ref[...])
    post = jnp.where(is_q, ATTN_SCALE * LOG2E, 1.0).astype(F32)
    lane = lax.broadcasted_iota(jnp.int32, (tm, HEAD_DIM), 1)
    first = (lane % 64) < 32
    cos = cos_ref[...]
    sin = sin_ref[...]
    for hh in range(2):
        x = x_ref[0, :, hh * HEAD_DIM:(hh + 1) * HEAD_DIM].astype(F32)
        xn = x * lax.rsqrt(jnp.mean(x * x, axis=-1, keepdims=True) + RMS_EPS) * g
        partner = jnp.where(first, pltpu.roll(xn, 96, 1), pltpu.roll(xn, 32, 1))
        y = (xn * cos + partner * sin) * post
        o_ref[0, :, hh * HEAD_DIM:(hh + 1) * HEAD_DIM] = y.astype(o_ref.dtype)


def _rope_tables(seq):
    half = HEAD_DIM // 2
    inv = jnp.asarray(1.0 / (ROPE_THETA ** (np.arange(0, half, 2) / half)), dtype=F32)
    t = jnp.arange(seq)
    ang_r = (t // GRID_W).astype(F32)[:, None] * inv[None, :]
    ang_c = (t % GRID_W).astype(F32)[:, None] * inv[None, :]
    cr, sr, cc, sn = jnp.cos(ang_r), jnp.sin(ang_r), jnp.cos(ang_c), jnp.sin(ang_c)
    return (jnp.concatenate([cr, cr, cc, cc], axis=-1), jnp.concatenate([-sr, sr, -sn, sn], axis=-1))


def _normrope(proj, qnorm_g, knorm_g):
    bsz, seq, _ = proj.shape
    tm = _pick(seq, (512,))
    cos, sin = _rope_tables(seq)
    nw = (AX_Q_HEADS + AX_KV_HEADS) * HEAD_DIM
    return pl.pallas_call(
        functools.partial(_normrope_kernel, tm=tm),
        grid=(bsz, seq // tm, nw // 256),
        in_specs=[pl.BlockSpec((1, tm, 256), lambda b, i, j: (b, i, j)),
                  pl.BlockSpec((tm, HEAD_DIM), lambda b, i, j: (i, 0)),
                  pl.BlockSpec((tm, HEAD_DIM), lambda b, i, j: (i, 0)),
                  pl.BlockSpec((1, HEAD_DIM), lambda b, i, j: (0, 0)),
                  pl.BlockSpec((1, HEAD_DIM), lambda b, i, j: (0, 0))],
        out_specs=pl.BlockSpec((1, tm, 256), lambda b, i, j: (b, i, j)),
        out_shape=jax.ShapeDtypeStruct((bsz, seq, nw), BF16),
        compiler_params=_cparams(("parallel", "parallel", "parallel")),
        name="normrope",
    )(proj, cos, sin, qnorm_g.reshape(1, HEAD_DIM), knorm_g.reshape(1, HEAD_DIM))


def _gqa_kernel(q_ref, k_ref, vt_ref, o_ref, m_ref, l_ref, acc_ref, *, tq, cq, nk):
    j = pl.program_id(3)

    @pl.when(j == 0)
    def _():
        m_ref[...] = jnp.full_like(m_ref, NEG_INF)
        l_ref[...] = jnp.zeros_like(l_ref)
        acc_ref[...] = jnp.zeros_like(acc_ref)

    k = k_ref[0]
    vt = vt_ref[0, 0]
    per = tq // cq
    for c in range(GQA_GROUP * per):
        g, sub = divmod(c, per)
        s = jnp.dot(k, q_ref[0, g, :, sub * cq:(sub + 1) * cq], preferred_element_type=F32)
        _online_update_t(s, vt, m_ref, l_ref, acc_ref, (slice(None), slice(c * cq, (c + 1) * cq)))

    @pl.when(j == nk - 1)
    def _():
        ot = acc_ref[...] / l_ref[...]
        for g in range(GQA_GROUP):
            o_ref[0, :, g * HEAD_DIM:(g + 1) * HEAD_DIM] = ot[:, g * tq:(g + 1) * tq].T.astype(o_ref.dtype)


def _gqa_attention(qk, proj):
    bsz, seq, _ = qk.shape
    tq = _pick(seq, (512,))
    tk = _pick(seq, (1024, 512))
    cq = 128
    nk = seq // tk
    gw = GQA_GROUP * HEAD_DIM
    qt = qk[:, :, :AX_Q_HEADS * HEAD_DIM].reshape(bsz, seq, AX_Q_HEADS, HEAD_DIM).transpose(0, 2, 3, 1)
    vt = proj[:, :, 1280:1536].reshape(bsz, seq, AX_KV_HEADS, HEAD_DIM).transpose(0, 2, 3, 1)
    return pl.pallas_call(
        functools.partial(_gqa_kernel, tq=tq, cq=cq, nk=nk),
        grid=(bsz, AX_KV_HEADS, seq // tq, nk),
        in_specs=[pl.BlockSpec((1, GQA_GROUP, HEAD_DIM, tq), lambda b, h, i, j: (b, h, 0, i)),
                  pl.BlockSpec((1, tk, HEAD_DIM), lambda b, h, i, j: (b, j, AX_Q_HEADS + h)),
                  pl.BlockSpec((1, 1, HEAD_DIM, tk), lambda b, h, i, j: (b, h, 0, j))],
        out_specs=pl.BlockSpec((1, tq, gw), lambda b, h, i, j: (b, i, h)),
        out_shape=jax.ShapeDtypeStruct((bsz, seq, AX_Q_HEADS * HEAD_DIM), BF16),
        scratch_shapes=[pltpu.VMEM((1, GQA_GROUP * tq), F32), pltpu.VMEM((1, GQA_GROUP * tq), F32),
                        pltpu.VMEM((HEAD_DIM, GQA_GROUP * tq), F32)],
        compiler_params=_cparams(("parallel", "parallel", "parallel", "arbitrary")),
        name="gqa_attn",
    )(qt, qk, vt)


def _na_bias_table(rpb):
    qi = np.arange(NA_ROWS)[:, None, None, None]
    qc = np.arange(GRID_W)[None, :, None, None]
    kj = np.arange(2 * NA_ROWS)[None, None, :, None]
    kc = np.arange(GRID_W)[None, None, None, :]
    c_start = np.clip(qc - NA_COLS // 2, 0, GRID_W - NA_COLS)
    col_ok = (kc >= c_start) & (kc < c_start + NA_COLS)
    rel_c = np.clip(kc - qc + NA_COLS - 1, 0, 2 * NA_COLS - 2)
    onehot_c = jnp.asarray(rel_c[0, :, 0, :, None] == np.arange(2 * NA_COLS - 1), dtype=F32)
    tabs = []
    for off, clamp in ((0, np.maximum), (-NA_ROWS // 2, None), (-NA_ROWS, np.minimum)):
        rs = qi - NA_ROWS // 2
        if clamp is not None:
            rs = clamp(rs, 0)
        kr = off + kj
        ok = (kr >= rs) & (kr < rs + NA_ROWS) & col_ok
        rel_r = np.clip(kr - qi + NA_ROWS - 1, 0, 2 * NA_ROWS - 2)[:, 0, :, 0]
        onehot_r = jnp.asarray(rel_r[:, :, None] == np.arange(2 * NA_ROWS - 1), dtype=F32)
        bias = jnp.einsum("ija,hab,cdb->hicjd", onehot_r, rpb.astype(F32), onehot_c, precision=HIGHEST)
        tabs.append(jnp.where(ok[None], bias, NEG_INF).reshape(NA_HEADS, NA_TQ, NA_WIN))
    return jnp.stack(tabs)


def _na_kernel(q_ref, k_ref, v_ref, bias_ref, o_ref, *, rows):
    i = pl.program_id(2)
    ws = jnp.clip(i * NA_ROWS - NA_ROWS // 2, 0, rows - 2 * NA_ROWS)
    start = pl.multiple_of(ws * GRID_W, GRID_W)
    k = k_ref[0, pl.ds(start, NA_WIN), :]
    v = v_ref[0, pl.ds(start, NA_WIN), :]
    s = lax.dot_general(q_ref[0], k, (((1,), (1,)), ((), ())), preferred_element_type=F32) + bias_ref[0, 0]
    m = jnp.max(s, axis=-1, keepdims=True)
    p = jnp.exp(s - m)
    l = jnp.sum(p, axis=-1, keepdims=True)
    o_ref[0] = (jnp.dot(p.astype(BF16), v, preferred_element_type=F32) / l).astype(o_ref.dtype)


def _na_attention(proj, bias_tab):
    bsz, seq, _ = proj.shape
    rows = seq // GRID_W
    nq = seq // NA_TQ
    assert rows >= 3 * NA_ROWS and seq % NA_TQ == 0
    qcol, kcol, vcol = 1536 // HEAD_DIM, 2560 // HEAD_DIM, 3584 // HEAD_DIM

    def case(i):
        return jnp.where(i == 0, 0, jnp.where(i == nq - 1, 2, 1))

    return pl.pallas_call(
        functools.partial(_na_kernel, rows=rows),
        grid=(bsz, NA_HEADS, nq),
        in_specs=[pl.BlockSpec((1, NA_TQ, HEAD_DIM), lambda b, h, i: (b, i, qcol + h)),
                  pl.BlockSpec((1, seq, HEAD_DIM), lambda b, h, i: (b, 0, kcol + h)),
                  pl.BlockSpec((1, seq, HEAD_DIM), lambda b, h, i: (b, 0, vcol + h)),
                  pl.BlockSpec((1, 1, NA_TQ, NA_WIN), lambda b, h, i: (case(i), h, 0, 0))],
        out_specs=pl.BlockSpec((1, NA_TQ, HEAD_DIM), lambda b, h, i: (b, i, h)),
        out_shape=jax.ShapeDtypeStruct((bsz, seq, NA_HEADS * HEAD_DIM), BF16),
        compiler_params=_cparams(("parallel", "parallel", "arbitrary")),
        name="na_attn",
    )(proj, proj, proj, bias_tab)


def _outproj_kernel(oa_ref, ob_ref, wa_ref, wb_ref, x_ref, g_ref, lng_ref, lnb_ref, o_ref):
    m = (jnp.dot(oa_ref[0], wa_ref[...], preferred_element_type=F32)
         + jnp.dot(ob_ref[0], wb_ref[...], preferred_element_type=F32))
    y = DEEPNORM_ALPHA * x_ref[0] + g_ref[0] * m
    o_ref[0] = _layer_norm(y, lng_ref[...], lnb_ref[...])


def _outproj_ln(oa, ob, w_out, x, gate, ln_g, ln_b):
    bsz, seq, d = x.shape
    half = oa.shape[-1]
    tm = _pick(seq, (512, 256))
    vec = pl.BlockSpec((1, d), lambda b, i: (0, 0))
    return pl.pallas_call(
        _outproj_kernel,
        grid=(bsz, seq // tm),
        in_specs=[pl.BlockSpec((1, tm, half), lambda b, i: (b, i, 0)),
                  pl.BlockSpec((1, tm, half), lambda b, i: (b, i, 0)),
                  pl.BlockSpec((half, d), lambda b, i: (0, 0)),
                  pl.BlockSpec((half, d), lambda b, i: (1, 0)),
                  pl.BlockSpec((1, tm, d), lambda b, i: (b, i, 0)),
                  pl.BlockSpec((1, 1, d), lambda b, i: (b, 0, 0)),
                  vec, vec],
        out_specs=pl.BlockSpec((1, tm, d), lambda b, i: (b, i, 0)),
        out_shape=jax.ShapeDtypeStruct((bsz, seq, d), F32),
        compiler_params=_cparams(("parallel", "parallel")),
        name="outproj_ln",
    )(oa, ob, w_out, w_out, x, gate, ln_g.reshape(1, d), ln_b.reshape(1, d))


def _route(logits):
    lane = lax.broadcasted_iota(jnp.int32, logits.shape, 1)
    big = jnp.int32(2 ** 30)
    is_g = lane < N_GROUPS
    mg = jnp.max(jnp.where(is_g, logits, -jnp.inf), axis=-1, keepdims=True)
    g_idx = jnp.min(jnp.where(is_g & (logits == mg), lane, big), axis=-1, keepdims=True)
    p_g = 1.0 / jnp.sum(jnp.where(is_g, jnp.exp(logits - mg), 0.0), axis=-1, keepdims=True)
    lo = N_GROUPS + EXPERTS_PER_GROUP * g_idx
    in_grp = (lane >= lo) & (lane < lo + EXPERTS_PER_GROUP)
    v1 = jnp.max(jnp.where(in_grp, logits, -jnp.inf), axis=-1, keepdims=True)
    i1 = jnp.min(jnp.where(in_grp & (logits == v1), lane, big), axis=-1, keepdims=True)
    rest = in_grp & (lane != i1)
    v2 = jnp.max(jnp.where(rest, logits, -jnp.inf), axis=-1, keepdims=True)
    i2 = jnp.min(jnp.where(rest & (logits == v2), lane, big), axis=-1, keepdims=True)
    e2 = jnp.exp(v2 - v1)
    w1 = p_g / (1.0 + e2)
    w2 = p_g * e2 / (1.0 + e2)
    gates = jnp.where(lane == i1, w1, 0.0) + jnp.where(lane == i2, w2, 0.0)
    return jnp.where(lane == 0, g_idx.astype(F32), gates)


def _moe_route_kernel(x_ref, sc_ref, sh_ref, wr_ref, br_ref, h_ref, gate_ref):
    h = x_ref[0] * (1.0 + sc_ref[0]) + sh_ref[0]
    h_ref[0] = h.astype(BF16)
    logits = jnp.dot(h, wr_ref[...], precision=HIGHEST, preferred_element_type=F32) + br_ref[...]
    gate_ref[0] = _route(logits)


def _moe_route(x, sc, sh, w_router, b_router):
    bsz, seq, d = x.shape
    tm = _pick(seq, (512, 256))
    mod = pl.BlockSpec((1, 1, d), lambda b, i: (b, 0, 0))
    return pl.pallas_call(
        _moe_route_kernel,
        grid=(bsz, seq // tm),
        in_specs=[pl.BlockSpec((1, tm, d), lambda b, i: (b, i, 0)), mod, mod,
                  pl.BlockSpec((d, ROUTER_LANES), lambda b, i: (0, 0)),
                  pl.BlockSpec((1, ROUTER_LANES), lambda b, i: (0, 0))],
        out_specs=[pl.BlockSpec((1, tm, d), lambda b, i: (b, i, 0)),
                   pl.BlockSpec((1, tm, ROUTER_LANES), lambda b, i: (b, i, 0))],
        out_shape=[jax.ShapeDtypeStruct((bsz, seq, d), BF16),
                   jax.ShapeDtypeStruct((bsz, seq, ROUTER_LANES), F32)],
        compiler_params=_cparams(("parallel", "parallel")),
        name="moe_route",
    )(x, sc, sh, w_router, b_router)


def _moe_expert_kernel(h_ref, gate_ref, wg_ref, wu_ref, wd_ref, o_ref, pmt_ref, hs_ref, gs_ref, ys_ref, bounds_ref,
                       *, tm, chunk):
    e = pl.program_id(2)

    @pl.when(e == 0)
    def _():
        rec = gate_ref[0]
        lane = lax.broadcasted_iota(jnp.int32, (tm, ROUTER_LANES), 1)
        onehot = jnp.where((lane < N_GROUPS) & (lane == rec[:, 0:1].astype(jnp.int32)), 1.0, 0.0)
        below = (lax.broadcasted_iota(jnp.int32, (tm, tm), 1)
                 < lax.broadcasted_iota(jnp.int32, (tm, tm), 0)).astype(BF16)
        rank = jnp.dot(below, onehot.astype(BF16), preferred_element_type=F32)
        cnt = jnp.sum(onehot, axis=0, keepdims=True)
        lane1 = lax.broadcasted_iota(jnp.int32, (1, ROUTER_LANES), 1)
        starts = jnp.zeros((1, ROUTER_LANES), F32)
        for g in range(N_GROUPS):
            first = jnp.sum(jnp.where(lane1 < g, cnt, 0.0))
            size = jnp.sum(jnp.where(lane1 == g, cnt, 0.0))
            starts = starts + jnp.where(lane1 == g, first, 0.0)
            first_i = first.astype(jnp.int32)
            bounds_ref[2 * g] = first_i // chunk
            bounds_ref[2 * g + 1] = (first_i + size.astype(jnp.int32) + chunk - 1) // chunk
        pos = jnp.sum(onehot * (rank + starts), axis=1, keepdims=True).astype(jnp.int32)
        pmt = lax.broadcasted_iota(jnp.int32, (tm, tm), 1) == pos
        pmt_ref[...] = pmt.astype(BF16)
        dn = (((0,), (0,)), ((), ()))
        hs_ref[...] = lax.dot_general(pmt.astype(BF16), h_ref[0], dn, preferred_element_type=F32).astype(BF16)
        gs_ref[...] = lax.dot_general(pmt.astype(F32), rec, dn, precision=HIGHEST, preferred_element_type=F32)
        ys_ref[...] = jnp.zeros_like(ys_ref)

    grp = e // EXPERTS_PER_GROUP
    lane_c = lax.broadcasted_iota(jnp.int32, (chunk, ROUTER_LANES), 1)

    def body(c, carry):
        rows = pl.ds(pl.multiple_of(c * chunk, chunk), chunk)
        hb = hs_ref[rows, :]
        a = jnp.dot(hb, wg_ref[0], preferred_element_type=F32)
        u = jnp.dot(hb, wu_ref[0], preferred_element_type=F32)
        ge = jnp.sum(jnp.where(lane_c == e + N_GROUPS, gs_ref[rows, :], 0.0), axis=-1, keepdims=True)
        act = (_silu(a) * u * ge).astype(BF16)
        ys_ref[rows, :] += jnp.dot(act, wd_ref[0], preferred_element_type=F32)
        return carry

    lax.fori_loop(bounds_ref[2 * grp], bounds_ref[2 * grp + 1], body, 0)

    @pl.when(e == N_EXPERTS - 1)
    def _():
        o_ref[0] = jnp.dot(pmt_ref[...], ys_ref[...].astype(BF16), preferred_element_type=F32).astype(o_ref.dtype)


def _moe_experts(h, rec, w_gate, w_up, w_down):
    bsz, seq, d = h.shape
    tm = _pick(seq, (1024, 512))
    chunk = 128
    return pl.pallas_call(
        functools.partial(_moe_expert_kernel, tm=tm, chunk=chunk),
        grid=(bsz, seq // tm, N_EXPERTS),
        in_specs=[pl.BlockSpec((1, tm, d), lambda b, i, e: (b, i, 0)),
                  pl.BlockSpec((1, tm, ROUTER_LANES), lambda b, i, e: (b, i, 0)),
                  pl.BlockSpec((1, d, D_EXPERT), lambda b, i, e: (e, 0, 0)),
                  pl.BlockSpec((1, d, D_EXPERT), lambda b, i, e: (e, 0, 0)),
                  pl.BlockSpec((1, D_EXPERT, d), lambda b, i, e: (e, 0, 0))],
        out_specs=pl.BlockSpec((1, tm, d), lambda b, i, e: (b, i, 0)),
        out_shape=jax.ShapeDtypeStruct((bsz, seq, d), BF16),
        scratch_shapes=[pltpu.VMEM((tm, tm), BF16), pltpu.VMEM((tm, d), BF16), pltpu.VMEM((tm, ROUTER_LANES), F32),
                        pltpu.VMEM((tm, d), F32), pltpu.SMEM((2 * N_GROUPS,), jnp.int32)],
        compiler_params=_cparams(("parallel", "parallel", "arbitrary")),
        name="moe_experts",
    )(h, rec, w_gate, w_up, w_down)


def _resid_ln_kernel(x_ref, f_ref, g_ref, lng_ref, lnb_ref, o_ref):
    y = DEEPNORM_ALPHA * x_ref[0] + g_ref[0] * f_ref[0].astype(F32)
    o_ref[0] = _layer_norm(y, lng_ref[...], lnb_ref[...])


def _resid_ln(x, f, gate, ln_g, ln_b):
    bsz, seq, d = x.shape
    tm = _pick(seq, (512, 256))
    vec = pl.BlockSpec((1, d), lambda b, i: (0, 0))
    tile = pl.BlockSpec((1, tm, d), lambda b, i: (b, i, 0))
    return pl.pallas_call(
        _resid_ln_kernel,
        grid=(bsz, seq // tm),
        in_specs=[tile, tile, pl.BlockSpec((1, 1, d), lambda b, i: (b, 0, 0)), vec, vec],
        out_specs=tile,
        out_shape=jax.ShapeDtypeStruct((bsz, seq, d), F32),
        compiler_params=_cparams(("parallel", "parallel")),
        name="resid_ln",
    )(x, f, gate, ln_g.reshape(1, d), ln_b.reshape(1, d))


def _moe_ln(x, sc, sh, gate, w_router, b_router, w_gate, w_up, w_down, ln_g, ln_b):
    h, rec = _moe_route(x, sc, sh, w_router, b_router)
    f = _moe_experts(h, rec, w_gate, w_up, w_down)
    return _resid_ln(x, f, gate, ln_g, ln_b)


def _alibi_slopes(n):
    return jnp.asarray(2.0 ** (-8.0 * np.arange(1, n + 1) / n), dtype=F32)


def _colscale(q_ranges):
    cs = np.ones((1, PROJ_N), np.float32)
    for lo, hi, scale in q_ranges:
        cs[:, lo:hi] = scale
    return jnp.asarray(cs)


def _prep_params(p):
    q = dict(p)
    for name in ("w_in_ab", "w_out_ab", "w_in_cd", "w_out_cd", "w_gate", "w_up", "w_down"):
        q[name] = p[name].astype(BF16)
    pad = ROUTER_LANES - N_GROUPS - N_EXPERTS
    q["w_router"] = jnp.pad(jnp.concatenate([p["w_rg"], p["w_re"]], axis=-1), ((0, 0), (0, 0), (0, pad)))
    q["b_router"] = jnp.pad(jnp.concatenate([p["b_rg"], p["b_re"]], axis=-1), ((0, 0), (0, pad)))[:, None, :]
    q["na_bias"] = [_na_bias_table(p["rpb_d"][i]) for i in range(p["rpb_d"].shape[0])]
    return q


def _trunk(x, ada, p):
    for l in range(DEPTH):
        sh1, sc1, g1, sh2, sc2, g2 = jnp.split(ada[l][:, None, :], 6, axis=-1)
        i = l // 2
        if l % 2 == 0:
            lam_init = 0.8 - 0.6 * math.exp(-0.3 * l)
            proj = _proj(x, sc1, sh1, p["w_in_ab"][i], _colscale(((0, 1024, ATTN_SCALE), (1536, 2560, ATTN_SCALE * LOG2E))))
            oa = _win_attention(proj, p["sink_a"][i], _alibi_slopes(WIN_Q_HEADS))
            lamvec = jnp.stack([p["lam_q1"][i], p["lam_k1"][i], p["lam_q2"][i], p["lam_k2"][i]])
            ob = _diff_attention(proj, _alibi_slopes(DIFF_HEADS), lamvec, p["subln_g"][i][None, :], lam_init)
            w_out = p["w_out_ab"][i]
        else:
            proj = _proj(x, sc1, sh1, p["w_in_cd"][i], _colscale(((1536, 2560, ATTN_SCALE),)))
            qk = _normrope(proj, p["qnorm_c"][i], p["knorm_c"][i])
            oa = _gqa_attention(qk, proj)
            ob = _na_attention(proj, p["na_bias"][i])
            w_out = p["w_out_cd"][i]
        x = _outproj_ln(oa, ob, w_out, x, g1, p["ln_g"][l, 0], p["ln_b"][l, 0])
        x = _moe_ln(x, sc2, sh2, g2, p["w_router"][l], p["b_router"][l],
                    p["w_gate"][l], p["w_up"][l], p["w_down"][l], p["ln_g"][l, 1], p["ln_b"][l, 1])
    return x


def kernel(x_prompt, x_sample, c_prompt, c_sample, w_ada, b_ada, ln_g, ln_b, w_in_ab, w_out_ab, sink_a,
           lam_q1, lam_k1, lam_q2, lam_k2, subln_g, w_in_cd, w_out_cd, qnorm_c, knorm_c, rpb_d,
           w_rg, b_rg, w_re, b_re, w_gate, w_up, w_down):
    p = _prep_params(dict(
        ln_g=ln_g, ln_b=ln_b, w_in_ab=w_in_ab, w_out_ab=w_out_ab, sink_a=sink_a,
        lam_q1=lam_q1, lam_k1=lam_k1, lam_q2=lam_q2, lam_k2=lam_k2, subln_g=subln_g,
        w_in_cd=w_in_cd, w_out_cd=w_out_cd, qnorm_c=qnorm_c, knorm_c=knorm_c, rpb_d=rpb_d,
        w_rg=w_rg, b_rg=b_rg, w_re=w_re, b_re=b_re, w_gate=w_gate, w_up=w_up, w_down=w_down))
    nb_p, nb_s = c_prompt.shape[0], c_sample.shape[0]
    c_all = jnp.concatenate([c_prompt, c_sample, jnp.zeros((8 - nb_p - nb_s, D_MODEL), F32)], axis=0)
    ada = _ada(c_all, w_ada, b_ada)
    y_prompt = _trunk(x_prompt, ada[:, :nb_p], p)
    y_sample = _trunk(x_sample, ada[:, nb_p:nb_p + nb_s], p)
    return (y_prompt, y_sample)
```

```python
import functools
import math

import numpy as np
import jax
import jax.numpy as jnp
from jax import lax
from jax.experimental import pallas as pl
from jax.experimental.pallas import tpu as pltpu

F32 = jnp.float32
BF16 = jnp.bfloat16
HIGHEST = lax.Precision.HIGHEST

D_MODEL = 2048
DEPTH = 2
HEAD_DIM = 128
WINDOW = 128
GRID_W = 64
WIN_Q_HEADS = 8
WIN_KV_HEADS = 2
GQA_GROUP = 4
DIFF_HEADS = 4
DIFF_V_DIM = 256
AX_Q_HEADS = 8
AX_KV_HEADS = 2
NA_HEADS = 8
NA_ROWS = 8
NA_COLS = 16
ROPE_THETA = 10000.0
N_GROUPS = 4
EXPERTS_PER_GROUP = 4
N_EXPERTS = 16
D_EXPERT = 512
LN_EPS = 1e-5
RMS_EPS = 1e-6
NEG_INF = -1e30
DEEPNORM_ALPHA = (2 * DEPTH) ** 0.25
PROJ_N = 4608
ATTN_SCALE = HEAD_DIM ** -0.5
LOG2E = math.log2(math.e)
LOG2E_SPLIT = (1.4453125, -0.00262451171875, 7.033348083496094e-06)
LANES = 128
ROUTER_LANES = LANES
SORT_ALIGN = 16
VMEM_LIMIT = 56 * 1024 * 1024

NA_TQ = NA_ROWS * GRID_W
NA_WIN = 2 * NA_ROWS * GRID_W


def _cparams(sem):
    return pltpu.CompilerParams(dimension_semantics=sem, vmem_limit_bytes=VMEM_LIMIT)


def _pick(n, cands):
    for c in cands:
        if n % c == 0:
            return c
    raise ValueError(f"no tile in {cands} divides {n}")


def _layer_norm(y, g, b):
    mu = jnp.mean(y, axis=-1, keepdims=True)
    yc = y - mu
    var = jnp.mean(yc * yc, axis=-1, keepdims=True)
    return yc * lax.rsqrt(var + LN_EPS) * g + b


def _silu(a):
    return a / (1.0 + jnp.exp(-a))


def _ada_kernel(c_ref, w_ref, b_ref, o_ref):
    a = _silu(c_ref[...])
    o_ref[0] = jnp.dot(a, w_ref[0], precision=HIGHEST, preferred_element_type=F32) + b_ref[0]


def _ada(c_all, w_ada, b_ada):
    n = 6 * D_MODEL
    tn = 1024
    return pl.pallas_call(
        _ada_kernel,
        grid=(DEPTH, n // tn),
        in_specs=[pl.BlockSpec((8, D_MODEL), lambda l, j: (0, 0)),
                  pl.BlockSpec((1, D_MODEL, tn), lambda l, j: (l, 0, j)),
                  pl.BlockSpec((1, 1, tn), lambda l, j: (l, 0, j))],
        out_specs=pl.BlockSpec((1, 8, tn), lambda l, j: (l, 0, j)),
        out_shape=jax.ShapeDtypeStruct((DEPTH, 8, n), F32),
        compiler_params=_cparams(("parallel", "parallel")),
        name="ada",
    )(c_all, w_ada, b_ada.reshape(DEPTH, 1, n))


def _proj_kernel(x_ref, sc_ref, sh_ref, w_ref, cs_ref, o_ref, h_ref):
    @pl.when(pl.program_id(2) == 0)
    def _():
        h_ref[...] = (x_ref[0] * (1.0 + sc_ref[0]) + sh_ref[0]).astype(BF16)

    acc = jnp.dot(h_ref[...], w_ref[...], preferred_element_type=F32)
    o_ref[0] = (acc * cs_ref[...]).astype(o_ref.dtype)


def _proj(x, sc, sh, w, colscale):
    bsz, seq, d = x.shape
    n = w.shape[1]
    tm = _pick(seq, (512, 256))
    tn = 1536
    return pl.pallas_call(
        _proj_kernel,
        grid=(bsz, seq // tm, n // tn),
        in_specs=[pl.BlockSpec((1, tm, d), lambda b, i, j: (b, i, 0)),
                  pl.BlockSpec((1, 1, d), lambda b, i, j: (b, 0, 0)),
                  pl.BlockSpec((1, 1, d), lambda b, i, j: (b, 0, 0)),
                  pl.BlockSpec((d, tn), lambda b, i, j: (0, j)),
                  pl.BlockSpec((1, tn), lambda b, i, j: (0, j))],
        out_specs=pl.BlockSpec((1, tm, tn), lambda b, i, j: (b, i, j)),
        out_shape=jax.ShapeDtypeStruct((bsz, seq, n), BF16),
        scratch_shapes=[pltpu.VMEM((tm, d), BF16)],
        compiler_params=_cparams(("parallel", "parallel", "arbitrary")),
        name="proj",
    )(x, sc, sh, w, colscale)


def _win_kernel(sink_ref, slope_ref, q_ref, kp_ref, km_ref, kn_ref, vp_ref, vm_ref, vn_ref, o_ref, *, tq, seq):
    i = pl.program_id(1)
    kvh = pl.program_id(2)
    k = jnp.concatenate([kp_ref[0], km_ref[0], kn_ref[0]], axis=0)
    v = jnp.concatenate([vp_ref[0], vm_ref[0], vn_ref[0]], axis=0)
    w = tq + 2 * WINDOW
    qrel = lax.broadcasted_iota(jnp.int32, (tq, w), 0)
    krel = lax.broadcasted_iota(jnp.int32, (tq, w), 1) - WINDOW
    dist = jnp.abs(qrel - krel)
    kabs = krel + i * tq
    valid = (dist <= WINDOW) & (kabs >= 0) & (kabs < seq)
    distf = dist.astype(F32)
    for g in range(GQA_GROUP):
        h = kvh * GQA_GROUP + g
        qg = q_ref[0, :, g * HEAD_DIM:(g + 1) * HEAD_DIM]
        s = lax.dot_general(qg, k, (((1,), (1,)), ((), ())), preferred_element_type=F32)
        s = jnp.where(valid, s - slope_ref[h] * distf, NEG_INF)
        snk = sink_ref[h]
        m = jnp.maximum(jnp.max(s, axis=-1, keepdims=True), snk)
        p = jnp.exp(s - m)
        l = jnp.sum(p, axis=-1, keepdims=True) + jnp.exp(snk - m)
        o = jnp.dot(p.astype(BF16), v, preferred_element_type=F32) / l
        o_ref[0, :, g * HEAD_DIM:(g + 1) * HEAD_DIM] = o.astype(o_ref.dtype)


def _win_attention(proj, sink, slopes):
    bsz, seq, _ = proj.shape
    tq = 256
    r = tq // WINDOW
    nb = seq // WINDOW
    kcol = WIN_Q_HEADS
    vcol = WIN_Q_HEADS + WIN_KV_HEADS

    def prev(col):
        return pl.BlockSpec((1, WINDOW, HEAD_DIM), lambda b, i, k: (b, jnp.maximum(i * r - 1, 0), col + k))

    def main(col):
        return pl.BlockSpec((1, tq, HEAD_DIM), lambda b, i, k: (b, i, col + k))

    def nxt(col):
        return pl.BlockSpec((1, WINDOW, HEAD_DIM), lambda b, i, k: (b, jnp.minimum((i + 1) * r, nb - 1), col + k))

    smem = pl.BlockSpec(memory_space=pltpu.SMEM)
    return pl.pallas_call(
        functools.partial(_win_kernel, tq=tq, seq=seq),
        grid=(bsz, seq // tq, WIN_KV_HEADS),
        in_specs=[smem, smem,
                  pl.BlockSpec((1, tq, GQA_GROUP * HEAD_DIM), lambda b, i, k: (b, i, k)),
                  prev(kcol), main(kcol), nxt(kcol), prev(vcol), main(vcol), nxt(vcol)],
        out_specs=pl.BlockSpec((1, tq, GQA_GROUP * HEAD_DIM), lambda b, i, k: (b, i, k)),
        out_shape=jax.ShapeDtypeStruct((bsz, seq, WIN_Q_HEADS * HEAD_DIM), BF16),
        compiler_params=_cparams(("parallel", "parallel", "parallel")),
        name="win_attn",
    )(sink, slopes, proj, proj, proj, proj, proj, proj, proj)


def _online_update_t(s, vt, m_ref, l_ref, acc_ref, idx):
    m_old = m_ref[idx]
    m_new = jnp.maximum(m_old, jnp.max(s, axis=0, keepdims=True))
    alpha = jnp.exp2(m_old - m_new)
    p = jnp.exp2(s - m_new)
    l_ref[idx] = alpha * l_ref[idx] + jnp.sum(p, axis=0, keepdims=True)
    acc_ref[idx] = alpha * acc_ref[idx] + jnp.dot(vt, p.astype(BF16), preferred_element_type=F32)
    m_ref[idx] = m_new


def _split3(x):
    p1 = x.astype(BF16)
    r1 = x - p1.astype(F32)
    p2 = r1.astype(BF16)
    return p1, p2, (r1 - p2.astype(F32)).astype(BF16)


def _diff_kernel(slope_ref, q_ref, k1_ref, k2_ref, vt_ref, lam_ref, g_ref, o_ref, qa_ref, m_ref, l_ref, acc_ref,
                 *, tq, tk, cq, nk, lam_init):
    h = pl.program_id(1)
    i = pl.program_id(2)
    j = pl.program_id(3)

    @pl.when(j == 0)
    def _():
        m_ref[...] = jnp.full_like(m_ref, NEG_INF)
        l_ref[...] = jnp.zeros_like(l_ref)
        acc_ref[...] = jnp.zeros_like(acc_ref)
        qa_ref[:, 0:HEAD_DIM, :] = q_ref[0, 0]
        qa_ref[:, HEAD_DIM:, :] = jnp.zeros((2, HEAD_DIM, tq), BF16)

    slope = slope_ref[h]
    vt = vt_ref[0, 0]

    @pl.when(i != j)
    def _():
        sgn = jnp.where(i > j, slope, -slope)
        dq = (i * tq - j * tk).astype(F32) + lax.broadcasted_iota(jnp.int32, (1, tq), 1).astype(F32)
        c1, c2, c3 = _split3(-(sgn * LOG2E) * dq)
        row = lax.broadcasted_iota(jnp.int32, (16, tq), 0)
        feat = jnp.zeros((16, tq), F32)
        for r, piece in enumerate(LOG2E_SPLIT):
            feat = jnp.where(row == r, 32.0 * piece * sgn, feat)
            feat = jnp.where(row == 3 + r, piece * sgn, feat)
        for r, cpiece in enumerate((c1, c2, c3)):
            feat = jnp.where(row == 6 + r, cpiece.astype(F32), feat)
        feat = feat.astype(BF16)
        qa_ref[0, HEAD_DIM:HEAD_DIM + 16, :] = feat
        qa_ref[1, HEAD_DIM:HEAD_DIM + 16, :] = feat
        for c in range(tq // cq):
            cols = slice(c * cq, (c + 1) * cq)
            for mi, k_ref in enumerate((k1_ref, k2_ref)):
                s = jnp.dot(k_ref[0], qa_ref[mi, :, cols], preferred_element_type=F32)
                _online_update_t(s, vt, m_ref, l_ref, acc_ref, (mi, slice(None), cols))

    @pl.when(i == j)
    def _():
        neg_slope = -slope * LOG2E
        dbase = (lax.broadcasted_iota(jnp.int32, (tk, cq), 1)
                 - lax.broadcasted_iota(jnp.int32, (tk, cq), 0)).astype(F32)
        for c in range(tq // cq):
            cols = slice(c * cq, (c + 1) * cq)
            off = (i * tq + c * cq - j * tk).astype(F32)
            bias = jnp.abs(dbase + off) * neg_slope
            for mi, k_ref in enumerate((k1_ref, k2_ref)):
                s = jnp.dot(k_ref[0, :, 0:HEAD_DIM], qa_ref[mi, 0:HEAD_DIM, cols], preferred_element_type=F32) + bias
                _online_update_t(s, vt, m_ref, l_ref, acc_ref, (mi, slice(None), cols))

    @pl.when(j == nk - 1)
    def _():
        lv = lam_ref[...]
        lam = (jnp.exp(jnp.sum(lv[0:1] * lv[1:2], axis=-1, keepdims=True))
               - jnp.exp(jnp.sum(lv[2:3] * lv[3:4], axis=-1, keepdims=True)) + lam_init)
        ot = acc_ref[0] / l_ref[0] - lam * (acc_ref[1] / l_ref[1])
        o = ot.T
        r = lax.rsqrt(jnp.mean(o * o, axis=-1, keepdims=True) + RMS_EPS)
        o_ref[0] = (o * r * g_ref[...] * (1.0 - lam_init)).astype(o_ref.dtype)


def _diff_attention(proj, slopes, lamvec, subln_g, lam_init):
    bsz, seq, _ = proj.shape
    tq = _pick(seq, (1024, 512))
    tk = _pick(seq, (1024, 512))
    assert tq == tk
    cq = 256
    nk = seq // tk
    qt = proj[:, :, 1536:2560].reshape(bsz, seq, DIFF_HEADS, 2, HEAD_DIM).transpose(0, 2, 3, 4, 1)
    vt = proj[:, :, 3584:4608].reshape(bsz, seq, DIFF_HEADS, DIFF_V_DIM).transpose(0, 2, 3, 1)
    kr = np.arange(seq) % tk
    feats = np.zeros((seq, HEAD_DIM), np.float32)
    feats[:, 0:3] = (kr // 32)[:, None]
    feats[:, 3:6] = (kr % 32)[:, None]
    feats[:, 6:9] = 1.0
    feats = jnp.broadcast_to(jnp.asarray(feats, BF16)[None, :, None, :], (bsz, seq, 2 * DIFF_HEADS, HEAD_DIM))
    kaug = jnp.concatenate([proj[:, :, 2560:3584].reshape(bsz, seq, 2 * DIFF_HEADS, HEAD_DIM), feats], axis=-1)
    kaug = kaug.reshape(bsz, seq, 2 * DIFF_HEADS * 2 * HEAD_DIM)
    smem = pl.BlockSpec(memory_space=pltpu.SMEM)

    def kspec(off):
        return pl.BlockSpec((1, tk, 2 * HEAD_DIM), lambda b, h, i, j: (b, j, 2 * h + off))

    return pl.pallas_call(
        functools.partial(_diff_kernel, tq=tq, tk=tk, cq=cq, nk=nk, lam_init=lam_init),
        grid=(bsz, DIFF_HEADS, seq // tq, nk),
        in_specs=[smem,
                  pl.BlockSpec((1, 1, 2, HEAD_DIM, tq), lambda b, h, i, j: (b, h, 0, 0, i)),
                  kspec(0), kspec(1),
                  pl.BlockSpec((1, 1, DIFF_V_DIM, tk), lambda b, h, i, j: (b, h, 0, j)),
                  pl.BlockSpec((4, HEAD_DIM), lambda b, h, i, j: (0, 0)),
                  pl.BlockSpec((1, DIFF_V_DIM), lambda b, h, i, j: (0, 0))],
        out_specs=pl.BlockSpec((1, tq, DIFF_V_DIM), lambda b, h, i, j: (b, i, h)),
        out_shape=jax.ShapeDtypeStruct((bsz, seq, DIFF_HEADS * DIFF_V_DIM), BF16),
        scratch_shapes=[pltpu.VMEM((2, 2 * HEAD_DIM, tq), BF16),
                        pltpu.VMEM((2, 1, tq), F32), pltpu.VMEM((2, 1, tq), F32),
                        pltpu.VMEM((2, DIFF_V_DIM, tq), F32)],
        compiler_params=_cparams(("parallel", "parallel", "parallel", "arbitrary")),
        name="diff_attn",
    )(slopes, qt, kaug, kaug, vt, lamvec, subln_g)


def _normrope_kernel(x_ref, cos_ref, sin_ref, gq_ref, gk_ref, o_ref, *, tm):
    j = pl.program_id(2)
    is_q = j < AX_Q_HEADS // 2
    g = jnp.where(is_q, gq_ref[...], gk_ref[...])
    post = jnp.where(is_q, ATTN_SCALE * LOG2E, 1.0).astype(F32)
    lane = lax.broadcasted_iota(jnp.int32, (tm, HEAD_DIM), 1)
    first = (lane % 64) < 32
    cos = cos_ref[...]
    sin = sin_ref[...]
    for hh in range(2):
        x = x_ref[0, :, hh * HEAD_DIM:(hh + 1) * HEAD_DIM].astype(F32)
        xn = x * lax.rsqrt(jnp.mean(x * x, axis=-1, keepdims=True) + RMS_EPS) * g
        partner = jnp.where(first, pltpu.roll(xn, 96, 1), pltpu.roll(xn, 32, 1))
        y = (xn * cos + partner * sin) * post
        o_ref[0, :, hh * HEAD_DIM:(hh + 1) * HEAD_DIM] = y.astype(o_ref.dtype)


def _rope_tables(seq):
    half = HEAD_DIM // 2
    inv = jnp.asarray(1.0 / (ROPE_THETA ** (np.arange(0, half, 2) / half)), dtype=F32)
    t = jnp.arange(seq)
    ang_r = (t // GRID_W).astype(F32)[:, None] * inv[None, :]
    ang_c = (t % GRID_W).astype(F32)[:, None] * inv[None, :]
    cr, sr, cc, sn = jnp.cos(ang_r), jnp.sin(ang_r), jnp.cos(ang_c), jnp.sin(ang_c)
    return (jnp.concatenate([cr, cr, cc, cc], axis=-1), jnp.concatenate([-sr, sr, -sn, sn], axis=-1))


def _normrope(proj, qnorm_g, knorm_g):
    bsz, seq, _ = proj.shape
    tm = _pick(seq, (512,))
    cos, sin = _rope_tables(seq)
    nw = (AX_Q_HEADS + AX_KV_HEADS) * HEAD_DIM
    return pl.pallas_call(
        functools.partial(_normrope_kernel, tm=tm),
        grid=(bsz, seq // tm, nw // 256),
        in_specs=[pl.BlockSpec((1, tm, 256), lambda b, i, j: (b, i, j)),
                  pl.BlockSpec((tm, HEAD_DIM), lambda b, i, j: (i, 0)),
                  pl.BlockSpec((tm, HEAD_DIM), lambda b, i, j: (i, 0)),
                  pl.BlockSpec((1, HEAD_DIM), lambda b, i, j: (0, 0)),
                  pl.BlockSpec((1, HEAD_DIM), lambda b, i, j: (0, 0))],
        out_specs=pl.BlockSpec((1, tm, 256), lambda b, i, j: (b, i, j)),
        out_shape=jax.ShapeDtypeStruct((bsz, seq, nw), BF16),
        compiler_params=_cparams(("parallel", "parallel", "parallel")),
        name="normrope",
    )(proj, cos, sin, qnorm_g.reshape(1, HEAD_DIM), knorm_g.reshape(1, HEAD_DIM))


def _gqa_kernel(q_ref, k_ref, vt_ref, o_ref, m_ref, l_ref, acc_ref, *, tq, cq, nk):
    j = pl.program_id(3)

    @pl.when(j == 0)
    def _():
        m_ref[...] = jnp.full_like(m_ref, NEG_INF)
        l_ref[...] = jnp.zeros_like(l_ref)
        acc_ref[...] = jnp.zeros_like(acc_ref)

    k = k_ref[0]
    vt = vt_ref[0, 0]
    per = tq // cq
    for c in range(GQA_GROUP * per):
        g, sub = divmod(c, per)
        s = jnp.dot(k, q_ref[0, g, :, sub * cq:(sub + 1) * cq], preferred_element_type=F32)
        _online_update_t(s, vt, m_ref, l_ref, acc_ref, (slice(None), slice(c * cq, (c + 1) * cq)))

    @pl.when(j == nk - 1)
    def _():
        ot = acc_ref[...] / l_ref[...]
        for g in range(GQA_GROUP):
            o_ref[0, :, g * HEAD_DIM:(g + 1) * HEAD_DIM] = ot[:, g * tq:(g + 1) * tq].T.astype(o_ref.dtype)


def _gqa_attention(qk, proj):
    bsz, seq, _ = qk.shape
    tq = _pick(seq, (512,))
    tk = _pick(seq, (1024, 512))
    cq = 128
    nk = seq // tk
    gw = GQA_GROUP * HEAD_DIM
    qt = qk[:, :, :AX_Q_HEADS * HEAD_DIM].reshape(bsz, seq, AX_Q_HEADS, HEAD_DIM).transpose(0, 2, 3, 1)
    vt = proj[:, :, 1280:1536].reshape(bsz, seq, AX_KV_HEADS, HEAD_DIM).transpose(0, 2, 3, 1)
    return pl.pallas_call(
        functools.partial(_gqa_kernel, tq=tq, cq=cq, nk=nk),
        grid=(bsz, AX_KV_HEADS, seq // tq, nk),
        in_specs=[pl.BlockSpec((1, GQA_GROUP, HEAD_DIM, tq), lambda b, h, i, j: (b, h, 0, i)),
                  pl.BlockSpec((1, tk, HEAD_DIM), lambda b, h, i, j: (b, j, AX_Q_HEADS + h)),
                  pl.BlockSpec((1, 1, HEAD_DIM, tk), lambda b, h, i, j: (b, h, 0, j))],
        out_specs=pl.BlockSpec((1, tq, gw), lambda b, h, i, j: (b, i, h)),
        out_shape=jax.ShapeDtypeStruct((bsz, seq, AX_Q_HEADS * HEAD_DIM), BF16),
        scratch_shapes=[pltpu.VMEM((1, GQA_GROUP * tq), F32), pltpu.VMEM((1, GQA_GROUP * tq), F32),
                        pltpu.VMEM((HEAD_DIM, GQA_GROUP * tq), F32)],
        compiler_params=_cparams(("parallel", "parallel", "parallel", "arbitrary")),
        name="gqa_attn",
    )(qt, qk, vt)


def _na_bias_table(rpb):
    qi = np.arange(NA_ROWS)[:, None, None, None]
    qc = np.arange(GRID_W)[None, :, None, None]
    kj = np.arange(2 * NA_ROWS)[None, None, :, None]
    kc = np.arange(GRID_W)[None, None, None, :]
    c_start = np.clip(qc - NA_COLS // 2, 0, GRID_W - NA_COLS)
    col_ok = (kc >= c_start) & (kc < c_start + NA_COLS)
    rel_c = np.clip(kc - qc + NA_COLS - 1, 0, 2 * NA_COLS - 2)
    onehot_c = jnp.asarray(rel_c[0, :, 0, :, None] == np.arange(2 * NA_COLS - 1), dtype=F32)
    tabs = []
    for off, clamp in ((0, np.maximum), (-NA_ROWS // 2, None), (-NA_ROWS, np.minimum)):
        rs = qi - NA_ROWS // 2
        if clamp is not None:
            rs = clamp(rs, 0)
        kr = off + kj
        ok = (kr >= rs) & (kr < rs + NA_ROWS) & col_ok
        rel_r = np.clip(kr - qi + NA_ROWS - 1, 0, 2 * NA_ROWS - 2)[:, 0, :, 0]
        onehot_r = jnp.asarray(rel_r[:, :, None] == np.arange(2 * NA_ROWS - 1), dtype=F32)
        bias = jnp.einsum("ija,hab,cdb->hicjd", onehot_r, rpb.astype(F32), onehot_c, precision=HIGHEST)
        tabs.append(jnp.where(ok[None], bias, NEG_INF).reshape(NA_HEADS, NA_TQ, NA_WIN))
    return jnp.stack(tabs)


def _na_kernel(q_ref, k_ref, v_ref, bias_ref, o_ref, *, rows):
    i = pl.program_id(2)
    ws = jnp.clip(i * NA_ROWS - NA_ROWS // 2, 0, rows - 2 * NA_ROWS)
    start = pl.multiple_of(ws * GRID_W, GRID_W)
    k = k_ref[0, pl.ds(start, NA_WIN), :]
    v = v_ref[0, pl.ds(start, NA_WIN), :]
    s = lax.dot_general(q_ref[0], k, (((1,), (1,)), ((), ())), preferred_element_type=F32) + bias_ref[0, 0]
    m = jnp.max(s, axis=-1, keepdims=True)
    p = jnp.exp(s - m)
    l = jnp.sum(p, axis=-1, keepdims=True)
    o_ref[0] = (jnp.dot(p.astype(BF16), v, preferred_element_type=F32) / l).astype(o_ref.dtype)


def _na_attention(proj, bias_tab):
    bsz, seq, _ = proj.shape
    rows = seq // GRID_W
    nq = seq // NA_TQ
    assert rows >= 3 * NA_ROWS and seq % NA_TQ == 0
    qcol, kcol, vcol = 1536 // HEAD_DIM, 2560 // HEAD_DIM, 3584 // HEAD_DIM

    def case(i):
        return jnp.where(i == 0, 0, jnp.where(i == nq - 1, 2, 1))

    return pl.pallas_call(
        functools.partial(_na_kernel, rows=rows),
        grid=(bsz, NA_HEADS, nq),
        in_specs=[pl.BlockSpec((1, NA_TQ, HEAD_DIM), lambda b, h, i: (b, i, qcol + h)),
                  pl.BlockSpec((1, seq, HEAD_DIM), lambda b, h, i: (b, 0, kcol + h)),
                  pl.BlockSpec((1, seq, HEAD_DIM), lambda b, h, i: (b, 0, vcol + h)),
                  pl.BlockSpec((1, 1, NA_TQ, NA_WIN), lambda b, h, i: (case(i), h, 0, 0))],
        out_specs=pl.BlockSpec((1, NA_TQ, HEAD_DIM), lambda b, h, i: (b, i, h)),
        out_shape=jax.ShapeDtypeStruct((bsz, seq, NA_HEADS * HEAD_DIM), BF16),
        compiler_params=_cparams(("parallel", "parallel", "arbitrary")),
        name="na_attn",
    )(proj, proj, proj, bias_tab)


def _outproj_kernel(oa_ref, ob_ref, wa_ref, wb_ref, x_ref, g_ref, lng_ref, lnb_ref, o_ref):
    m = (jnp.dot(oa_ref[0], wa_ref[...], preferred_element_type=F32)
         + jnp.dot(ob_ref[0], wb_ref[...], preferred_element_type=F32))
    y = DEEPNORM_ALPHA * x_ref[0] + g_ref[0] * m
    o_ref[0] = _layer_norm(y, lng_ref[...], lnb_ref[...])


def _outproj_ln(oa, ob, w_out, x, gate, ln_g, ln_b):
    bsz, seq, d = x.shape
    half = oa.shape[-1]
    tm = _pick(seq, (512, 256))
    vec = pl.BlockSpec((1, d), lambda b, i: (0, 0))
    return pl.pallas_call(
        _outproj_kernel,
        grid=(bsz, seq // tm),
        in_specs=[pl.BlockSpec((1, tm, half), lambda b, i: (b, i, 0)),
                  pl.BlockSpec((1, tm, half), lambda b, i: (b, i, 0)),
                  pl.BlockSpec((half, d), lambda b, i: (0, 0)),
                  pl.BlockSpec((half, d), lambda b, i: (1, 0)),
                  pl.BlockSpec((1, tm, d), lambda b, i: (b, i, 0)),
                  pl.BlockSpec((1, 1, d), lambda b, i: (b, 0, 0)),
                  vec, vec],
        out_specs=pl.BlockSpec((1, tm, d), lambda b, i: (b, i, 0)),
        out_shape=jax.ShapeDtypeStruct((bsz, seq, d), F32),
        compiler_params=_cparams(("parallel", "parallel")),
        name="outproj_ln",
    )(oa, ob, w_out, w_out, x, gate, ln_g.reshape(1, d), ln_b.reshape(1, d))


def _route(logits):
    lane = lax.broadcasted_iota(jnp.int32, logits.shape, 1)
    big = jnp.int32(2 ** 30)
    is_g = lane < N_GROUPS
    mg = jnp.max(jnp.where(is_g, logits, -jnp.inf), axis=-1, keepdims=True)
    g_idx = jnp.min(jnp.where(is_g & (logits == mg), lane, big), axis=-1, keepdims=True)
    p_g = 1.0 / jnp.sum(jnp.where(is_g, jnp.exp(logits - mg), 0.0), axis=-1, keepdims=True)
    lo = N_GROUPS + EXPERTS_PER_GROUP * g_idx
    in_grp = (lane >= lo) & (lane < lo + EXPERTS_PER_GROUP)
    v1 = jnp.max(jnp.where(in_grp, logits, -jnp.inf), axis=-1, keepdims=True)
    i1 = jnp.min(jnp.where(in_grp & (logits == v1), lane, big), axis=-1, keepdims=True)
    rest = in_grp & (lane != i1)
    v2 = jnp.max(jnp.where(rest, logits, -jnp.inf), axis=-1, keepdims=True)
    i2 = jnp.min(jnp.where(rest & (logits == v2), lane, big), axis=-1, keepdims=True)
    e2 = jnp.exp(v2 - v1)
    w1 = p_g / (1.0 + e2)
    w2 = p_g * e2 / (1.0 + e2)
    gates = jnp.where(lane == i1, w1, 0.0) + jnp.where(lane == i2, w2, 0.0)
    return jnp.where(lane == 0, g_idx.astype(F32), gates)


def _moe_route_kernel(x_ref, sc_ref, sh_ref, wr_ref, br_ref, h_ref, gate_ref):
    h = x_ref[0] * (1.0 + sc_ref[0]) + sh_ref[0]
    h_ref[0] = h.astype(BF16)
    logits = jnp.dot(h, wr_ref[...], precision=HIGHEST, preferred_element_type=F32) + br_ref[...]
    gate_ref[0] = _route(logits)


def _moe_route(x, sc, sh, w_router, b_router):
    bsz, seq, d = x.shape
    tm = _pick(seq, (512, 256))
    mod = pl.BlockSpec((1, 1, d), lambda b, i: (b, 0, 0))
    return pl.pallas_call(
        _moe_route_kernel,
        grid=(bsz, seq // tm),
        in_specs=[pl.BlockSpec((1, tm, d), lambda b, i: (b, i, 0)), mod, mod,
                  pl.BlockSpec((d, ROUTER_LANES), lambda b, i: (0, 0)),
                  pl.BlockSpec((1, ROUTER_LANES), lambda b, i: (0, 0))],
        out_specs=[pl.BlockSpec((1, tm, d), lambda b, i: (b, i, 0)),
                   pl.BlockSpec((1, tm, ROUTER_LANES), lambda b, i: (b, i, 0))],
        out_shape=[jax.ShapeDtypeStruct((bsz, seq, d), BF16),
                   jax.ShapeDtypeStruct((bsz, seq, ROUTER_LANES), F32)],
        compiler_params=_cparams(("parallel", "parallel")),
        name="moe_route",
    )(x, sc, sh, w_router, b_router)


def _moe_expert_kernel(h_ref, gate_ref, wg_ref, wu_ref, wd_ref, o_ref, pmt_ref, hs_ref, gs_ref, ys_ref, bounds_ref,
                       *, tm, chunk):
    e = pl.program_id(2)

    @pl.when(e == 0)
    def _():
        rec = gate_ref[0]
        lane = lax.broadcasted_iota(jnp.int32, (tm, ROUTER_LANES), 1)
        onehot = jnp.where((lane < N_GROUPS) & (lane == rec[:, 0:1].astype(jnp.int32)), 1.0, 0.0)
        below = (lax.broadcasted_iota(jnp.int32, (tm, tm), 1)
                 < lax.broadcasted_iota(jnp.int32, (tm, tm), 0)).astype(BF16)
        rank = jnp.dot(below, onehot.astype(BF16), preferred_element_type=F32)
        cnt = jnp.sum(onehot, axis=0, keepdims=True)
        padded = jnp.floor((cnt + (SORT_ALIGN - 1)) * (1.0 / SORT_ALIGN)) * SORT_ALIGN
        lane1 = lax.broadcasted_iota(jnp.int32, (1, ROUTER_LANES), 1)
        starts = jnp.zeros((1, ROUTER_LANES), F32)
        for g in range(N_GROUPS):
            first = jnp.sum(jnp.where(lane1 < g, padded, 0.0))
            size = jnp.sum(jnp.where(lane1 == g, cnt, 0.0))
            starts = starts + jnp.where(lane1 == g, first, 0.0)
            bounds_ref[2 * g] = first.astype(jnp.int32)
            bounds_ref[2 * g + 1] = (size.astype(jnp.int32) + chunk - 1) // chunk
        pos = jnp.sum(onehot * (rank + starts), axis=1, keepdims=True).astype(jnp.int32)
        pmt = (lax.broadcasted_iota(jnp.int32, pmt_ref.shape, 1) == pos).astype(BF16)
        pmt_ref[...] = pmt
        dn = (((0,), (0,)), ((), ()))
        hs_ref[...] = lax.dot_general(pmt, h_ref[0], dn, preferred_element_type=F32).astype(BF16)
        gs = jnp.zeros(gs_ref.shape, F32)
        for piece in _split3(rec):
            gs = gs + lax.dot_general(pmt, piece, dn, preferred_element_type=F32)
        gs_ref[...] = gs
        ys_ref[...] = jnp.zeros_like(ys_ref)

    grp = e // EXPERTS_PER_GROUP
    lane_c = lax.broadcasted_iota(jnp.int32, (chunk, ROUTER_LANES), 1)
    first_row = bounds_ref[2 * grp]

    def body(c, carry):
        rows = pl.ds(pl.multiple_of(first_row + c * chunk, SORT_ALIGN), chunk)
        hb = hs_ref[rows, :]
        a = jnp.dot(hb, wg_ref[0], preferred_element_type=F32)
        u = jnp.dot(hb, wu_ref[0], preferred_element_type=F32)
        ge = jnp.sum(jnp.where(lane_c == e + N_GROUPS, gs_ref[rows, :], 0.0), axis=-1, keepdims=True)
        act = (_silu(a) * u * ge).astype(BF16)
        ys_ref[rows, :] += jnp.dot(act, wd_ref[0], preferred_element_type=F32)
        return carry

    lax.fori_loop(0, bounds_ref[2 * grp + 1], body, 0)

    @pl.when(e == N_EXPERTS - 1)
    def _():
        o_ref[0] = jnp.dot(pmt_ref[...], ys_ref[...].astype(BF16), preferred_element_type=F32).astype(o_ref.dtype)


def _moe_experts(h, rec, w_gate, w_up, w_down):
    bsz, seq, d = h.shape
    tm = _pick(seq, (1024, 512))
    chunk = 128
    ps = tm + 2 * chunk
    assert N_GROUPS * (SORT_ALIGN - 1) + chunk <= 2 * chunk
    return pl.pallas_call(
        functools.partial(_moe_expert_kernel, tm=tm, chunk=chunk),
        grid=(bsz, seq // tm, N_EXPERTS),
        in_specs=[pl.BlockSpec((1, tm, d), lambda b, i, e: (b, i, 0)),
                  pl.BlockSpec((1, tm, ROUTER_LANES), lambda b, i, e: (b, i, 0)),
                  pl.BlockSpec((1, d, D_EXPERT), lambda b, i, e: (e, 0, 0)),
                  pl.BlockSpec((1, d, D_EXPERT), lambda b, i, e: (e, 0, 0)),
                  pl.BlockSpec((1, D_EXPERT, d), lambda b, i, e: (e, 0, 0))],
        out_specs=pl.BlockSpec((1, tm, d), lambda b, i, e: (b, i, 0)),
        out_shape=jax.ShapeDtypeStruct((bsz, seq, d), BF16),
        scratch_shapes=[pltpu.VMEM((tm, ps), BF16), pltpu.VMEM((ps, d), BF16), pltpu.VMEM((ps, ROUTER_LANES), F32),
                        pltpu.VMEM((ps, d), F32), pltpu.SMEM((2 * N_GROUPS,), jnp.int32)],
        compiler_params=_cparams(("parallel", "parallel", "arbitrary")),
        name="moe_experts",
    )(h, rec, w_gate, w_up, w_down)


def _resid_ln_kernel(x_ref, f_ref, g_ref, lng_ref, lnb_ref, o_ref):
    y = DEEPNORM_ALPHA * x_ref[0] + g_ref[0] * f_ref[0].astype(F32)
    o_ref[0] = _layer_norm(y, lng_ref[...], lnb_ref[...])


def _resid_ln(x, f, gate, ln_g, ln_b):
    bsz, seq, d = x.shape
    tm = _pick(seq, (512, 256))
    vec = pl.BlockSpec((1, d), lambda b, i: (0, 0))
    tile = pl.BlockSpec((1, tm, d), lambda b, i: (b, i, 0))
    return pl.pallas_call(
        _resid_ln_kernel,
        grid=(bsz, seq // tm),
        in_specs=[tile, tile, pl.BlockSpec((1, 1, d), lambda b, i: (b, 0, 0)), vec, vec],
        out_specs=tile,
        out_shape=jax.ShapeDtypeStruct((bsz, seq, d), F32),
        compiler_params=_cparams(("parallel", "parallel")),
        name="resid_ln",
    )(x, f, gate, ln_g.reshape(1, d), ln_b.reshape(1, d))


def _moe_ln(x, sc, sh, gate, w_router, b_router, w_gate, w_up, w_down, ln_g, ln_b):
    h, rec = _moe_route(x, sc, sh, w_router, b_router)
    f = _moe_experts(h, rec, w_gate, w_up, w_down)
    return _resid_ln(x, f, gate, ln_g, ln_b)


def _alibi_slopes(n):
    return jnp.asarray(2.0 ** (-8.0 * np.arange(1, n + 1) / n), dtype=F32)


def _colscale(q_ranges):
    cs = np.ones((1, PROJ_N), np.float32)
    for lo, hi, scale in q_ranges:
        cs[:, lo:hi] = scale
    return jnp.asarray(cs)


def _prep_params(p):
    q = dict(p)
    for name in ("w_in_ab", "w_out_ab", "w_in_cd", "w_out_cd", "w_gate", "w_up", "w_down"):
        q[name] = p[name].astype(BF16)
    pad = ROUTER_LANES - N_GROUPS - N_EXPERTS
    q["w_router"] = jnp.pad(jnp.concatenate([p["w_rg"], p["w_re"]], axis=-1), ((0, 0), (0, 0), (0, pad)))
    q["b_router"] = jnp.pad(jnp.concatenate([p["b_rg"], p["b_re"]], axis=-1), ((0, 0), (0, pad)))[:, None, :]
    q["na_bias"] = [_na_bias_table(p["rpb_d"][i]) for i in range(p["rpb_d"].shape[0])]
    return q


def _trunk(x, ada, p):
    for l in range(DEPTH):
        sh1, sc1, g1, sh2, sc2, g2 = jnp.split(ada[l][:, None, :], 6, axis=-1)
        i = l // 2
        if l % 2 == 0:
            lam_init = 0.8 - 0.6 * math.exp(-0.3 * l)
            proj = _proj(x, sc1, sh1, p["w_in_ab"][i], _colscale(((0, 1024, ATTN_SCALE), (1536, 2560, ATTN_SCALE * LOG2E))))
            oa = _win_attention(proj, p["sink_a"][i], _alibi_slopes(WIN_Q_HEADS))
            lamvec = jnp.stack([p["lam_q1"][i], p["lam_k1"][i], p["lam_q2"][i], p["lam_k2"][i]])
            ob = _diff_attention(proj, _alibi_slopes(DIFF_HEADS), lamvec, p["subln_g"][i][None, :], lam_init)
            w_out = p["w_out_ab"][i]
        else:
            proj = _proj(x, sc1, sh1, p["w_in_cd"][i], _colscale(((1536, 2560, ATTN_SCALE),)))
            qk = _normrope(proj, p["qnorm_c"][i], p["knorm_c"][i])
            oa = _gqa_attention(qk, proj)
            ob = _na_attention(proj, p["na_bias"][i])
            w_out = p["w_out_cd"][i]
        x = _outproj_ln(oa, ob, w_out, x, g1, p["ln_g"][l, 0], p["ln_b"][l, 0])
        x = _moe_ln(x, sc2, sh2, g2, p["w_router"][l], p["b_router"][l],
                    p["w_gate"][l], p["w_up"][l], p["w_down"][l], p["ln_g"][l, 1], p["ln_b"][l, 1])
    return x


def kernel(x_prompt, x_sample, c_prompt, c_sample, w_ada, b_ada, ln_g, ln_b, w_in_ab, w_out_ab, sink_a,
           lam_q1, lam_k1, lam_q2, lam_k2, subln_g, w_in_cd, w_out_cd, qnorm_c, knorm_c, rpb_d,
           w_rg, b_rg, w_re, b_re, w_gate, w_up, w_down):
    p = _prep_params(dict(
        ln_g=ln_g, ln_b=ln_b, w_in_ab=w_in_ab, w_out_ab=w_out_ab, sink_a=sink_a,
        lam_q1=lam_q1, lam_k1=lam_k1, lam_q2=lam_q2, lam_k2=lam_k2, subln_g=subln_g,
        w_in_cd=w_in_cd, w_out_cd=w_out_cd, qnorm_c=qnorm_c, knorm_c=knorm_c, rpb_d=rpb_d,
        w_rg=w_rg, b_rg=b_rg, w_re=w_re, b_re=b_re, w_gate=w_gate, w_up=w_up, w_down=w_down))
    nb_p, nb_s = c_prompt.shape[0], c_sample.shape[0]
    c_all = jnp.concatenate([c_prompt, c_sample, jnp.zeros((8 - nb_p - nb_s, D_MODEL), F32)], axis=0)
    ada = _ada(c_all, w_ada, b_ada)
    y_prompt = _trunk(x_prompt, ada[:, :nb_p], p)
    y_sample = _trunk(x_sample, ada[:, nb_p:nb_p + nb_s], p)
    return (y_prompt, y_sample)
```

```python
import functools
import math

import numpy as np
import jax
import jax.numpy as jnp
from jax import lax
from jax.experimental import pallas as pl
from jax.experimental.pallas import tpu as pltpu

F32 = jnp.float32
BF16 = jnp.bfloat16
HIGHEST = lax.Precision.HIGHEST

D_MODEL = 2048
DEPTH = 2
HEAD_DIM = 128
WINDOW = 128
GRID_W = 64
WIN_Q_HEADS = 8
WIN_KV_HEADS = 2
GQA_GROUP = 4
DIFF_HEADS = 4
DIFF_V_DIM = 256
AX_Q_HEADS = 8
AX_KV_HEADS = 2
NA_HEADS = 8
NA_ROWS = 8
NA_COLS = 16
ROPE_THETA = 10000.0
N_GROUPS = 4
EXPERTS_PER_GROUP = 4
N_EXPERTS = 16
D_EXPERT = 512
LN_EPS = 1e-5
RMS_EPS = 1e-6
NEG_INF = -1e30
DEEPNORM_ALPHA = (2 * DEPTH) ** 0.25
PROJ_N = 4608
ATTN_SCALE = HEAD_DIM ** -0.5
LOG2E = math.log2(math.e)
LANES = 128
ROUTER_LANES = LANES
SORT_ALIGN = 16
VMEM_LIMIT = 56 * 1024 * 1024

NA_TQ = NA_ROWS * GRID_W
NA_WIN = 2 * NA_ROWS * GRID_W


def _cparams(sem):
    return pltpu.CompilerParams(dimension_semantics=sem, vmem_limit_bytes=VMEM_LIMIT)


def _pick(n, cands):
    for c in cands:
        if n % c == 0:
            return c
    raise ValueError(f"no tile in {cands} divides {n}")


def _layer_norm(y, g, b):
    mu = jnp.mean(y, axis=-1, keepdims=True)
    yc = y - mu
    var = jnp.mean(yc * yc, axis=-1, keepdims=True)
    return yc * lax.rsqrt(var + LN_EPS) * g + b


def _silu(a):
    return a / (1.0 + jnp.exp(-a))


def _ada_kernel(c_ref, w_ref, b_ref, o_ref):
    a = _silu(c_ref[...])
    o_ref[0] = jnp.dot(a, w_ref[0], precision=HIGHEST, preferred_element_type=F32) + b_ref[0]


def _ada(c_all, w_ada, b_ada):
    n = 6 * D_MODEL
    tn = 1024
    return pl.pallas_call(
        _ada_kernel,
        grid=(DEPTH, n // tn),
        in_specs=[pl.BlockSpec((8, D_MODEL), lambda l, j: (0, 0)),
                  pl.BlockSpec((1, D_MODEL, tn), lambda l, j: (l, 0, j)),
                  pl.BlockSpec((1, 1, tn), lambda l, j: (l, 0, j))],
        out_specs=pl.BlockSpec((1, 8, tn), lambda l, j: (l, 0, j)),
        out_shape=jax.ShapeDtypeStruct((DEPTH, 8, n), F32),
        compiler_params=_cparams(("parallel", "parallel")),
        name="ada",
    )(c_all, w_ada, b_ada.reshape(DEPTH, 1, n))


def _proj_kernel(x_ref, sc_ref, sh_ref, w_ref, cs_ref, o_ref, h_ref):
    @pl.when(pl.program_id(2) == 0)
    def _():
        h_ref[...] = (x_ref[0] * (1.0 + sc_ref[0]) + sh_ref[0]).astype(BF16)

    acc = jnp.dot(h_ref[...], w_ref[...], preferred_element_type=F32)
    o_ref[0] = (acc * cs_ref[...]).astype(o_ref.dtype)


def _proj(x, sc, sh, w, colscale):
    bsz, seq, d = x.shape
    n = w.shape[1]
    tm = _pick(seq, (512, 256))
    tn = 1536
    return pl.pallas_call(
        _proj_kernel,
        grid=(bsz, seq // tm, n // tn),
        in_specs=[pl.BlockSpec((1, tm, d), lambda b, i, j: (b, i, 0)),
                  pl.BlockSpec((1, 1, d), lambda b, i, j: (b, 0, 0)),
                  pl.BlockSpec((1, 1, d), lambda b, i, j: (b, 0, 0)),
                  pl.BlockSpec((d, tn), lambda b, i, j: (0, j)),
                  pl.BlockSpec((1, tn), lambda b, i, j: (0, j))],
        out_specs=pl.BlockSpec((1, tm, tn), lambda b, i, j: (b, i, j)),
        out_shape=jax.ShapeDtypeStruct((bsz, seq, n), BF16),
        scratch_shapes=[pltpu.VMEM((tm, d), BF16)],
        compiler_params=_cparams(("parallel", "parallel", "arbitrary")),
        name="proj",
    )(x, sc, sh, w, colscale)


def _win_kernel(sink_ref, slope_ref, q_ref, kp_ref, km_ref, kn_ref, vp_ref, vm_ref, vn_ref, o_ref, *, tq, seq):
    i = pl.program_id(1)
    kvh = pl.program_id(2)
    k = jnp.concatenate([kp_ref[0], km_ref[0], kn_ref[0]], axis=0)
    v = jnp.concatenate([vp_ref[0], vm_ref[0], vn_ref[0]], axis=0)
    w = tq + 2 * WINDOW
    qrel = lax.broadcasted_iota(jnp.int32, (tq, w), 0)
    krel = lax.broadcasted_iota(jnp.int32, (tq, w), 1) - WINDOW
    dist = jnp.abs(qrel - krel)
    kabs = krel + i * tq
    valid = (dist <= WINDOW) & (kabs >= 0) & (kabs < seq)
    distf = dist.astype(F32)
    for g in range(GQA_GROUP):
        h = kvh * GQA_GROUP + g
        qg = q_ref[0, :, g * HEAD_DIM:(g + 1) * HEAD_DIM]
        s = lax.dot_general(qg, k, (((1,), (1,)), ((), ())), preferred_element_type=F32)
        s = jnp.where(valid, s - slope_ref[h] * distf, NEG_INF)
        snk = sink_ref[h]
        m = jnp.maximum(jnp.max(s, axis=-1, keepdims=True), snk)
        p = jnp.exp(s - m)
        l = jnp.sum(p, axis=-1, keepdims=True) + jnp.exp(snk - m)
        o = jnp.dot(p.astype(BF16), v, preferred_element_type=F32) / l
        o_ref[0, :, g * HEAD_DIM:(g + 1) * HEAD_DIM] = o.astype(o_ref.dtype)


def _win_attention(proj, sink, slopes):
    bsz, seq, _ = proj.shape
    tq = 256
    r = tq // WINDOW
    nb = seq // WINDOW
    kcol = WIN_Q_HEADS
    vcol = WIN_Q_HEADS + WIN_KV_HEADS

    def prev(col):
        return pl.BlockSpec((1, WINDOW, HEAD_DIM), lambda b, i, k: (b, jnp.maximum(i * r - 1, 0), col + k))

    def main(col):
        return pl.BlockSpec((1, tq, HEAD_DIM), lambda b, i, k: (b, i, col + k))

    def nxt(col):
        return pl.BlockSpec((1, WINDOW, HEAD_DIM), lambda b, i, k: (b, jnp.minimum((i + 1) * r, nb - 1), col + k))

    smem = pl.BlockSpec(memory_space=pltpu.SMEM)
    return pl.pallas_call(
        functools.partial(_win_kernel, tq=tq, seq=seq),
        grid=(bsz, seq // tq, WIN_KV_HEADS),
        in_specs=[smem, smem,
                  pl.BlockSpec((1, tq, GQA_GROUP * HEAD_DIM), lambda b, i, k: (b, i, k)),
                  prev(kcol), main(kcol), nxt(kcol), prev(vcol), main(vcol), nxt(vcol)],
        out_specs=pl.BlockSpec((1, tq, GQA_GROUP * HEAD_DIM), lambda b, i, k: (b, i, k)),
        out_shape=jax.ShapeDtypeStruct((bsz, seq, WIN_Q_HEADS * HEAD_DIM), BF16),
        compiler_params=_cparams(("parallel", "parallel", "parallel")),
        name="win_attn",
    )(sink, slopes, proj, proj, proj, proj, proj, proj, proj)


def _online_update_t(s, vt, m_ref, l_ref, acc_ref, idx):
    m_old = m_ref[idx]
    m_new = jnp.maximum(m_old, jnp.max(s, axis=0, keepdims=True))
    alpha = jnp.exp2(m_old - m_new)
    p = jnp.exp2(s - m_new)
    l_ref[idx] = alpha * l_ref[idx] + jnp.sum(p, axis=0, keepdims=True)
    acc_ref[idx] = alpha * acc_ref[idx] + jnp.dot(vt, p.astype(BF16), preferred_element_type=F32)
    m_ref[idx] = m_new


def _split3(x):
    p1 = x.astype(BF16)
    r1 = x - p1.astype(F32)
    p2 = r1.astype(BF16)
    return p1, p2, (r1 - p2.astype(F32)).astype(BF16)


def _online_update_shifted(t, shift, vt, m_ref, l_ref, acc_ref, idx):
    m_old = m_ref[idx]
    m_new = jnp.maximum(m_old, jnp.max(t, axis=0, keepdims=True) + shift)
    alpha = jnp.exp2(m_old - m_new)
    p = jnp.exp2(t - (m_new - shift))
    l_ref[idx] = alpha * l_ref[idx] + jnp.sum(p, axis=0, keepdims=True)
    acc_ref[idx] = alpha * acc_ref[idx] + jnp.dot(vt, p.astype(BF16), preferred_element_type=F32)
    m_ref[idx] = m_new


def _diff_kernel(slope_ref, q_ref, k1_ref, k2_ref, vt_ref, lam_ref, g_ref, o_ref, bm_ref, m_ref, l_ref, acc_ref,
                 *, tq, tk, cq, nk, lam_init):
    h = pl.program_id(1)
    i = pl.program_id(2)
    j = pl.program_id(3)
    sigma = slope_ref[h] * LOG2E

    @pl.when(j == 0)
    def _():
        m_ref[...] = jnp.full_like(m_ref, NEG_INF)
        l_ref[...] = jnp.zeros_like(l_ref)
        acc_ref[...] = jnp.zeros_like(acc_ref)
        dbase = (lax.broadcasted_iota(jnp.int32, (tk, cq), 1)
                 - lax.broadcasted_iota(jnp.int32, (tk, cq), 0)).astype(F32)
        bm_ref[...] = dbase * sigma

    vt = vt_ref[0, 0]

    def sweep(frame):
        for c in range(tq // cq):
            cols = slice(c * cq, (c + 1) * cq)
            soff = sigma * (i * tq + c * cq - j * tk).astype(F32)
            for mi, k_ref in enumerate((k1_ref, k2_ref)):
                s = jnp.dot(k_ref[0], q_ref[0, 0, mi, :, cols], preferred_element_type=F32)
                t, shift = frame(s, soff)
                _online_update_shifted(t, shift, vt, m_ref, l_ref, acc_ref, (mi, slice(None), cols))

    @pl.when(i * tq >= (j + 1) * tk)
    def _():
        sweep(lambda s, soff: (s - bm_ref[...], -soff))

    @pl.when((i + 1) * tq <= j * tk)
    def _():
        sweep(lambda s, soff: (s + bm_ref[...], soff))

    @pl.when((i * tq < (j + 1) * tk) & ((i + 1) * tq > j * tk))
    def _():
        sweep(lambda s, soff: (s - jnp.abs(bm_ref[...] + soff), 0.0))

    @pl.when(j == nk - 1)
    def _():
        lv = lam_ref[...]
        lam = (jnp.exp(jnp.sum(lv[0:1] * lv[1:2], axis=-1, keepdims=True))
               - jnp.exp(jnp.sum(lv[2:3] * lv[3:4], axis=-1, keepdims=True)) + lam_init)
        ot = acc_ref[0] / l_ref[0] - lam * (acc_ref[1] / l_ref[1])
        o = ot.T
        r = lax.rsqrt(jnp.mean(o * o, axis=-1, keepdims=True) + RMS_EPS)
        o_ref[0] = (o * r * g_ref[...] * (1.0 - lam_init)).astype(o_ref.dtype)


def _diff_attention(proj, slopes, lamvec, subln_g, lam_init):
    bsz, seq, _ = proj.shape
    tq = _pick(seq, (1024, 512))
    tk = _pick(seq, (1024, 512))
    cq = 256
    nk = seq // tk
    kcol = 2560 // HEAD_DIM
    qt = proj[:, :, 1536:2560].reshape(bsz, seq, DIFF_HEADS, 2, HEAD_DIM).transpose(0, 2, 3, 4, 1)
    vt = proj[:, :, 3584:4608].reshape(bsz, seq, DIFF_HEADS, DIFF_V_DIM).transpose(0, 2, 3, 1)
    smem = pl.BlockSpec(memory_space=pltpu.SMEM)

    def kspec(off):
        return pl.BlockSpec((1, tk, HEAD_DIM), lambda b, h, i, j: (b, j, kcol + 2 * h + off))

    return pl.pallas_call(
        functools.partial(_diff_kernel, tq=tq, tk=tk, cq=cq, nk=nk, lam_init=lam_init),
        grid=(bsz, DIFF_HEADS, seq // tq, nk),
        in_specs=[smem,
                  pl.BlockSpec((1, 1, 2, HEAD_DIM, tq), lambda b, h, i, j: (b, h, 0, 0, i)),
                  kspec(0), kspec(1),
                  pl.BlockSpec((1, 1, DIFF_V_DIM, tk), lambda b, h, i, j: (b, h, 0, j)),
                  pl.BlockSpec((4, HEAD_DIM), lambda b, h, i, j: (0, 0)),
                  pl.BlockSpec((1, DIFF_V_DIM), lambda b, h, i, j: (0, 0))],
        out_specs=pl.BlockSpec((1, tq, DIFF_V_DIM), lambda b, h, i, j: (b, i, h)),
        out_shape=jax.ShapeDtypeStruct((bsz, seq, DIFF_HEADS * DIFF_V_DIM), BF16),
        scratch_shapes=[pltpu.VMEM((tk, cq), F32),
                        pltpu.VMEM((2, 1, tq), F32), pltpu.VMEM((2, 1, tq), F32),
                        pltpu.VMEM((2, DIFF_V_DIM, tq), F32)],
        compiler_params=_cparams(("parallel", "parallel", "parallel", "arbitrary")),
        name="diff_attn",
    )(slopes, qt, proj, proj, vt, lamvec, subln_g)


def _normrope_kernel(x_ref, cos_ref, sin_ref, gq_ref, gk_ref, o_ref, *, tm):
    j = pl.program_id(2)
    is_q = j < AX_Q_HEADS // 2
    g = jnp.where(is_q, gq_ref[...], gk_ref[...])
    post = jnp.where(is_q, ATTN_SCALE * LOG2E, 1.0).astype(F32)
    lane = lax.broadcasted_iota(jnp.int32, (tm, HEAD_DIM), 1)
    first = (lane % 64) < 32
    cos = cos_ref[...]
    sin = sin_ref[...]
    for hh in range(2):
        x = x_ref[0, :, hh * HEAD_DIM:(hh + 1) * HEAD_DIM].astype(F32)
        xn = x * lax.rsqrt(jnp.mean(x * x, axis=-1, keepdims=True) + RMS_EPS) * g
        partner = jnp.where(first, pltpu.roll(xn, 96, 1), pltpu.roll(xn, 32, 1))
        y = (xn * cos + partner * sin) * post
        o_ref[0, :, hh * HEAD_DIM:(hh + 1) * HEAD_DIM] = y.astype(o_ref.dtype)


def _rope_tables(seq):
    half = HEAD_DIM // 2
    inv = jnp.asarray(1.0 / (ROPE_THETA ** (np.arange(0, half, 2) / half)), dtype=F32)
    t = jnp.arange(seq)
    ang_r = (t // GRID_W).astype(F32)[:, None] * inv[None, :]
    ang_c = (t % GRID_W).astype(F32)[:, None] * inv[None, :]
    cr, sr, cc, sn = jnp.cos(ang_r), jnp.sin(ang_r), jnp.cos(ang_c), jnp.sin(ang_c)
    return (jnp.concatenate([cr, cr, cc, cc], axis=-1), jnp.concatenate([-sr, sr, -sn, sn], axis=-1))


def _normrope(proj, qnorm_g, knorm_g):
    bsz, seq, _ = proj.shape
    tm = _pick(seq, (512,))
    cos, sin = _rope_tables(seq)
    nw = (AX_Q_HEADS + AX_KV_HEADS) * HEAD_DIM
    return pl.pallas_call(
        functools.partial(_normrope_kernel, tm=tm),
        grid=(bsz, seq // tm, nw // 256),
        in_specs=[pl.BlockSpec((1, tm, 256), lambda b, i, j: (b, i, j)),
                  pl.BlockSpec((tm, HEAD_DIM), lambda b, i, j: (i, 0)),
                  pl.BlockSpec((tm, HEAD_DIM), lambda b, i, j: (i, 0)),
                  pl.BlockSpec((1, HEAD_DIM), lambda b, i, j: (0, 0)),
                  pl.BlockSpec((1, HEAD_DIM), lambda b, i, j: (0, 0))],
        out_specs=pl.BlockSpec((1, tm, 256), lambda b, i, j: (b, i, j)),
        out_shape=jax.ShapeDtypeStruct((bsz, seq, nw), BF16),
        compiler_params=_cparams(("parallel", "parallel", "parallel")),
        name="normrope",
    )(proj, cos, sin, qnorm_g.reshape(1, HEAD_DIM), knorm_g.reshape(1, HEAD_DIM))


def _gqa_kernel(q_ref, k_ref, vt_ref, o_ref, m_ref, l_ref, acc_ref, *, tq, cq, nk):
    j = pl.program_id(3)

    @pl.when(j == 0)
    def _():
        m_ref[...] = jnp.full_like(m_ref, NEG_INF)
        l_ref[...] = jnp.zeros_like(l_ref)
        acc_ref[...] = jnp.zeros_like(acc_ref)

    k = k_ref[0]
    vt = vt_ref[0, 0]
    per = tq // cq
    for c in range(GQA_GROUP * per):
        g, sub = divmod(c, per)
        s = jnp.dot(k, q_ref[0, g, :, sub * cq:(sub + 1) * cq], preferred_element_type=F32)
        _online_update_t(s, vt, m_ref, l_ref, acc_ref, (slice(None), slice(c * cq, (c + 1) * cq)))

    @pl.when(j == nk - 1)
    def _():
        ot = acc_ref[...] / l_ref[...]
        for g in range(GQA_GROUP):
            o_ref[0, :, g * HEAD_DIM:(g + 1) * HEAD_DIM] = ot[:, g * tq:(g + 1) * tq].T.astype(o_ref.dtype)


def _gqa_attention(qk, proj):
    bsz, seq, _ = qk.shape
    tq = _pick(seq, (512,))
    tk = _pick(seq, (1024, 512))
    cq = 128
    nk = seq // tk
    gw = GQA_GROUP * HEAD_DIM
    qt = qk[:, :, :AX_Q_HEADS * HEAD_DIM].reshape(bsz, seq, AX_Q_HEADS, HEAD_DIM).transpose(0, 2, 3, 1)
    vt = proj[:, :, 1280:1536].reshape(bsz, seq, AX_KV_HEADS, HEAD_DIM).transpose(0, 2, 3, 1)
    return pl.pallas_call(
        functools.partial(_gqa_kernel, tq=tq, cq=cq, nk=nk),
        grid=(bsz, AX_KV_HEADS, seq // tq, nk),
        in_specs=[pl.BlockSpec((1, GQA_GROUP, HEAD_DIM, tq), lambda b, h, i, j: (b, h, 0, i)),
                  pl.BlockSpec((1, tk, HEAD_DIM), lambda b, h, i, j: (b, j, AX_Q_HEADS + h)),
                  pl.BlockSpec((1, 1, HEAD_DIM, tk), lambda b, h, i, j: (b, h, 0, j))],
        out_specs=pl.BlockSpec((1, tq, gw), lambda b, h, i, j: (b, i, h)),
        out_shape=jax.ShapeDtypeStruct((bsz, seq, AX_Q_HEADS * HEAD_DIM), BF16),
        scratch_shapes=[pltpu.VMEM((1, GQA_GROUP * tq), F32), pltpu.VMEM((1, GQA_GROUP * tq), F32),
                        pltpu.VMEM((HEAD_DIM, GQA_GROUP * tq), F32)],
        compiler_params=_cparams(("parallel", "parallel", "parallel", "arbitrary")),
        name="gqa_attn",
    )(qt, qk, vt)


def _na_bias_table(rpb):
    qi = np.arange(NA_ROWS)[:, None, None, None]
    qc = np.arange(GRID_W)[None, :, None, None]
    kj = np.arange(2 * NA_ROWS)[None, None, :, None]
    kc = np.arange(GRID_W)[None, None, None, :]
    c_start = np.clip(qc - NA_COLS // 2, 0, GRID_W - NA_COLS)
    col_ok = (kc >= c_start) & (kc < c_start + NA_COLS)
    rel_c = np.clip(kc - qc + NA_COLS - 1, 0, 2 * NA_COLS - 2)
    onehot_c = jnp.asarray(rel_c[0, :, 0, :, None] == np.arange(2 * NA_COLS - 1), dtype=F32)
    tabs = []
    for off, clamp in ((0, np.maximum), (-NA_ROWS // 2, None), (-NA_ROWS, np.minimum)):
        rs = qi - NA_ROWS // 2
        if clamp is not None:
            rs = clamp(rs, 0)
        kr = off + kj
        ok = (kr >= rs) & (kr < rs + NA_ROWS) & col_ok
        rel_r = np.clip(kr - qi + NA_ROWS - 1, 0, 2 * NA_ROWS - 2)[:, 0, :, 0]
        onehot_r = jnp.asarray(rel_r[:, :, None] == np.arange(2 * NA_ROWS - 1), dtype=F32)
        bias = jnp.einsum("ija,hab,cdb->hicjd", onehot_r, rpb.astype(F32), onehot_c, precision=HIGHEST)
        tabs.append(jnp.where(ok[None], bias, NEG_INF).reshape(NA_HEADS, NA_TQ, NA_WIN))
    return jnp.stack(tabs)


def _na_kernel(q_ref, k_ref, v_ref, bias_ref, o_ref, *, rows):
    i = pl.program_id(2)
    ws = jnp.clip(i * NA_ROWS - NA_ROWS // 2, 0, rows - 2 * NA_ROWS)
    start = pl.multiple_of(ws * GRID_W, GRID_W)
    k = k_ref[0, pl.ds(start, NA_WIN), :]
    v = v_ref[0, pl.ds(start, NA_WIN), :]
    s = lax.dot_general(q_ref[0], k, (((1,), (1,)), ((), ())), preferred_element_type=F32) + bias_ref[0, 0]
    m = jnp.max(s, axis=-1, keepdims=True)
    p = jnp.exp(s - m)
    l = jnp.sum(p, axis=-1, keepdims=True)
    o_ref[0] = (jnp.dot(p.astype(BF16), v, preferred_element_type=F32) / l).astype(o_ref.dtype)


def _na_attention(proj, bias_tab):
    bsz, seq, _ = proj.shape
    rows = seq // GRID_W
    nq = seq // NA_TQ
    assert rows >= 3 * NA_ROWS and seq % NA_TQ == 0
    qcol, kcol, vcol = 1536 // HEAD_DIM, 2560 // HEAD_DIM, 3584 // HEAD_DIM

    def case(i):
        return jnp.where(i == 0, 0, jnp.where(i == nq - 1, 2, 1))

    return pl.pallas_call(
        functools.partial(_na_kernel, rows=rows),
        grid=(bsz, NA_HEADS, nq),
        in_specs=[pl.BlockSpec((1, NA_TQ, HEAD_DIM), lambda b, h, i: (b, i, qcol + h)),
                  pl.BlockSpec((1, seq, HEAD_DIM), lambda b, h, i: (b, 0, kcol + h)),
                  pl.BlockSpec((1, seq, HEAD_DIM), lambda b, h, i: (b, 0, vcol + h)),
                  pl.BlockSpec((1, 1, NA_TQ, NA_WIN), lambda b, h, i: (case(i), h, 0, 0))],
        out_specs=pl.BlockSpec((1, NA_TQ, HEAD_DIM), lambda b, h, i: (b, i, h)),
        out_shape=jax.ShapeDtypeStruct((bsz, seq, NA_HEADS * HEAD_DIM), BF16),
        compiler_params=_cparams(("parallel", "parallel", "arbitrary")),
        name="na_attn",
    )(proj, proj, proj, bias_tab)


def _outproj_kernel(oa_ref, ob_ref, wa_ref, wb_ref, x_ref, g_ref, lng_ref, lnb_ref, o_ref):
    m = (jnp.dot(oa_ref[0], wa_ref[...], preferred_element_type=F32)
         + jnp.dot(ob_ref[0], wb_ref[...], preferred_element_type=F32))
    y = DEEPNORM_ALPHA * x_ref[0] + g_ref[0] * m
    o_ref[0] = _layer_norm(y, lng_ref[...], lnb_ref[...])


def _outproj_ln(oa, ob, w_out, x, gate, ln_g, ln_b):
    bsz, seq, d = x.shape
    half = oa.shape[-1]
    tm = _pick(seq, (512, 256))
    vec = pl.BlockSpec((1, d), lambda b, i: (0, 0))
    return pl.pallas_call(
        _outproj_kernel,
        grid=(bsz, seq // tm),
        in_specs=[pl.BlockSpec((1, tm, half), lambda b, i: (b, i, 0)),
                  pl.BlockSpec((1, tm, half), lambda b, i: (b, i, 0)),
                  pl.BlockSpec((half, d), lambda b, i: (0, 0)),
                  pl.BlockSpec((half, d), lambda b, i: (1, 0)),
                  pl.BlockSpec((1, tm, d), lambda b, i: (b, i, 0)),
                  pl.BlockSpec((1, 1, d), lambda b, i: (b, 0, 0)),
                  vec, vec],
        out_specs=pl.BlockSpec((1, tm, d), lambda b, i: (b, i, 0)),
        out_shape=jax.ShapeDtypeStruct((bsz, seq, d), F32),
        compiler_params=_cparams(("parallel", "parallel")),
        name="outproj_ln",
    )(oa, ob, w_out, w_out, x, gate, ln_g.reshape(1, d), ln_b.reshape(1, d))


def _route(logits):
    lane = lax.broadcasted_iota(jnp.int32, logits.shape, 1)
    big = jnp.int32(2 ** 30)
    is_g = lane < N_GROUPS
    mg = jnp.max(jnp.where(is_g, logits, -jnp.inf), axis=-1, keepdims=True)
    g_idx = jnp.min(jnp.where(is_g & (logits == mg), lane, big), axis=-1, keepdims=True)
    p_g = 1.0 / jnp.sum(jnp.where(is_g, jnp.exp(logits - mg), 0.0), axis=-1, keepdims=True)
    lo = N_GROUPS + EXPERTS_PER_GROUP * g_idx
    in_grp = (lane >= lo) & (lane < lo + EXPERTS_PER_GROUP)
    v1 = jnp.max(jnp.where(in_grp, logits, -jnp.inf), axis=-1, keepdims=True)
    i1 = jnp.min(jnp.where(in_grp & (logits == v1), lane, big), axis=-1, keepdims=True)
    rest = in_grp & (lane != i1)
    v2 = jnp.max(jnp.where(rest, logits, -jnp.inf), axis=-1, keepdims=True)
    i2 = jnp.min(jnp.where(rest & (logits == v2), lane, big), axis=-1, keepdims=True)
    e2 = jnp.exp(v2 - v1)
    w1 = p_g / (1.0 + e2)
    w2 = p_g * e2 / (1.0 + e2)
    gates = jnp.where(lane == i1, w1, 0.0) + jnp.where(lane == i2, w2, 0.0)
    return jnp.where(lane == 0, g_idx.astype(F32), gates)


def _moe_route_kernel(x_ref, sc_ref, sh_ref, wr_ref, br_ref, h_ref, gate_ref):
    h = x_ref[0] * (1.0 + sc_ref[0]) + sh_ref[0]
    h_ref[0] = h.astype(BF16)
    logits = jnp.dot(h, wr_ref[...], precision=HIGHEST, preferred_element_type=F32) + br_ref[...]
    gate_ref[0] = _route(logits)


def _moe_route(x, sc, sh, w_router, b_router):
    bsz, seq, d = x.shape
    tm = _pick(seq, (512, 256))
    mod = pl.BlockSpec((1, 1, d), lambda b, i: (b, 0, 0))
    return pl.pallas_call(
        _moe_route_kernel,
        grid=(bsz, seq // tm),
        in_specs=[pl.BlockSpec((1, tm, d), lambda b, i: (b, i, 0)), mod, mod,
                  pl.BlockSpec((d, ROUTER_LANES), lambda b, i: (0, 0)),
                  pl.BlockSpec((1, ROUTER_LANES), lambda b, i: (0, 0))],
        out_specs=[pl.BlockSpec((1, tm, d), lambda b, i: (b, i, 0)),
                   pl.BlockSpec((1, tm, ROUTER_LANES), lambda b, i: (b, i, 0))],
        out_shape=[jax.ShapeDtypeStruct((bsz, seq, d), BF16),
                   jax.ShapeDtypeStruct((bsz, seq, ROUTER_LANES), F32)],
        compiler_params=_cparams(("parallel", "parallel")),
        name="moe_route",
    )(x, sc, sh, w_router, b_router)


def _moe_expert_kernel(h_ref, gate_ref, wg_ref, wu_ref, wd_ref, o_ref, pmt_ref, hs_ref, gs_ref, ys_ref, bounds_ref,
                       *, tm, chunk):
    e = pl.program_id(2)

    @pl.when(e == 0)
    def _():
        rec = gate_ref[0]
        lane = lax.broadcasted_iota(jnp.int32, (tm, ROUTER_LANES), 1)
        onehot = jnp.where((lane < N_GROUPS) & (lane == rec[:, 0:1].astype(jnp.int32)), 1.0, 0.0)
        below = (lax.broadcasted_iota(jnp.int32, (tm, tm), 1)
                 < lax.broadcasted_iota(jnp.int32, (tm, tm), 0)).astype(BF16)
        rank = jnp.dot(below, onehot.astype(BF16), preferred_element_type=F32)
        cnt = jnp.sum(onehot, axis=0, keepdims=True)
        padded = jnp.floor((cnt + (SORT_ALIGN - 1)) * (1.0 / SORT_ALIGN)) * SORT_ALIGN
        lane1 = lax.broadcasted_iota(jnp.int32, (1, ROUTER_LANES), 1)
        starts = jnp.zeros((1, ROUTER_LANES), F32)
        for g in range(N_GROUPS):
            first = jnp.sum(jnp.where(lane1 < g, padded, 0.0))
            size = jnp.sum(jnp.where(lane1 == g, cnt, 0.0))
            starts = starts + jnp.where(lane1 == g, first, 0.0)
            bounds_ref[2 * g] = first.astype(jnp.int32)
            bounds_ref[2 * g + 1] = (size.astype(jnp.int32) + chunk - 1) // chunk
        pos = jnp.sum(onehot * (rank + starts), axis=1, keepdims=True).astype(jnp.int32)
        pmt = (lax.broadcasted_iota(jnp.int32, pmt_ref.shape, 1) == pos).astype(BF16)
        pmt_ref[...] = pmt
        dn = (((0,), (0,)), ((), ()))
        hs_ref[...] = lax.dot_general(pmt, h_ref[0], dn, preferred_element_type=F32).astype(BF16)
        gs = jnp.zeros(gs_ref.shape, F32)
        for piece in _split3(rec):
            gs = gs + lax.dot_general(pmt, piece, dn, preferred_element_type=F32)
        gs_ref[...] = gs
        ys_ref[...] = jnp.zeros_like(ys_ref)

    grp = e // EXPERTS_PER_GROUP
    lane_c = lax.broadcasted_iota(jnp.int32, (chunk, ROUTER_LANES), 1)
    first_row = bounds_ref[2 * grp]

    def body(c, carry):
        rows = pl.ds(pl.multiple_of(first_row + c * chunk, SORT_ALIGN), chunk)
        hb = hs_ref[rows, :]
        a = jnp.dot(hb, wg_ref[0], preferred_element_type=F32)
        u = jnp.dot(hb, wu_ref[0], preferred_element_type=F32)
        ge = jnp.sum(jnp.where(lane_c == e + N_GROUPS, gs_ref[rows, :], 0.0), axis=-1, keepdims=True)
        act = (_silu(a) * u * ge).astype(BF16)
        ys_ref[rows, :] += jnp.dot(act, wd_ref[0], preferred_element_type=F32)
        return carry

    lax.fori_loop(0, bounds_ref[2 * grp + 1], body, 0)

    @pl.when(e == N_EXPERTS - 1)
    def _():
        o_ref[0] = jnp.dot(pmt_ref[...], ys_ref[...].astype(BF16), preferred_element_type=F32).astype(o_ref.dtype)


def _moe_experts(h, rec, w_gate, w_up, w_down):
    bsz, seq, d = h.shape
    tm = _pick(seq, (1024, 512))
    chunk = 128
    ps = tm + 2 * chunk
    assert N_GROUPS * (SORT_ALIGN - 1) + chunk <= 2 * chunk
    return pl.pallas_call(
        functools.partial(_moe_expert_kernel, tm=tm, chunk=chunk),
        grid=(bsz, seq // tm, N_EXPERTS),
        in_specs=[pl.BlockSpec((1, tm, d), lambda b, i, e: (b, i, 0)),
                  pl.BlockSpec((1, tm, ROUTER_LANES), lambda b, i, e: (b, i, 0)),
                  pl.BlockSpec((1, d, D_EXPERT), lambda b, i, e: (e, 0, 0)),
                  pl.BlockSpec((1, d, D_EXPERT), lambda b, i, e: (e, 0, 0)),
                  pl.BlockSpec((1, D_EXPERT, d), lambda b, i, e: (e, 0, 0))],
        out_specs=pl.BlockSpec((1, tm, d), lambda b, i, e: (b, i, 0)),
        out_shape=jax.ShapeDtypeStruct((bsz, seq, d), BF16),
        scratch_shapes=[pltpu.VMEM((tm, ps), BF16), pltpu.VMEM((ps, d), BF16), pltpu.VMEM((ps, ROUTER_LANES), F32),
                        pltpu.VMEM((ps, d), F32), pltpu.SMEM((2 * N_GROUPS,), jnp.int32)],
        compiler_params=_cparams(("parallel", "parallel", "arbitrary")),
        name="moe_experts",
    )(h, rec, w_gate, w_up, w_down)


def _resid_ln_kernel(x_ref, f_ref, g_ref, lng_ref, lnb_ref, o_ref):
    y = DEEPNORM_ALPHA * x_ref[0] + g_ref[0] * f_ref[0].astype(F32)
    o_ref[0] = _layer_norm(y, lng_ref[...], lnb_ref[...])


def _resid_ln(x, f, gate, ln_g, ln_b):
    bsz, seq, d = x.shape
    tm = _pick(seq, (512, 256))
    vec = pl.BlockSpec((1, d), lambda b, i: (0, 0))
    tile = pl.BlockSpec((1, tm, d), lambda b, i: (b, i, 0))
    return pl.pallas_call(
        _resid_ln_kernel,
        grid=(bsz, seq // tm),
        in_specs=[tile, tile, pl.BlockSpec((1, 1, d), lambda b, i: (b, 0, 0)), vec, vec],
        out_specs=tile,
        out_shape=jax.ShapeDtypeStruct((bsz, seq, d), F32),
        compiler_params=_cparams(("parallel", "parallel")),
        name="resid_ln",
    )(x, f, gate, ln_g.reshape(1, d), ln_b.reshape(1, d))


def _moe_ln(x, sc, sh, gate, w_router, b_router, w_gate, w_up, w_down, ln_g, ln_b):
    h, rec = _moe_route(x, sc, sh, w_router, b_router)
    f = _moe_experts(h, rec, w_gate, w_up, w_down)
    return _resid_ln(x, f, gate, ln_g, ln_b)


def _alibi_slopes(n):
    return jnp.asarray(2.0 ** (-8.0 * np.arange(1, n + 1) / n), dtype=F32)


def _colscale(q_ranges):
    cs = np.ones((1, PROJ_N), np.float32)
    for lo, hi, scale in q_ranges:
        cs[:, lo:hi] = scale
    return jnp.asarray(cs)


def _prep_params(p):
    q = dict(p)
    for name in ("w_in_ab", "w_out_ab", "w_in_cd", "w_out_cd", "w_gate", "w_up", "w_down"):
        q[name] = p[name].astype(BF16)
    pad = ROUTER_LANES - N_GROUPS - N_EXPERTS
    q["w_router"] = jnp.pad(jnp.concatenate([p["w_rg"], p["w_re"]], axis=-1), ((0, 0), (0, 0), (0, pad)))
    q["b_router"] = jnp.pad(jnp.concatenate([p["b_rg"], p["b_re"]], axis=-1), ((0, 0), (0, pad)))[:, None, :]
    q["na_bias"] = [_na_bias_table(p["rpb_d"][i]) for i in range(p["rpb_d"].shape[0])]
    return q


def _trunk(x, ada, p):
    for l in range(DEPTH):
        sh1, sc1, g1, sh2, sc2, g2 = jnp.split(ada[l][:, None, :], 6, axis=-1)
        i = l // 2
        if l % 2 == 0:
            lam_init = 0.8 - 0.6 * math.exp(-0.3 * l)
            proj = _proj(x, sc1, sh1, p["w_in_ab"][i], _colscale(((0, 1024, ATTN_SCALE), (1536, 2560, ATTN_SCALE * LOG2E))))
            oa = _win_attention(proj, p["sink_a"][i], _alibi_slopes(WIN_Q_HEADS))
            lamvec = jnp.stack([p["lam_q1"][i], p["lam_k1"][i], p["lam_q2"][i], p["lam_k2"][i]])
            ob = _diff_attention(proj, _alibi_slopes(DIFF_HEADS), lamvec, p["subln_g"][i][None, :], lam_init)
            w_out = p["w_out_ab"][i]
        else:
            proj = _proj(x, sc1, sh1, p["w_in_cd"][i], _colscale(((1536, 2560, ATTN_SCALE),)))
            qk = _normrope(proj, p["qnorm_c"][i], p["knorm_c"][i])
            oa = _gqa_attention(qk, proj)
            ob = _na_attention(proj, p["na_bias"][i])
            w_out = p["w_out_cd"][i]
        x = _outproj_ln(oa, ob, w_out, x, g1, p["ln_g"][l, 0], p["ln_b"][l, 0])
        x = _moe_ln(x, sc2, sh2, g2, p["w_router"][l], p["b_router"][l],
                    p["w_gate"][l], p["w_up"][l], p["w_down"][l], p["ln_g"][l, 1], p["ln_b"][l, 1])
    return x


def kernel(x_prompt, x_sample, c_prompt, c_sample, w_ada, b_ada, ln_g, ln_b, w_in_ab, w_out_ab, sink_a,
           lam_q1, lam_k1, lam_q2, lam_k2, subln_g, w_in_cd, w_out_cd, qnorm_c, knorm_c, rpb_d,
           w_rg, b_rg, w_re, b_re, w_gate, w_up, w_down):
    p = _prep_params(dict(
        ln_g=ln_g, ln_b=ln_b, w_in_ab=w_in_ab, w_out_ab=w_out_ab, sink_a=sink_a,
        lam_q1=lam_q1, lam_k1=lam_k1, lam_q2=lam_q2, lam_k2=lam_k2, subln_g=subln_g,
        w_in_cd=w_in_cd, w_out_cd=w_out_cd, qnorm_c=qnorm_c, knorm_c=knorm_c, rpb_d=rpb_d,
        w_rg=w_rg, b_rg=b_rg, w_re=w_re, b_re=b_re, w_gate=w_gate, w_up=w_up, w_down=w_down))
    nb_p, nb_s = c_prompt.shape[0], c_sample.shape[0]
    c_all = jnp.concatenate([c_prompt, c_sample, jnp.zeros((8 - nb_p - nb_s, D_MODEL), F32)], axis=0)
    ada = _ada(c_all, w_ada, b_ada)
    y_prompt = _trunk(x_prompt, ada[:, :nb_p], p)
    y_sample = _trunk(x_sample, ada[:, nb_p:nb_p + nb_s], p)
    return (y_prompt, y_sample)
```

```python
import functools
import math

import numpy as np
import jax
import jax.numpy as jnp
from jax import lax
from jax.experimental import pallas as pl
from jax.experimental.pallas import tpu as pltpu

F32 = jnp.float32
BF16 = jnp.bfloat16
HIGHEST = lax.Precision.HIGHEST

D_MODEL = 2048
DEPTH = 2
HEAD_DIM = 128
WINDOW = 128
GRID_W = 64
WIN_Q_HEADS = 8
WIN_KV_HEADS = 2
GQA_GROUP = 4
DIFF_HEADS = 4
DIFF_V_DIM = 256
AX_Q_HEADS = 8
AX_KV_HEADS = 2
NA_HEADS = 8
NA_ROWS = 8
NA_COLS = 16
ROPE_THETA = 10000.0
N_GROUPS = 4
EXPERTS_PER_GROUP = 4
N_EXPERTS = 16
D_EXPERT = 512
LN_EPS = 1e-5
RMS_EPS = 1e-6
NEG_INF = -1e30
DEEPNORM_ALPHA = (2 * DEPTH) ** 0.25
PROJ_N = 4608
ATTN_SCALE = HEAD_DIM ** -0.5
LOG2E = math.log2(math.e)
LANES = 128
ROUTER_LANES = LANES
SORT_ALIGN = 16
VMEM_LIMIT = 56 * 1024 * 1024

GQA_KEY_SUB = 256
NA_TQ = NA_ROWS * GRID_W
NA_WIN = 2 * NA_ROWS * GRID_W


def _cparams(sem):
    return pltpu.CompilerParams(dimension_semantics=sem, vmem_limit_bytes=VMEM_LIMIT)


def _pick(n, cands):
    for c in cands:
        if n % c == 0:
            return c
    raise ValueError(f"no tile in {cands} divides {n}")


def _layer_norm(y, g, b):
    mu = jnp.mean(y, axis=-1, keepdims=True)
    yc = y - mu
    var = jnp.mean(yc * yc, axis=-1, keepdims=True)
    return yc * lax.rsqrt(var + LN_EPS) * g + b


def _silu(a):
    return a / (1.0 + jnp.exp(-a))


def _ada_kernel(c_ref, w_ref, b_ref, o_ref):
    a = _silu(c_ref[...])
    o_ref[0] = jnp.dot(a, w_ref[0], precision=HIGHEST, preferred_element_type=F32) + b_ref[0]


def _ada(c_all, w_ada, b_ada):
    n = 6 * D_MODEL
    tn = 1024
    return pl.pallas_call(
        _ada_kernel,
        grid=(DEPTH, n // tn),
        in_specs=[pl.BlockSpec((8, D_MODEL), lambda l, j: (0, 0)),
                  pl.BlockSpec((1, D_MODEL, tn), lambda l, j: (l, 0, j)),
                  pl.BlockSpec((1, 1, tn), lambda l, j: (l, 0, j))],
        out_specs=pl.BlockSpec((1, 8, tn), lambda l, j: (l, 0, j)),
        out_shape=jax.ShapeDtypeStruct((DEPTH, 8, n), F32),
        compiler_params=_cparams(("parallel", "parallel")),
        name="ada",
    )(c_all, w_ada, b_ada.reshape(DEPTH, 1, n))


def _proj_kernel(x_ref, sc_ref, sh_ref, w_ref, cs_ref, o_ref, h_ref):
    @pl.when(pl.program_id(2) == 0)
    def _():
        h_ref[...] = (x_ref[0] * (1.0 + sc_ref[0]) + sh_ref[0]).astype(BF16)

    acc = jnp.dot(h_ref[...], w_ref[...], preferred_element_type=F32)
    o_ref[0] = (acc * cs_ref[...]).astype(o_ref.dtype)


def _proj(x, sc, sh, w, colscale):
    bsz, seq, d = x.shape
    n = w.shape[1]
    tm = _pick(seq, (512, 256))
    tn = 2304
    return pl.pallas_call(
        _proj_kernel,
        grid=(bsz, seq // tm, n // tn),
        in_specs=[pl.BlockSpec((1, tm, d), lambda b, i, j: (b, i, 0)),
                  pl.BlockSpec((1, 1, d), lambda b, i, j: (b, 0, 0)),
                  pl.BlockSpec((1, 1, d), lambda b, i, j: (b, 0, 0)),
                  pl.BlockSpec((d, tn), lambda b, i, j: (0, j)),
                  pl.BlockSpec((1, tn), lambda b, i, j: (0, j))],
        out_specs=pl.BlockSpec((1, tm, tn), lambda b, i, j: (b, i, j)),
        out_shape=jax.ShapeDtypeStruct((bsz, seq, n), BF16),
        scratch_shapes=[pltpu.VMEM((tm, d), BF16)],
        compiler_params=_cparams(("parallel", "parallel", "arbitrary")),
        name="proj",
    )(x, sc, sh, w, colscale)


def _win_kernel(sink_ref, slope_ref, q_ref, kp_ref, km_ref, kn_ref, vp_ref, vm_ref, vn_ref, o_ref, *, tq, seq):
    i = pl.program_id(1)
    kvh = pl.program_id(2)
    k = jnp.concatenate([kp_ref[0], km_ref[0], kn_ref[0]], axis=0)
    v = jnp.concatenate([vp_ref[0], vm_ref[0], vn_ref[0]], axis=0)
    w = tq + 2 * WINDOW
    qrel = lax.broadcasted_iota(jnp.int32, (tq, w), 0)
    krel = lax.broadcasted_iota(jnp.int32, (tq, w), 1) - WINDOW
    dist = jnp.abs(qrel - krel)
    kabs = krel + i * tq
    valid = (dist <= WINDOW) & (kabs >= 0) & (kabs < seq)
    distf = dist.astype(F32)
    for g in range(GQA_GROUP):
        h = kvh * GQA_GROUP + g
        qg = q_ref[0, :, g * HEAD_DIM:(g + 1) * HEAD_DIM]
        s = lax.dot_general(qg, k, (((1,), (1,)), ((), ())), preferred_element_type=F32)
        s = jnp.where(valid, s - slope_ref[h] * distf, NEG_INF)
        snk = sink_ref[h]
        m = jnp.maximum(jnp.max(s, axis=-1, keepdims=True), snk)
        p = jnp.exp(s - m)
        l = jnp.sum(p, axis=-1, keepdims=True) + jnp.exp(snk - m)
        o = jnp.dot(p.astype(BF16), v, preferred_element_type=F32) / l
        o_ref[0, :, g * HEAD_DIM:(g + 1) * HEAD_DIM] = o.astype(o_ref.dtype)


def _win_attention(proj, sink, slopes):
    bsz, seq, _ = proj.shape
    tq = 256
    r = tq // WINDOW
    nb = seq // WINDOW
    kcol = WIN_Q_HEADS
    vcol = WIN_Q_HEADS + WIN_KV_HEADS

    def prev(col):
        return pl.BlockSpec((1, WINDOW, HEAD_DIM), lambda b, i, k: (b, jnp.maximum(i * r - 1, 0), col + k))

    def main(col):
        return pl.BlockSpec((1, tq, HEAD_DIM), lambda b, i, k: (b, i, col + k))

    def nxt(col):
        return pl.BlockSpec((1, WINDOW, HEAD_DIM), lambda b, i, k: (b, jnp.minimum((i + 1) * r, nb - 1), col + k))

    smem = pl.BlockSpec(memory_space=pltpu.SMEM)
    return pl.pallas_call(
        functools.partial(_win_kernel, tq=tq, seq=seq),
        grid=(bsz, seq // tq, WIN_KV_HEADS),
        in_specs=[smem, smem,
                  pl.BlockSpec((1, tq, GQA_GROUP * HEAD_DIM), lambda b, i, k: (b, i, k)),
                  prev(kcol), main(kcol), nxt(kcol), prev(vcol), main(vcol), nxt(vcol)],
        out_specs=pl.BlockSpec((1, tq, GQA_GROUP * HEAD_DIM), lambda b, i, k: (b, i, k)),
        out_shape=jax.ShapeDtypeStruct((bsz, seq, WIN_Q_HEADS * HEAD_DIM), BF16),
        compiler_params=_cparams(("parallel", "parallel", "parallel")),
        name="win_attn",
    )(sink, slopes, proj, proj, proj, proj, proj, proj, proj)


def _online_update_t(s, vt, m_ref, l_ref, acc_ref, idx):
    m_old = m_ref[idx]
    m_new = jnp.maximum(m_old, jnp.max(s, axis=0, keepdims=True))
    alpha = jnp.exp2(m_old - m_new)
    p = jnp.exp2(s - m_new)
    l_ref[idx] = alpha * l_ref[idx] + jnp.sum(p, axis=0, keepdims=True)
    acc_ref[idx] = alpha * acc_ref[idx] + jnp.dot(vt, p.astype(BF16), preferred_element_type=F32)
    m_ref[idx] = m_new


def _split3(x):
    p1 = x.astype(BF16)
    r1 = x - p1.astype(F32)
    p2 = r1.astype(BF16)
    return p1, p2, (r1 - p2.astype(F32)).astype(BF16)


def _online_update_shifted(t, shift, vt, m_ref, l_ref, acc_ref, idx):
    m_old = m_ref[idx]
    m_new = jnp.maximum(m_old, jnp.max(t, axis=0, keepdims=True) + shift)
    alpha = jnp.exp2(m_old - m_new)
    p = jnp.exp2(t - (m_new - shift))
    l_ref[idx] = alpha * l_ref[idx] + jnp.sum(p, axis=0, keepdims=True)
    acc_ref[idx] = alpha * acc_ref[idx] + jnp.dot(vt, p.astype(BF16), preferred_element_type=F32)
    m_ref[idx] = m_new


def _diff_kernel(slope_ref, q_ref, k1_ref, k2_ref, vt_ref, lam_ref, g_ref, o_ref, bm_ref, m_ref, l_ref, acc_ref,
                 *, tq, tk, cq, nk, lam_init):
    h = pl.program_id(1)
    i = pl.program_id(2)
    j = pl.program_id(3)
    sigma = slope_ref[h] * LOG2E

    @pl.when(j == 0)
    def _():
        m_ref[...] = jnp.full_like(m_ref, NEG_INF)
        l_ref[...] = jnp.zeros_like(l_ref)
        acc_ref[...] = jnp.zeros_like(acc_ref)
        dbase = (lax.broadcasted_iota(jnp.int32, (tk, cq), 1)
                 - lax.broadcasted_iota(jnp.int32, (tk, cq), 0)).astype(F32)
        bm_ref[...] = dbase * sigma

    vt = vt_ref[0, 0]

    def sweep(frame):
        for c in range(tq // cq):
            cols = slice(c * cq, (c + 1) * cq)
            soff = sigma * (i * tq + c * cq - j * tk).astype(F32)
            for mi, k_ref in enumerate((k1_ref, k2_ref)):
                s = jnp.dot(k_ref[0], q_ref[0, 0, mi, :, cols], preferred_element_type=F32)
                t, shift = frame(s, soff)
                _online_update_shifted(t, shift, vt, m_ref, l_ref, acc_ref, (mi, slice(None), cols))

    @pl.when(i * tq >= (j + 1) * tk)
    def _():
        sweep(lambda s, soff: (s - bm_ref[...], -soff))

    @pl.when((i + 1) * tq <= j * tk)
    def _():
        sweep(lambda s, soff: (s + bm_ref[...], soff))

    @pl.when((i * tq < (j + 1) * tk) & ((i + 1) * tq > j * tk))
    def _():
        sweep(lambda s, soff: (s - jnp.abs(bm_ref[...] + soff), 0.0))

    @pl.when(j == nk - 1)
    def _():
        lv = lam_ref[...]
        lam = (jnp.exp(jnp.sum(lv[0:1] * lv[1:2], axis=-1, keepdims=True))
               - jnp.exp(jnp.sum(lv[2:3] * lv[3:4], axis=-1, keepdims=True)) + lam_init)
        ot = acc_ref[0] / l_ref[0] - lam * (acc_ref[1] / l_ref[1])
        o = ot.T
        r = lax.rsqrt(jnp.mean(o * o, axis=-1, keepdims=True) + RMS_EPS)
        o_ref[0] = (o * r * g_ref[...] * (1.0 - lam_init)).astype(o_ref.dtype)


def _diff_attention(proj, slopes, lamvec, subln_g, lam_init):
    bsz, seq, _ = proj.shape
    tq = _pick(seq, (2048, 1024, 512))
    tk = _pick(seq, (2048, 1024, 512))
    cq = 256
    nk = seq // tk
    kcol = 2560 // HEAD_DIM
    qt = proj[:, :, 1536:2560].reshape(bsz, seq, DIFF_HEADS, 2, HEAD_DIM).transpose(0, 2, 3, 4, 1)
    vt = proj[:, :, 3584:4608].reshape(bsz, seq, DIFF_HEADS, DIFF_V_DIM).transpose(0, 2, 3, 1)
    smem = pl.BlockSpec(memory_space=pltpu.SMEM)

    def kspec(off):
        return pl.BlockSpec((1, tk, HEAD_DIM), lambda b, h, i, j: (b, j, kcol + 2 * h + off))

    return pl.pallas_call(
        functools.partial(_diff_kernel, tq=tq, tk=tk, cq=cq, nk=nk, lam_init=lam_init),
        grid=(bsz, DIFF_HEADS, seq // tq, nk),
        in_specs=[smem,
                  pl.BlockSpec((1, 1, 2, HEAD_DIM, tq), lambda b, h, i, j: (b, h, 0, 0, i)),
                  kspec(0), kspec(1),
                  pl.BlockSpec((1, 1, DIFF_V_DIM, tk), lambda b, h, i, j: (b, h, 0, j)),
                  pl.BlockSpec((4, HEAD_DIM), lambda b, h, i, j: (0, 0)),
                  pl.BlockSpec((1, DIFF_V_DIM), lambda b, h, i, j: (0, 0))],
        out_specs=pl.BlockSpec((1, tq, DIFF_V_DIM), lambda b, h, i, j: (b, i, h)),
        out_shape=jax.ShapeDtypeStruct((bsz, seq, DIFF_HEADS * DIFF_V_DIM), BF16),
        scratch_shapes=[pltpu.VMEM((tk, cq), F32),
                        pltpu.VMEM((2, 1, tq), F32), pltpu.VMEM((2, 1, tq), F32),
                        pltpu.VMEM((2, DIFF_V_DIM, tq), F32)],
        compiler_params=_cparams(("parallel", "parallel", "parallel", "arbitrary")),
        name="diff_attn",
    )(slopes, qt, proj, proj, vt, lamvec, subln_g)


def _normrope_kernel(x_ref, cos_ref, sin_ref, gq_ref, gk_ref, o_ref, *, tm):
    j = pl.program_id(2)
    is_q = j < AX_Q_HEADS // 2
    g = jnp.where(is_q, gq_ref[...], gk_ref[...])
    post = jnp.where(is_q, ATTN_SCALE * LOG2E, 1.0).astype(F32)
    lane = lax.broadcasted_iota(jnp.int32, (tm, HEAD_DIM), 1)
    first = (lane % 64) < 32
    cos = cos_ref[...]
    sin = sin_ref[...]
    for hh in range(2):
        x = x_ref[0, :, hh * HEAD_DIM:(hh + 1) * HEAD_DIM].astype(F32)
        xn = x * lax.rsqrt(jnp.mean(x * x, axis=-1, keepdims=True) + RMS_EPS) * g
        partner = jnp.where(first, pltpu.roll(xn, 96, 1), pltpu.roll(xn, 32, 1))
        y = (xn * cos + partner * sin) * post
        o_ref[0, :, hh * HEAD_DIM:(hh + 1) * HEAD_DIM] = y.astype(o_ref.dtype)


def _rope_tables(seq):
    half = HEAD_DIM // 2
    inv = jnp.asarray(1.0 / (ROPE_THETA ** (np.arange(0, half, 2) / half)), dtype=F32)
    t = jnp.arange(seq)
    ang_r = (t // GRID_W).astype(F32)[:, None] * inv[None, :]
    ang_c = (t % GRID_W).astype(F32)[:, None] * inv[None, :]
    cr, sr, cc, sn = jnp.cos(ang_r), jnp.sin(ang_r), jnp.cos(ang_c), jnp.sin(ang_c)
    return (jnp.concatenate([cr, cr, cc, cc], axis=-1), jnp.concatenate([-sr, sr, -sn, sn], axis=-1))


def _normrope(proj, qnorm_g, knorm_g):
    bsz, seq, _ = proj.shape
    tm = _pick(seq, (512,))
    cos, sin = _rope_tables(seq)
    nw = (AX_Q_HEADS + AX_KV_HEADS) * HEAD_DIM
    return pl.pallas_call(
        functools.partial(_normrope_kernel, tm=tm),
        grid=(bsz, seq // tm, nw // 256),
        in_specs=[pl.BlockSpec((1, tm, 256), lambda b, i, j: (b, i, j)),
                  pl.BlockSpec((tm, HEAD_DIM), lambda b, i, j: (i, 0)),
                  pl.BlockSpec((tm, HEAD_DIM), lambda b, i, j: (i, 0)),
                  pl.BlockSpec((1, HEAD_DIM), lambda b, i, j: (0, 0)),
                  pl.BlockSpec((1, HEAD_DIM), lambda b, i, j: (0, 0))],
        out_specs=pl.BlockSpec((1, tm, 256), lambda b, i, j: (b, i, j)),
        out_shape=jax.ShapeDtypeStruct((bsz, seq, nw), BF16),
        compiler_params=_cparams(("parallel", "parallel", "parallel")),
        name="normrope",
    )(proj, cos, sin, qnorm_g.reshape(1, HEAD_DIM), knorm_g.reshape(1, HEAD_DIM))


def _gqa_kernel(q_ref, k_ref, vt_ref, o_ref, m_ref, l_ref, acc_ref, *, tq, cq, nk):
    j = pl.program_id(3)

    @pl.when(j == 0)
    def _():
        m_ref[...] = jnp.full_like(m_ref, NEG_INF)
        l_ref[...] = jnp.zeros_like(l_ref)
        acc_ref[...] = jnp.zeros_like(acc_ref)

    tk = k_ref.shape[1]
    per = tq // cq
    for c in range(GQA_GROUP * per):
        g, sb = divmod(c, per)
        cols = slice(c * cq, (c + 1) * cq)
        qs = q_ref[0, g, :, sb * cq:(sb + 1) * cq]
        m = m_ref[:, cols]
        l = l_ref[:, cols]
        acc = acc_ref[:, cols]
        for kb in range(tk // GQA_KEY_SUB):
            rows = slice(kb * GQA_KEY_SUB, (kb + 1) * GQA_KEY_SUB)
            s = jnp.dot(k_ref[0, rows, :], qs, preferred_element_type=F32)
            m_new = jnp.maximum(m, jnp.max(s, axis=0, keepdims=True))
            alpha = jnp.exp2(m - m_new)
            p = jnp.exp2(s - m_new)
            l = alpha * l + jnp.sum(p, axis=0, keepdims=True)
            acc = alpha * acc + jnp.dot(vt_ref[0, 0, :, rows], p.astype(BF16), preferred_element_type=F32)
            m = m_new
        m_ref[:, cols] = m
        l_ref[:, cols] = l
        acc_ref[:, cols] = acc

    @pl.when(j == nk - 1)
    def _():
        ot = acc_ref[...] / l_ref[...]
        for g in range(GQA_GROUP):
            o_ref[0, :, g * HEAD_DIM:(g + 1) * HEAD_DIM] = ot[:, g * tq:(g + 1) * tq].T.astype(o_ref.dtype)


def _gqa_attention(qk, proj):
    bsz, seq, _ = qk.shape
    tq = _pick(seq, (512,))
    tk = _pick(seq, (1024, 512))
    cq = 128
    nk = seq // tk
    gw = GQA_GROUP * HEAD_DIM
    qt = qk[:, :, :AX_Q_HEADS * HEAD_DIM].reshape(bsz, seq, AX_Q_HEADS, HEAD_DIM).transpose(0, 2, 3, 1)
    vt = proj[:, :, 1280:1536].reshape(bsz, seq, AX_KV_HEADS, HEAD_DIM).transpose(0, 2, 3, 1)
    return pl.pallas_call(
        functools.partial(_gqa_kernel, tq=tq, cq=cq, nk=nk),
        grid=(bsz, AX_KV_HEADS, seq // tq, nk),
        in_specs=[pl.BlockSpec((1, GQA_GROUP, HEAD_DIM, tq), lambda b, h, i, j: (b, h, 0, i)),
                  pl.BlockSpec((1, tk, HEAD_DIM), lambda b, h, i, j: (b, j, AX_Q_HEADS + h)),
                  pl.BlockSpec((1, 1, HEAD_DIM, tk), lambda b, h, i, j: (b, h, 0, j))],
        out_specs=pl.BlockSpec((1, tq, gw), lambda b, h, i, j: (b, i, h)),
        out_shape=jax.ShapeDtypeStruct((bsz, seq, AX_Q_HEADS * HEAD_DIM), BF16),
        scratch_shapes=[pltpu.VMEM((1, GQA_GROUP * tq), F32), pltpu.VMEM((1, GQA_GROUP * tq), F32),
                        pltpu.VMEM((HEAD_DIM, GQA_GROUP * tq), F32)],
        compiler_params=_cparams(("parallel", "parallel", "parallel", "arbitrary")),
        name="gqa_attn",
    )(qt, qk, vt)


def _na_bias_table(rpb):
    qi = np.arange(NA_ROWS)[:, None, None, None]
    qc = np.arange(GRID_W)[None, :, None, None]
    kj = np.arange(2 * NA_ROWS)[None, None, :, None]
    kc = np.arange(GRID_W)[None, None, None, :]
    c_start = np.clip(qc - NA_COLS // 2, 0, GRID_W - NA_COLS)
    col_ok = (kc >= c_start) & (kc < c_start + NA_COLS)
    rel_c = np.clip(kc - qc + NA_COLS - 1, 0, 2 * NA_COLS - 2)
    onehot_c = jnp.asarray(rel_c[0, :, 0, :, None] == np.arange(2 * NA_COLS - 1), dtype=F32)
    tabs = []
    for off, clamp in ((0, np.maximum), (-NA_ROWS // 2, None), (-NA_ROWS, np.minimum)):
        rs = qi - NA_ROWS // 2
        if clamp is not None:
            rs = clamp(rs, 0)
        kr = off + kj
        ok = (kr >= rs) & (kr < rs + NA_ROWS) & col_ok
        rel_r = np.clip(kr - qi + NA_ROWS - 1, 0, 2 * NA_ROWS - 2)[:, 0, :, 0]
        onehot_r = jnp.asarray(rel_r[:, :, None] == np.arange(2 * NA_ROWS - 1), dtype=F32)
        bias = jnp.einsum("ija,hab,cdb->hicjd", onehot_r, rpb.astype(F32), onehot_c, precision=HIGHEST)
        tabs.append(jnp.where(ok[None], bias, NEG_INF).reshape(NA_HEADS, NA_TQ, NA_WIN))
    return jnp.stack(tabs)


def _na_kernel(q_ref, k_ref, v_ref, bias_ref, o_ref, *, rows):
    i = pl.program_id(2)
    ws = jnp.clip(i * NA_ROWS - NA_ROWS // 2, 0, rows - 2 * NA_ROWS)
    start = pl.multiple_of(ws * GRID_W, GRID_W)
    k = k_ref[0, pl.ds(start, NA_WIN), :]
    v = v_ref[0, pl.ds(start, NA_WIN), :]
    s = lax.dot_general(q_ref[0], k, (((1,), (1,)), ((), ())), preferred_element_type=F32) + bias_ref[0, 0]
    m = jnp.max(s, axis=-1, keepdims=True)
    p = jnp.exp(s - m)
    l = jnp.sum(p, axis=-1, keepdims=True)
    o_ref[0] = (jnp.dot(p.astype(BF16), v, preferred_element_type=F32) / l).astype(o_ref.dtype)


def _na_attention(proj, bias_tab):
    bsz, seq, _ = proj.shape
    rows = seq // GRID_W
    nq = seq // NA_TQ
    assert rows >= 3 * NA_ROWS and seq % NA_TQ == 0
    qcol, kcol, vcol = 1536 // HEAD_DIM, 2560 // HEAD_DIM, 3584 // HEAD_DIM

    def case(i):
        return jnp.where(i == 0, 0, jnp.where(i == nq - 1, 2, 1))

    return pl.pallas_call(
        functools.partial(_na_kernel, rows=rows),
        grid=(bsz, NA_HEADS, nq),
        in_specs=[pl.BlockSpec((1, NA_TQ, HEAD_DIM), lambda b, h, i: (b, i, qcol + h)),
                  pl.BlockSpec((1, seq, HEAD_DIM), lambda b, h, i: (b, 0, kcol + h)),
                  pl.BlockSpec((1, seq, HEAD_DIM), lambda b, h, i: (b, 0, vcol + h)),
                  pl.BlockSpec((1, 1, NA_TQ, NA_WIN), lambda b, h, i: (case(i), h, 0, 0))],
        out_specs=pl.BlockSpec((1, NA_TQ, HEAD_DIM), lambda b, h, i: (b, i, h)),
        out_shape=jax.ShapeDtypeStruct((bsz, seq, NA_HEADS * HEAD_DIM), BF16),
        compiler_params=_cparams(("parallel", "parallel", "arbitrary")),
        name="na_attn",
    )(proj, proj, proj, bias_tab)


def _outproj_kernel(oa_ref, ob_ref, wa_ref, wb_ref, x_ref, g_ref, lng_ref, lnb_ref, o_ref):
    m = (jnp.dot(oa_ref[0], wa_ref[...], preferred_element_type=F32)
         + jnp.dot(ob_ref[0], wb_ref[...], preferred_element_type=F32))
    y = DEEPNORM_ALPHA * x_ref[0] + g_ref[0] * m
    o_ref[0] = _layer_norm(y, lng_ref[...], lnb_ref[...])


def _outproj_ln(oa, ob, w_out, x, gate, ln_g, ln_b):
    bsz, seq, d = x.shape
    half = oa.shape[-1]
    tm = _pick(seq, (512, 256))
    vec = pl.BlockSpec((1, d), lambda b, i: (0, 0))
    return pl.pallas_call(
        _outproj_kernel,
        grid=(bsz, seq // tm),
        in_specs=[pl.BlockSpec((1, tm, half), lambda b, i: (b, i, 0)),
                  pl.BlockSpec((1, tm, half), lambda b, i: (b, i, 0)),
                  pl.BlockSpec((half, d), lambda b, i: (0, 0)),
                  pl.BlockSpec((half, d), lambda b, i: (1, 0)),
                  pl.BlockSpec((1, tm, d), lambda b, i: (b, i, 0)),
                  pl.BlockSpec((1, 1, d), lambda b, i: (b, 0, 0)),
                  vec, vec],
        out_specs=pl.BlockSpec((1, tm, d), lambda b, i: (b, i, 0)),
        out_shape=jax.ShapeDtypeStruct((bsz, seq, d), F32),
        compiler_params=_cparams(("parallel", "parallel")),
        name="outproj_ln",
    )(oa, ob, w_out, w_out, x, gate, ln_g.reshape(1, d), ln_b.reshape(1, d))


def _route(logits):
    lane = lax.broadcasted_iota(jnp.int32, logits.shape, 1)
    big = jnp.int32(2 ** 30)
    is_g = lane < N_GROUPS
    mg = jnp.max(jnp.where(is_g, logits, -jnp.inf), axis=-1, keepdims=True)
    g_idx = jnp.min(jnp.where(is_g & (logits == mg), lane, big), axis=-1, keepdims=True)
    p_g = 1.0 / jnp.sum(jnp.where(is_g, jnp.exp(logits - mg), 0.0), axis=-1, keepdims=True)
    lo = N_GROUPS + EXPERTS_PER_GROUP * g_idx
    in_grp = (lane >= lo) & (lane < lo + EXPERTS_PER_GROUP)
    v1 = jnp.max(jnp.where(in_grp, logits, -jnp.inf), axis=-1, keepdims=True)
    i1 = jnp.min(jnp.where(in_grp & (logits == v1), lane, big), axis=-1, keepdims=True)
    rest = in_grp & (lane != i1)
    v2 = jnp.max(jnp.where(rest, logits, -jnp.inf), axis=-1, keepdims=True)
    i2 = jnp.min(jnp.where(rest & (logits == v2), lane, big), axis=-1, keepdims=True)
    e2 = jnp.exp(v2 - v1)
    w1 = p_g / (1.0 + e2)
    w2 = p_g * e2 / (1.0 + e2)
    gates = jnp.where(lane == i1, w1, 0.0) + jnp.where(lane == i2, w2, 0.0)
    return jnp.where(lane == 0, g_idx.astype(F32), gates)


def _moe_route_kernel(x_ref, sc_ref, sh_ref, wr_ref, br_ref, h_ref, gate_ref):
    h = x_ref[0] * (1.0 + sc_ref[0]) + sh_ref[0]
    h_ref[0] = h.astype(BF16)
    logits = jnp.dot(h, wr_ref[...], precision=HIGHEST, preferred_element_type=F32) + br_ref[...]
    gate_ref[0] = _route(logits)


def _moe_route(x, sc, sh, w_router, b_router):
    bsz, seq, d = x.shape
    tm = _pick(seq, (512, 256))
    mod = pl.BlockSpec((1, 1, d), lambda b, i: (b, 0, 0))
    return pl.pallas_call(
        _moe_route_kernel,
        grid=(bsz, seq // tm),
        in_specs=[pl.BlockSpec((1, tm, d), lambda b, i: (b, i, 0)), mod, mod,
                  pl.BlockSpec((d, ROUTER_LANES), lambda b, i: (0, 0)),
                  pl.BlockSpec((1, ROUTER_LANES), lambda b, i: (0, 0))],
        out_specs=[pl.BlockSpec((1, tm, d), lambda b, i: (b, i, 0)),
                   pl.BlockSpec((1, tm, ROUTER_LANES), lambda b, i: (b, i, 0))],
        out_shape=[jax.ShapeDtypeStruct((bsz, seq, d), BF16),
                   jax.ShapeDtypeStruct((bsz, seq, ROUTER_LANES), F32)],
        compiler_params=_cparams(("parallel", "parallel")),
        name="moe_route",
    )(x, sc, sh, w_router, b_router)


def _moe_expert_kernel(h_ref, gate_ref, wg_ref, wu_ref, wd_ref, o_ref, pmt_ref, hs_ref, gs_ref, ys_ref, bounds_ref,
                       *, tm, chunk):
    e = pl.program_id(2)

    @pl.when(e == 0)
    def _():
        rec = gate_ref[0]
        lane = lax.broadcasted_iota(jnp.int32, (tm, ROUTER_LANES), 1)
        onehot = jnp.where((lane < N_GROUPS) & (lane == rec[:, 0:1].astype(jnp.int32)), 1.0, 0.0)
        below = (lax.broadcasted_iota(jnp.int32, (tm, tm), 1)
                 < lax.broadcasted_iota(jnp.int32, (tm, tm), 0)).astype(BF16)
        rank = jnp.dot(below, onehot.astype(BF16), preferred_element_type=F32)
        cnt = jnp.sum(onehot, axis=0, keepdims=True)
        padded = jnp.floor((cnt + (SORT_ALIGN - 1)) * (1.0 / SORT_ALIGN)) * SORT_ALIGN
        lane1 = lax.broadcasted_iota(jnp.int32, (1, ROUTER_LANES), 1)
        starts = jnp.zeros((1, ROUTER_LANES), F32)
        for g in range(N_GROUPS):
            first = jnp.sum(jnp.where(lane1 < g, padded, 0.0))
            size = jnp.sum(jnp.where(lane1 == g, cnt, 0.0))
            starts = starts + jnp.where(lane1 == g, first, 0.0)
            bounds_ref[2 * g] = first.astype(jnp.int32)
            bounds_ref[2 * g + 1] = (size.astype(jnp.int32) + chunk - 1) // chunk
        pos = jnp.sum(onehot * (rank + starts), axis=1, keepdims=True).astype(jnp.int32)
        pmt = (lax.broadcasted_iota(jnp.int32, pmt_ref.shape, 1) == pos).astype(BF16)
        pmt_ref[...] = pmt
        dn = (((0,), (0,)), ((), ()))
        hs_ref[...] = lax.dot_general(pmt, h_ref[0], dn, preferred_element_type=F32).astype(BF16)
        gs = jnp.zeros(gs_ref.shape, F32)
        for piece in _split3(rec):
            gs = gs + lax.dot_general(pmt, piece, dn, preferred_element_type=F32)
        gs_ref[...] = gs
        ys_ref[...] = jnp.zeros_like(ys_ref)

    grp = e // EXPERTS_PER_GROUP
    lane_c = lax.broadcasted_iota(jnp.int32, (chunk, ROUTER_LANES), 1)
    first_row = bounds_ref[2 * grp]

    def body(c, carry):
        rows = pl.ds(pl.multiple_of(first_row + c * chunk, SORT_ALIGN), chunk)
        hb = hs_ref[rows, :]
        a = jnp.dot(hb, wg_ref[0], preferred_element_type=F32)
        u = jnp.dot(hb, wu_ref[0], preferred_element_type=F32)
        ge = jnp.sum(jnp.where(lane_c == e + N_GROUPS, gs_ref[rows, :], 0.0), axis=-1, keepdims=True)
        act = (_silu(a) * u * ge).astype(BF16)
        ys_ref[rows, :] += jnp.dot(act, wd_ref[0], preferred_element_type=F32)
        return carry

    lax.fori_loop(0, bounds_ref[2 * grp + 1], body, 0)

    @pl.when(e == N_EXPERTS - 1)
    def _():
        o_ref[0] = jnp.dot(pmt_ref[...], ys_ref[...].astype(BF16), preferred_element_type=F32).astype(o_ref.dtype)


def _moe_experts(h, rec, w_gate, w_up, w_down):
    bsz, seq, d = h.shape
    tm = _pick(seq, (1024, 512))
    chunk = 128
    ps = tm + 2 * chunk
    assert N_GROUPS * (SORT_ALIGN - 1) + chunk <= 2 * chunk
    return pl.pallas_call(
        functools.partial(_moe_expert_kernel, tm=tm, chunk=chunk),
        grid=(bsz, seq // tm, N_EXPERTS),
        in_specs=[pl.BlockSpec((1, tm, d), lambda b, i, e: (b, i, 0)),
                  pl.BlockSpec((1, tm, ROUTER_LANES), lambda b, i, e: (b, i, 0)),
                  pl.BlockSpec((1, d, D_EXPERT), lambda b, i, e: (e, 0, 0)),
                  pl.BlockSpec((1, d, D_EXPERT), lambda b, i, e: (e, 0, 0)),
                  pl.BlockSpec((1, D_EXPERT, d), lambda b, i, e: (e, 0, 0))],
        out_specs=pl.BlockSpec((1, tm, d), lambda b, i, e: (b, i, 0)),
        out_shape=jax.ShapeDtypeStruct((bsz, seq, d), BF16),
        scratch_shapes=[pltpu.VMEM((tm, ps), BF16), pltpu.VMEM((ps, d), BF16), pltpu.VMEM((ps, ROUTER_LANES), F32),
                        pltpu.VMEM((ps, d), F32), pltpu.SMEM((2 * N_GROUPS,), jnp.int32)],
        compiler_params=_cparams(("parallel", "parallel", "arbitrary")),
        name="moe_experts",
    )(h, rec, w_gate, w_up, w_down)


def _resid_ln_kernel(x_ref, f_ref, g_ref, lng_ref, lnb_ref, o_ref):
    y = DEEPNORM_ALPHA * x_ref[0] + g_ref[0] * f_ref[0].astype(F32)
    o_ref[0] = _layer_norm(y, lng_ref[...], lnb_ref[...])


def _resid_ln(x, f, gate, ln_g, ln_b):
    bsz, seq, d = x.shape
    tm = _pick(seq, (512, 256))
    vec = pl.BlockSpec((1, d), lambda b, i: (0, 0))
    tile = pl.BlockSpec((1, tm, d), lambda b, i: (b, i, 0))
    return pl.pallas_call(
        _resid_ln_kernel,
        grid=(bsz, seq // tm),
        in_specs=[tile, tile, pl.BlockSpec((1, 1, d), lambda b, i: (b, 0, 0)), vec, vec],
        out_specs=tile,
        out_shape=jax.ShapeDtypeStruct((bsz, seq, d), F32),
        compiler_params=_cparams(("parallel", "parallel")),
        name="resid_ln",
    )(x, f, gate, ln_g.reshape(1, d), ln_b.reshape(1, d))


def _moe_ln(x, sc, sh, gate, w_router, b_router, w_gate, w_up, w_down, ln_g, ln_b):
    h, rec = _moe_route(x, sc, sh, w_router, b_router)
    f = _moe_experts(h, rec, w_gate, w_up, w_down)
    return _resid_ln(x, f, gate, ln_g, ln_b)


def _alibi_slopes(n):
    return jnp.asarray(2.0 ** (-8.0 * np.arange(1, n + 1) / n), dtype=F32)


def _colscale(q_ranges):
    cs = np.ones((1, PROJ_N), np.float32)
    for lo, hi, scale in q_ranges:
        cs[:, lo:hi] = scale
    return jnp.asarray(cs)


def _prep_params(p):
    q = dict(p)
    for name in ("w_in_ab", "w_out_ab", "w_in_cd", "w_out_cd", "w_gate", "w_up", "w_down"):
        q[name] = p[name].astype(BF16)
    pad = ROUTER_LANES - N_GROUPS - N_EXPERTS
    q["w_router"] = jnp.pad(jnp.concatenate([p["w_rg"], p["w_re"]], axis=-1), ((0, 0), (0, 0), (0, pad)))
    q["b_router"] = jnp.pad(jnp.concatenate([p["b_rg"], p["b_re"]], axis=-1), ((0, 0), (0, pad)))[:, None, :]
    q["na_bias"] = [_na_bias_table(p["rpb_d"][i]) for i in range(p["rpb_d"].shape[0])]
    return q


def _trunk(x, ada, p):
    for l in range(DEPTH):
        sh1, sc1, g1, sh2, sc2, g2 = jnp.split(ada[l][:, None, :], 6, axis=-1)
        i = l // 2
        if l % 2 == 0:
            lam_init = 0.8 - 0.6 * math.exp(-0.3 * l)
            proj = _proj(x, sc1, sh1, p["w_in_ab"][i], _colscale(((0, 1024, ATTN_SCALE), (1536, 2560, ATTN_SCALE * LOG2E))))
            oa = _win_attention(proj, p["sink_a"][i], _alibi_slopes(WIN_Q_HEADS))
            lamvec = jnp.stack([p["lam_q1"][i], p["lam_k1"][i], p["lam_q2"][i], p["lam_k2"][i]])
            ob = _diff_attention(proj, _alibi_slopes(DIFF_HEADS), lamvec, p["subln_g"][i][None, :], lam_init)
            w_out = p["w_out_ab"][i]
        else:
            proj = _proj(x, sc1, sh1, p["w_in_cd"][i], _colscale(((1536, 2560, ATTN_SCALE),)))
            qk = _normrope(proj, p["qnorm_c"][i], p["knorm_c"][i])
            oa = _gqa_attention(qk, proj)
            ob = _na_attention(proj, p["na_bias"][i])
            w_out = p["w_out_cd"][i]
        x = _outproj_ln(oa, ob, w_out, x, g1, p["ln_g"][l, 0], p["ln_b"][l, 0])
        x = _moe_ln(x, sc2, sh2, g2, p["w_router"][l], p["b_router"][l],
                    p["w_gate"][l], p["w_up"][l], p["w_down"][l], p["ln_g"][l, 1], p["ln_b"][l, 1])
    return x


def kernel(x_prompt, x_sample, c_prompt, c_sample, w_ada, b_ada, ln_g, ln_b, w_in_ab, w_out_ab, sink_a,
           lam_q1, lam_k1, lam_q2, lam_k2, subln_g, w_in_cd, w_out_cd, qnorm_c, knorm_c, rpb_d,
           w_rg, b_rg, w_re, b_re, w_gate, w_up, w_down):
    p = _prep_params(dict(
        ln_g=ln_g, ln_b=ln_b, w_in_ab=w_in_ab, w_out_ab=w_out_ab, sink_a=sink_a,
        lam_q1=lam_q1, lam_k1=lam_k1, lam_q2=lam_q2, lam_k2=lam_k2, subln_g=subln_g,
        w_in_cd=w_in_cd, w_out_cd=w_out_cd, qnorm_c=qnorm_c, knorm_c=knorm_c, rpb_d=rpb_d,
        w_rg=w_rg, b_rg=b_rg, w_re=w_re, b_re=b_re, w_gate=w_gate, w_up=w_up, w_down=w_down))
    nb_p, nb_s = c_prompt.shape[0], c_sample.shape[0]
    c_all = jnp.concatenate([c_prompt, c_sample, jnp.zeros((8 - nb_p - nb_s, D_MODEL), F32)], axis=0)
    ada = _ada(c_all, w_ada, b_ada)
    y_prompt = _trunk(x_prompt, ada[:, :nb_p], p)
    y_sample = _trunk(x_sample, ada[:, nb_p:nb_p + nb_s], p)
    return (y_prompt, y_sample)
```

```python
import functools
import math

import numpy as np
import jax
import jax.numpy as jnp
from jax import lax
from jax.experimental import pallas as pl
from jax.experimental.pallas import tpu as pltpu

F32 = jnp.float32
BF16 = jnp.bfloat16
HIGHEST = lax.Precision.HIGHEST

D_MODEL = 2048
DEPTH = 2
HEAD_DIM = 128
WINDOW = 128
GRID_W = 64
WIN_Q_HEADS = 8
WIN_KV_HEADS = 2
GQA_GROUP = 4
DIFF_HEADS = 4
DIFF_V_DIM = 256
AX_Q_HEADS = 8
AX_KV_HEADS = 2
NA_HEADS = 8
NA_ROWS = 8
NA_COLS = 16
ROPE_THETA = 10000.0
N_GROUPS = 4
EXPERTS_PER_GROUP = 4
N_EXPERTS = 16
D_EXPERT = 512
LN_EPS = 1e-5
RMS_EPS = 1e-6
NEG_INF = -1e30
DEEPNORM_ALPHA = (2 * DEPTH) ** 0.25
PROJ_N = 4608
ATTN_SCALE = HEAD_DIM ** -0.5
LOG2E = math.log2(math.e)
LANES = 128
ROUTER_LANES = LANES
SORT_ALIGN = 16
VMEM_LIMIT = 56 * 1024 * 1024

GQA_KEY_SUB = 256
NA_TQ = NA_ROWS * GRID_W
NA_WIN = 2 * NA_ROWS * GRID_W


def _cparams(sem):
    return pltpu.CompilerParams(dimension_semantics=sem, vmem_limit_bytes=VMEM_LIMIT)


def _pick(n, cands):
    for c in cands:
        if n % c == 0:
            return c
    raise ValueError(f"no tile in {cands} divides {n}")


def _layer_norm(y, g, b):
    mu = jnp.mean(y, axis=-1, keepdims=True)
    yc = y - mu
    var = jnp.mean(yc * yc, axis=-1, keepdims=True)
    return yc * lax.rsqrt(var + LN_EPS) * g + b


def _silu(a):
    return a / (1.0 + jnp.exp(-a))


def _ada_kernel(c_ref, w_ref, b_ref, o_ref):
    a = _silu(c_ref[...])
    o_ref[0] = jnp.dot(a, w_ref[0], precision=HIGHEST, preferred_element_type=F32) + b_ref[0]


def _ada(c_all, w_ada, b_ada):
    n = 6 * D_MODEL
    tn = 1024
    return pl.pallas_call(
        _ada_kernel,
        grid=(DEPTH, n // tn),
        in_specs=[pl.BlockSpec((8, D_MODEL), lambda l, j: (0, 0)),
                  pl.BlockSpec((1, D_MODEL, tn), lambda l, j: (l, 0, j)),
                  pl.BlockSpec((1, 1, tn), lambda l, j: (l, 0, j))],
        out_specs=pl.BlockSpec((1, 8, tn), lambda l, j: (l, 0, j)),
        out_shape=jax.ShapeDtypeStruct((DEPTH, 8, n), F32),
        compiler_params=_cparams(("parallel", "parallel")),
        name="ada",
    )(c_all, w_ada, b_ada.reshape(DEPTH, 1, n))


def _proj_kernel(x_ref, sc_ref, sh_ref, w_ref, cs_ref, o_ref, h_ref):
    @pl.when(pl.program_id(2) == 0)
    def _():
        h_ref[...] = (x_ref[0] * (1.0 + sc_ref[0]) + sh_ref[0]).astype(BF16)

    acc = jnp.dot(h_ref[...], w_ref[...], preferred_element_type=F32)
    o_ref[0] = (acc * cs_ref[...]).astype(o_ref.dtype)


def _proj(x, sc, sh, w, colscale):
    bsz, seq, d = x.shape
    n = w.shape[1]
    tm = _pick(seq, (512, 256))
    tn = 2304
    return pl.pallas_call(
        _proj_kernel,
        grid=(bsz, seq // tm, n // tn),
        in_specs=[pl.BlockSpec((1, tm, d), lambda b, i, j: (b, i, 0)),
                  pl.BlockSpec((1, 1, d), lambda b, i, j: (b, 0, 0)),
                  pl.BlockSpec((1, 1, d), lambda b, i, j: (b, 0, 0)),
                  pl.BlockSpec((d, tn), lambda b, i, j: (0, j)),
                  pl.BlockSpec((1, tn), lambda b, i, j: (0, j))],
        out_specs=pl.BlockSpec((1, tm, tn), lambda b, i, j: (b, i, j)),
        out_shape=jax.ShapeDtypeStruct((bsz, seq, n), BF16),
        scratch_shapes=[pltpu.VMEM((tm, d), BF16)],
        compiler_params=_cparams(("parallel", "parallel", "arbitrary")),
        name="proj",
    )(x, sc, sh, w, colscale)


def _win_kernel(sink_ref, slope_ref, q_ref, kp_ref, km_ref, kn_ref, vp_ref, vm_ref, vn_ref, o_ref, *, tq, seq):
    i = pl.program_id(1)
    kvh = pl.program_id(2)
    k = jnp.concatenate([kp_ref[0], km_ref[0], kn_ref[0]], axis=0)
    v = jnp.concatenate([vp_ref[0], vm_ref[0], vn_ref[0]], axis=0)
    w = tq + 2 * WINDOW
    qrel = lax.broadcasted_iota(jnp.int32, (tq, w), 0)
    krel = lax.broadcasted_iota(jnp.int32, (tq, w), 1) - WINDOW
    dist = jnp.abs(qrel - krel)
    kabs = krel + i * tq
    valid = (dist <= WINDOW) & (kabs >= 0) & (kabs < seq)
    distf = dist.astype(F32)
    for g in range(GQA_GROUP):
        h = kvh * GQA_GROUP + g
        qg = q_ref[0, :, g * HEAD_DIM:(g + 1) * HEAD_DIM]
        s = lax.dot_general(qg, k, (((1,), (1,)), ((), ())), preferred_element_type=F32)
        s = jnp.where(valid, s - slope_ref[h] * distf, NEG_INF)
        snk = sink_ref[h]
        m = jnp.maximum(jnp.max(s, axis=-1, keepdims=True), snk)
        p = jnp.exp(s - m)
        l = jnp.sum(p, axis=-1, keepdims=True) + jnp.exp(snk - m)
        o = jnp.dot(p.astype(BF16), v, preferred_element_type=F32) / l
        o_ref[0, :, g * HEAD_DIM:(g + 1) * HEAD_DIM] = o.astype(o_ref.dtype)


def _win_attention(proj, sink, slopes):
    bsz, seq, _ = proj.shape
    tq = 256
    r = tq // WINDOW
    nb = seq // WINDOW
    kcol = WIN_Q_HEADS
    vcol = WIN_Q_HEADS + WIN_KV_HEADS

    def prev(col):
        return pl.BlockSpec((1, WINDOW, HEAD_DIM), lambda b, i, k: (b, jnp.maximum(i * r - 1, 0), col + k))

    def main(col):
        return pl.BlockSpec((1, tq, HEAD_DIM), lambda b, i, k: (b, i, col + k))

    def nxt(col):
        return pl.BlockSpec((1, WINDOW, HEAD_DIM), lambda b, i, k: (b, jnp.minimum((i + 1) * r, nb - 1), col + k))

    smem = pl.BlockSpec(memory_space=pltpu.SMEM)
    return pl.pallas_call(
        functools.partial(_win_kernel, tq=tq, seq=seq),
        grid=(bsz, seq // tq, WIN_KV_HEADS),
        in_specs=[smem, smem,
                  pl.BlockSpec((1, tq, GQA_GROUP * HEAD_DIM), lambda b, i, k: (b, i, k)),
                  prev(kcol), main(kcol), nxt(kcol), prev(vcol), main(vcol), nxt(vcol)],
        out_specs=pl.BlockSpec((1, tq, GQA_GROUP * HEAD_DIM), lambda b, i, k: (b, i, k)),
        out_shape=jax.ShapeDtypeStruct((bsz, seq, WIN_Q_HEADS * HEAD_DIM), BF16),
        compiler_params=_cparams(("parallel", "parallel", "parallel")),
        name="win_attn",
    )(sink, slopes, proj, proj, proj, proj, proj, proj, proj)


def _online_update_t(s, vt, m_ref, l_ref, acc_ref, idx):
    m_old = m_ref[idx]
    m_new = jnp.maximum(m_old, jnp.max(s, axis=0, keepdims=True))
    alpha = jnp.exp2(m_old - m_new)
    p = jnp.exp2(s - m_new)
    l_ref[idx] = alpha * l_ref[idx] + jnp.sum(p, axis=0, keepdims=True)
    acc_ref[idx] = alpha * acc_ref[idx] + jnp.dot(vt, p.astype(BF16), preferred_element_type=F32)
    m_ref[idx] = m_new


def _split3(x):
    p1 = x.astype(BF16)
    r1 = x - p1.astype(F32)
    p2 = r1.astype(BF16)
    return p1, p2, (r1 - p2.astype(F32)).astype(BF16)


def _online_update_shifted(t, shift, vt, m_ref, l_ref, acc_ref, idx):
    m_old = m_ref[idx]
    m_new = jnp.maximum(m_old, jnp.max(t, axis=0, keepdims=True) + shift)
    alpha = jnp.exp2(m_old - m_new)
    p = jnp.exp2(t - (m_new - shift))
    l_ref[idx] = alpha * l_ref[idx] + jnp.sum(p, axis=0, keepdims=True)
    acc_ref[idx] = alpha * acc_ref[idx] + jnp.dot(vt, p.astype(BF16), preferred_element_type=F32)
    m_ref[idx] = m_new


def _diff_kernel(slope_ref, q_ref, k1_ref, k2_ref, vt_ref, lam_ref, g_ref, o_ref, bm_ref, m_ref, l_ref, acc_ref,
                 *, tq, tk, cq, nk, lam_init):
    h = pl.program_id(1)
    i = pl.program_id(2)
    j = pl.program_id(3)
    sigma = slope_ref[h] * LOG2E

    @pl.when(j == 0)
    def _():
        m_ref[...] = jnp.full_like(m_ref, NEG_INF)
        l_ref[...] = jnp.zeros_like(l_ref)
        acc_ref[...] = jnp.zeros_like(acc_ref)
        dbase = (lax.broadcasted_iota(jnp.int32, (tk, cq), 1)
                 - lax.broadcasted_iota(jnp.int32, (tk, cq), 0)).astype(F32)
        bm_ref[...] = dbase * sigma

    vt = vt_ref[0, 0]

    def sweep(frame):
        for c in range(tq // cq):
            cols = slice(c * cq, (c + 1) * cq)
            soff = sigma * (i * tq + c * cq - j * tk).astype(F32)
            for mi, k_ref in enumerate((k1_ref, k2_ref)):
                s = jnp.dot(k_ref[0, 0], q_ref[0, 0, mi, :, cols], preferred_element_type=F32)
                t, shift = frame(s, soff)
                _online_update_shifted(t, shift, vt, m_ref, l_ref, acc_ref, (mi, slice(None), cols))

    @pl.when(i * tq >= (j + 1) * tk)
    def _():
        sweep(lambda s, soff: (s - bm_ref[...], -soff))

    @pl.when((i + 1) * tq <= j * tk)
    def _():
        sweep(lambda s, soff: (s + bm_ref[...], soff))

    @pl.when((i * tq < (j + 1) * tk) & ((i + 1) * tq > j * tk))
    def _():
        sweep(lambda s, soff: (s - jnp.abs(bm_ref[...] + soff), 0.0))

    @pl.when(j == nk - 1)
    def _():
        lv = lam_ref[...]
        lam = (jnp.exp(jnp.sum(lv[0:1] * lv[1:2], axis=-1, keepdims=True))
               - jnp.exp(jnp.sum(lv[2:3] * lv[3:4], axis=-1, keepdims=True)) + lam_init)
        ot = acc_ref[0] / l_ref[0] - lam * (acc_ref[1] / l_ref[1])
        o = ot.T
        r = lax.rsqrt(jnp.mean(o * o, axis=-1, keepdims=True) + RMS_EPS)
        o_ref[0] = (o * r * g_ref[...] * (1.0 - lam_init)).astype(o_ref.dtype)


def _diff_attention(proj, slopes, lamvec, subln_g, lam_init):
    bsz, seq, _ = proj.shape
    tq = _pick(seq, (1024, 512))
    tk = _pick(seq, (1024, 512))
    cq = 256
    nk = seq // tk
    qt = proj[:, :, 1536:2560].reshape(bsz, seq, DIFF_HEADS, 2, HEAD_DIM).transpose(0, 2, 3, 4, 1)
    vt = proj[:, :, 3584:4608].reshape(bsz, seq, DIFF_HEADS, DIFF_V_DIM).transpose(0, 2, 3, 1)
    kh = proj[:, :, 2560:3584].reshape(bsz, seq, 2 * DIFF_HEADS, HEAD_DIM).transpose(0, 2, 1, 3)
    smem = pl.BlockSpec(memory_space=pltpu.SMEM)

    def kspec(off):
        return pl.BlockSpec((1, 1, tk, HEAD_DIM), lambda b, h, i, j: (b, 2 * h + off, j, 0))

    return pl.pallas_call(
        functools.partial(_diff_kernel, tq=tq, tk=tk, cq=cq, nk=nk, lam_init=lam_init),
        grid=(bsz, DIFF_HEADS, seq // tq, nk),
        in_specs=[smem,
                  pl.BlockSpec((1, 1, 2, HEAD_DIM, tq), lambda b, h, i, j: (b, h, 0, 0, i)),
                  kspec(0), kspec(1),
                  pl.BlockSpec((1, 1, DIFF_V_DIM, tk), lambda b, h, i, j: (b, h, 0, j)),
                  pl.BlockSpec((4, HEAD_DIM), lambda b, h, i, j: (0, 0)),
                  pl.BlockSpec((1, DIFF_V_DIM), lambda b, h, i, j: (0, 0))],
        out_specs=pl.BlockSpec((1, tq, DIFF_V_DIM), lambda b, h, i, j: (b, i, h)),
        out_shape=jax.ShapeDtypeStruct((bsz, seq, DIFF_HEADS * DIFF_V_DIM), BF16),
        scratch_shapes=[pltpu.VMEM((tk, cq), F32),
                        pltpu.VMEM((2, 1, tq), F32), pltpu.VMEM((2, 1, tq), F32),
                        pltpu.VMEM((2, DIFF_V_DIM, tq), F32)],
        compiler_params=_cparams(("parallel", "parallel", "parallel", "arbitrary")),
        name="diff_attn",
    )(slopes, qt, kh, kh, vt, lamvec, subln_g)


def _normrope_kernel(x_ref, cos_ref, sin_ref, gq_ref, gk_ref, o_ref, *, tm):
    j = pl.program_id(2)
    is_q = j < AX_Q_HEADS // 2
    g = jnp.where(is_q, gq_ref[...], gk_ref[...])
    post = jnp.where(is_q, ATTN_SCALE * LOG2E, 1.0).astype(F32)
    lane = lax.broadcasted_iota(jnp.int32, (tm, HEAD_DIM), 1)
    first = (lane % 64) < 32
    cos = cos_ref[...]
    sin = sin_ref[...]
    for hh in range(2):
        x = x_ref[0, :, hh * HEAD_DIM:(hh + 1) * HEAD_DIM].astype(F32)
        xn = x * lax.rsqrt(jnp.mean(x * x, axis=-1, keepdims=True) + RMS_EPS) * g
        partner = jnp.where(first, pltpu.roll(xn, 96, 1), pltpu.roll(xn, 32, 1))
        y = (xn * cos + partner * sin) * post
        o_ref[0, :, hh * HEAD_DIM:(hh + 1) * HEAD_DIM] = y.astype(o_ref.dtype)


def _rope_tables(seq):
    half = HEAD_DIM // 2
    inv = jnp.asarray(1.0 / (ROPE_THETA ** (np.arange(0, half, 2) / half)), dtype=F32)
    t = jnp.arange(seq)
    ang_r = (t // GRID_W).astype(F32)[:, None] * inv[None, :]
    ang_c = (t % GRID_W).astype(F32)[:, None] * inv[None, :]
    cr, sr, cc, sn = jnp.cos(ang_r), jnp.sin(ang_r), jnp.cos(ang_c), jnp.sin(ang_c)
    return (jnp.concatenate([cr, cr, cc, cc], axis=-1), jnp.concatenate([-sr, sr, -sn, sn], axis=-1))


def _normrope(proj, qnorm_g, knorm_g):
    bsz, seq, _ = proj.shape
    tm = _pick(seq, (512,))
    cos, sin = _rope_tables(seq)
    nw = (AX_Q_HEADS + AX_KV_HEADS) * HEAD_DIM
    return pl.pallas_call(
        functools.partial(_normrope_kernel, tm=tm),
        grid=(bsz, seq // tm, nw // 256),
        in_specs=[pl.BlockSpec((1, tm, 256), lambda b, i, j: (b, i, j)),
                  pl.BlockSpec((tm, HEAD_DIM), lambda b, i, j: (i, 0)),
                  pl.BlockSpec((tm, HEAD_DIM), lambda b, i, j: (i, 0)),
                  pl.BlockSpec((1, HEAD_DIM), lambda b, i, j: (0, 0)),
                  pl.BlockSpec((1, HEAD_DIM), lambda b, i, j: (0, 0))],
        out_specs=pl.BlockSpec((1, tm, 256), lambda b, i, j: (b, i, j)),
        out_shape=jax.ShapeDtypeStruct((bsz, seq, nw), BF16),
        compiler_params=_cparams(("parallel", "parallel", "parallel")),
        name="normrope",
    )(proj, cos, sin, qnorm_g.reshape(1, HEAD_DIM), knorm_g.reshape(1, HEAD_DIM))


def _gqa_kernel(q_ref, k_ref, vt_ref, o_ref, m_ref, l_ref, acc_ref, *, tq, cq, nk):
    j = pl.program_id(3)

    @pl.when(j == 0)
    def _():
        m_ref[...] = jnp.full_like(m_ref, NEG_INF)
        l_ref[...] = jnp.zeros_like(l_ref)
        acc_ref[...] = jnp.zeros_like(acc_ref)

    tk = k_ref.shape[2]
    per = tq // cq
    for c in range(GQA_GROUP * per):
        g, sb = divmod(c, per)
        cols = slice(c * cq, (c + 1) * cq)
        qs = q_ref[0, g, :, sb * cq:(sb + 1) * cq]
        m = m_ref[:, cols]
        l = l_ref[:, cols]
        acc = acc_ref[:, cols]
        for kb in range(tk // GQA_KEY_SUB):
            rows = slice(kb * GQA_KEY_SUB, (kb + 1) * GQA_KEY_SUB)
            s = jnp.dot(k_ref[0, 0, rows, :], qs, preferred_element_type=F32)
            m_new = jnp.maximum(m, jnp.max(s, axis=0, keepdims=True))
            alpha = jnp.exp2(m - m_new)
            p = jnp.exp2(s - m_new)
            l = alpha * l + jnp.sum(p, axis=0, keepdims=True)
            acc = alpha * acc + jnp.dot(vt_ref[0, 0, :, rows], p.astype(BF16), preferred_element_type=F32)
            m = m_new
        m_ref[:, cols] = m
        l_ref[:, cols] = l
        acc_ref[:, cols] = acc

    @pl.when(j == nk - 1)
    def _():
        ot = acc_ref[...] / l_ref[...]
        for g in range(GQA_GROUP):
            o_ref[0, :, g * HEAD_DIM:(g + 1) * HEAD_DIM] = ot[:, g * tq:(g + 1) * tq].T.astype(o_ref.dtype)


def _gqa_attention(qk, proj):
    bsz, seq, _ = qk.shape
    tq = _pick(seq, (512,))
    tk = _pick(seq, (1024, 512))
    cq = 128
    nk = seq // tk
    gw = GQA_GROUP * HEAD_DIM
    qt = qk[:, :, :AX_Q_HEADS * HEAD_DIM].reshape(bsz, seq, AX_Q_HEADS, HEAD_DIM).transpose(0, 2, 3, 1)
    vt = proj[:, :, 1280:1536].reshape(bsz, seq, AX_KV_HEADS, HEAD_DIM).transpose(0, 2, 3, 1)
    kh = qk[:, :, AX_Q_HEADS * HEAD_DIM:].reshape(bsz, seq, AX_KV_HEADS, HEAD_DIM).transpose(0, 2, 1, 3)
    return pl.pallas_call(
        functools.partial(_gqa_kernel, tq=tq, cq=cq, nk=nk),
        grid=(bsz, AX_KV_HEADS, seq // tq, nk),
        in_specs=[pl.BlockSpec((1, GQA_GROUP, HEAD_DIM, tq), lambda b, h, i, j: (b, h, 0, i)),
                  pl.BlockSpec((1, 1, tk, HEAD_DIM), lambda b, h, i, j: (b, h, j, 0)),
                  pl.BlockSpec((1, 1, HEAD_DIM, tk), lambda b, h, i, j: (b, h, 0, j))],
        out_specs=pl.BlockSpec((1, tq, gw), lambda b, h, i, j: (b, i, h)),
        out_shape=jax.ShapeDtypeStruct((bsz, seq, AX_Q_HEADS * HEAD_DIM), BF16),
        scratch_shapes=[pltpu.VMEM((1, GQA_GROUP * tq), F32), pltpu.VMEM((1, GQA_GROUP * tq), F32),
                        pltpu.VMEM((HEAD_DIM, GQA_GROUP * tq), F32)],
        compiler_params=_cparams(("parallel", "parallel", "parallel", "arbitrary")),
        name="gqa_attn",
    )(qt, kh, vt)


def _na_bias_table(rpb):
    qi = np.arange(NA_ROWS)[:, None, None, None]
    qc = np.arange(GRID_W)[None, :, None, None]
    kj = np.arange(2 * NA_ROWS)[None, None, :, None]
    kc = np.arange(GRID_W)[None, None, None, :]
    c_start = np.clip(qc - NA_COLS // 2, 0, GRID_W - NA_COLS)
    col_ok = (kc >= c_start) & (kc < c_start + NA_COLS)
    rel_c = np.clip(kc - qc + NA_COLS - 1, 0, 2 * NA_COLS - 2)
    onehot_c = jnp.asarray(rel_c[0, :, 0, :, None] == np.arange(2 * NA_COLS - 1), dtype=F32)
    tabs = []
    for off, clamp in ((0, np.maximum), (-NA_ROWS // 2, None), (-NA_ROWS, np.minimum)):
        rs = qi - NA_ROWS // 2
        if clamp is not None:
            rs = clamp(rs, 0)
        kr = off + kj
        ok = (kr >= rs) & (kr < rs + NA_ROWS) & col_ok
        rel_r = np.clip(kr - qi + NA_ROWS - 1, 0, 2 * NA_ROWS - 2)[:, 0, :, 0]
        onehot_r = jnp.asarray(rel_r[:, :, None] == np.arange(2 * NA_ROWS - 1), dtype=F32)
        bias = jnp.einsum("ija,hab,cdb->hicjd", onehot_r, rpb.astype(F32), onehot_c, precision=HIGHEST)
        tabs.append(jnp.where(ok[None], bias, NEG_INF).reshape(NA_HEADS, NA_TQ, NA_WIN))
    return jnp.stack(tabs)


def _na_kernel(q_ref, k_ref, v_ref, bias_ref, o_ref, *, rows):
    i = pl.program_id(2)
    ws = jnp.clip(i * NA_ROWS - NA_ROWS // 2, 0, rows - 2 * NA_ROWS)
    start = pl.multiple_of(ws * GRID_W, GRID_W)
    k = k_ref[0, pl.ds(start, NA_WIN), :]
    v = v_ref[0, pl.ds(start, NA_WIN), :]
    s = lax.dot_general(q_ref[0], k, (((1,), (1,)), ((), ())), preferred_element_type=F32) + bias_ref[0, 0]
    m = jnp.max(s, axis=-1, keepdims=True)
    p = jnp.exp(s - m)
    l = jnp.sum(p, axis=-1, keepdims=True)
    o_ref[0] = (jnp.dot(p.astype(BF16), v, preferred_element_type=F32) / l).astype(o_ref.dtype)


def _na_attention(proj, bias_tab):
    bsz, seq, _ = proj.shape
    rows = seq // GRID_W
    nq = seq // NA_TQ
    assert rows >= 3 * NA_ROWS and seq % NA_TQ == 0
    qcol, kcol, vcol = 1536 // HEAD_DIM, 2560 // HEAD_DIM, 3584 // HEAD_DIM

    def case(i):
        return jnp.where(i == 0, 0, jnp.where(i == nq - 1, 2, 1))

    return pl.pallas_call(
        functools.partial(_na_kernel, rows=rows),
        grid=(bsz, NA_HEADS, nq),
        in_specs=[pl.BlockSpec((1, NA_TQ, HEAD_DIM), lambda b, h, i: (b, i, qcol + h)),
                  pl.BlockSpec((1, seq, HEAD_DIM), lambda b, h, i: (b, 0, kcol + h)),
                  pl.BlockSpec((1, seq, HEAD_DIM), lambda b, h, i: (b, 0, vcol + h)),
                  pl.BlockSpec((1, 1, NA_TQ, NA_WIN), lambda b, h, i: (case(i), h, 0, 0))],
        out_specs=pl.BlockSpec((1, NA_TQ, HEAD_DIM), lambda b, h, i: (b, i, h)),
        out_shape=jax.ShapeDtypeStruct((bsz, seq, NA_HEADS * HEAD_DIM), BF16),
        compiler_params=_cparams(("parallel", "parallel", "arbitrary")),
        name="na_attn",
    )(proj, proj, proj, bias_tab)


def _outproj_kernel(oa_ref, ob_ref, wa_ref, wb_ref, x_ref, g_ref, lng_ref, lnb_ref, o_ref):
    m = (jnp.dot(oa_ref[0], wa_ref[...], preferred_element_type=F32)
         + jnp.dot(ob_ref[0], wb_ref[...], preferred_element_type=F32))
    y = DEEPNORM_ALPHA * x_ref[0] + g_ref[0] * m
    o_ref[0] = _layer_norm(y, lng_ref[...], lnb_ref[...])


def _outproj_ln(oa, ob, w_out, x, gate, ln_g, ln_b):
    bsz, seq, d = x.shape
    half = oa.shape[-1]
    tm = _pick(seq, (512, 256))
    vec = pl.BlockSpec((1, d), lambda b, i: (0, 0))
    return pl.pallas_call(
        _outproj_kernel,
        grid=(bsz, seq // tm),
        in_specs=[pl.BlockSpec((1, tm, half), lambda b, i: (b, i, 0)),
                  pl.BlockSpec((1, tm, half), lambda b, i: (b, i, 0)),
                  pl.BlockSpec((half, d), lambda b, i: (0, 0)),
                  pl.BlockSpec((half, d), lambda b, i: (1, 0)),
                  pl.BlockSpec((1, tm, d), lambda b, i: (b, i, 0)),
                  pl.BlockSpec((1, 1, d), lambda b, i: (b, 0, 0)),
                  vec, vec],
        out_specs=pl.BlockSpec((1, tm, d), lambda b, i: (b, i, 0)),
        out_shape=jax.ShapeDtypeStruct((bsz, seq, d), F32),
        compiler_params=_cparams(("parallel", "parallel")),
        name="outproj_ln",
    )(oa, ob, w_out, w_out, x, gate, ln_g.reshape(1, d), ln_b.reshape(1, d))


def _route(logits):
    lane = lax.broadcasted_iota(jnp.int32, logits.shape, 1)
    big = jnp.int32(2 ** 30)
    is_g = lane < N_GROUPS
    mg = jnp.max(jnp.where(is_g, logits, -jnp.inf), axis=-1, keepdims=True)
    g_idx = jnp.min(jnp.where(is_g & (logits == mg), lane, big), axis=-1, keepdims=True)
    p_g = 1.0 / jnp.sum(jnp.where(is_g, jnp.exp(logits - mg), 0.0), axis=-1, keepdims=True)
    lo = N_GROUPS + EXPERTS_PER_GROUP * g_idx
    in_grp = (lane >= lo) & (lane < lo + EXPERTS_PER_GROUP)
    v1 = jnp.max(jnp.where(in_grp, logits, -jnp.inf), axis=-1, keepdims=True)
    i1 = jnp.min(jnp.where(in_grp & (logits == v1), lane, big), axis=-1, keepdims=True)
    rest = in_grp & (lane != i1)
    v2 = jnp.max(jnp.where(rest, logits, -jnp.inf), axis=-1, keepdims=True)
    i2 = jnp.min(jnp.where(rest & (logits == v2), lane, big), axis=-1, keepdims=True)
    e2 = jnp.exp(v2 - v1)
    w1 = p_g / (1.0 + e2)
    w2 = p_g * e2 / (1.0 + e2)
    gates = jnp.where(lane == i1, w1, 0.0) + jnp.where(lane == i2, w2, 0.0)
    return jnp.where(lane == 0, g_idx.astype(F32), gates)


def _moe_route_kernel(x_ref, sc_ref, sh_ref, wr_ref, br_ref, h_ref, gate_ref):
    h = x_ref[0] * (1.0 + sc_ref[0]) + sh_ref[0]
    h_ref[0] = h.astype(BF16)
    logits = jnp.dot(h, wr_ref[...], precision=HIGHEST, preferred_element_type=F32) + br_ref[...]
    gate_ref[0] = _route(logits)


def _moe_route(x, sc, sh, w_router, b_router):
    bsz, seq, d = x.shape
    tm = _pick(seq, (512, 256))
    mod = pl.BlockSpec((1, 1, d), lambda b, i: (b, 0, 0))
    return pl.pallas_call(
        _moe_route_kernel,
        grid=(bsz, seq // tm),
        in_specs=[pl.BlockSpec((1, tm, d), lambda b, i: (b, i, 0)), mod, mod,
                  pl.BlockSpec((d, ROUTER_LANES), lambda b, i: (0, 0)),
                  pl.BlockSpec((1, ROUTER_LANES), lambda b, i: (0, 0))],
        out_specs=[pl.BlockSpec((1, tm, d), lambda b, i: (b, i, 0)),
                   pl.BlockSpec((1, tm, ROUTER_LANES), lambda b, i: (b, i, 0))],
        out_shape=[jax.ShapeDtypeStruct((bsz, seq, d), BF16),
                   jax.ShapeDtypeStruct((bsz, seq, ROUTER_LANES), F32)],
        compiler_params=_cparams(("parallel", "parallel")),
        name="moe_route",
    )(x, sc, sh, w_router, b_router)


def _moe_expert_kernel(h_ref, gate_ref, wg_ref, wu_ref, wd_ref, o_ref, pmt_ref, hs_ref, gs_ref, ys_ref, bounds_ref,
                       *, tm, chunk):
    e = pl.program_id(2)

    @pl.when(e == 0)
    def _():
        rec = gate_ref[0]
        lane = lax.broadcasted_iota(jnp.int32, (tm, ROUTER_LANES), 1)
        onehot = jnp.where((lane < N_GROUPS) & (lane == rec[:, 0:1].astype(jnp.int32)), 1.0, 0.0)
        below = (lax.broadcasted_iota(jnp.int32, (tm, tm), 1)
                 < lax.broadcasted_iota(jnp.int32, (tm, tm), 0)).astype(BF16)
        rank = jnp.dot(below, onehot.astype(BF16), preferred_element_type=F32)
        cnt = jnp.sum(onehot, axis=0, keepdims=True)
        padded = jnp.floor((cnt + (SORT_ALIGN - 1)) * (1.0 / SORT_ALIGN)) * SORT_ALIGN
        lane1 = lax.broadcasted_iota(jnp.int32, (1, ROUTER_LANES), 1)
        starts = jnp.zeros((1, ROUTER_LANES), F32)
        for g in range(N_GROUPS):
            first = jnp.sum(jnp.where(lane1 < g, padded, 0.0))
            size = jnp.sum(jnp.where(lane1 == g, cnt, 0.0))
            starts = starts + jnp.where(lane1 == g, first, 0.0)
            bounds_ref[2 * g] = first.astype(jnp.int32)
            bounds_ref[2 * g + 1] = (size.astype(jnp.int32) + chunk - 1) // chunk
        pos = jnp.sum(onehot * (rank + starts), axis=1, keepdims=True).astype(jnp.int32)
        pmt = (lax.broadcasted_iota(jnp.int32, pmt_ref.shape, 1) == pos).astype(BF16)
        pmt_ref[...] = pmt
        dn = (((0,), (0,)), ((), ()))
        hs_ref[...] = lax.dot_general(pmt, h_ref[0], dn, preferred_element_type=F32).astype(BF16)
        gs = jnp.zeros(gs_ref.shape, F32)
        for piece in _split3(rec):
            gs = gs + lax.dot_general(pmt, piece, dn, preferred_element_type=F32)
        gs_ref[...] = gs
        ys_ref[...] = jnp.zeros_like(ys_ref)

    grp = e // EXPERTS_PER_GROUP
    lane_c = lax.broadcasted_iota(jnp.int32, (chunk, ROUTER_LANES), 1)
    first_row = bounds_ref[2 * grp]

    def body(c, carry):
        rows = pl.ds(pl.multiple_of(first_row + c * chunk, SORT_ALIGN), chunk)
        hb = hs_ref[rows, :]
        a = jnp.dot(hb, wg_ref[0], preferred_element_type=F32)
        u = jnp.dot(hb, wu_ref[0], preferred_element_type=F32)
        ge = jnp.sum(jnp.where(lane_c == e + N_GROUPS, gs_ref[rows, :], 0.0), axis=-1, keepdims=True)
        act = (_silu(a) * u * ge).astype(BF16)
        ys_ref[rows, :] += jnp.dot(act, wd_ref[0], preferred_element_type=F32)
        return carry

    lax.fori_loop(0, bounds_ref[2 * grp + 1], body, 0)

    @pl.when(e == N_EXPERTS - 1)
    def _():
        o_ref[0] = jnp.dot(pmt_ref[...], ys_ref[...].astype(BF16), preferred_element_type=F32).astype(o_ref.dtype)


def _moe_experts(h, rec, w_gate, w_up, w_down):
    bsz, seq, d = h.shape
    tm = _pick(seq, (1024, 512))
    chunk = 128
    ps = tm + 2 * chunk
    assert N_GROUPS * (SORT_ALIGN - 1) + chunk <= 2 * chunk
    return pl.pallas_call(
        functools.partial(_moe_expert_kernel, tm=tm, chunk=chunk),
        grid=(bsz, seq // tm, N_EXPERTS),
        in_specs=[pl.BlockSpec((1, tm, d), lambda b, i, e: (b, i, 0)),
                  pl.BlockSpec((1, tm, ROUTER_LANES), lambda b, i, e: (b, i, 0)),
                  pl.BlockSpec((1, d, D_EXPERT), lambda b, i, e: (e, 0, 0)),
                  pl.BlockSpec((1, d, D_EXPERT), lambda b, i, e: (e, 0, 0)),
                  pl.BlockSpec((1, D_EXPERT, d), lambda b, i, e: (e, 0, 0))],
        out_specs=pl.BlockSpec((1, tm, d), lambda b, i, e: (b, i, 0)),
        out_shape=jax.ShapeDtypeStruct((bsz, seq, d), BF16),
        scratch_shapes=[pltpu.VMEM((tm, ps), BF16), pltpu.VMEM((ps, d), BF16), pltpu.VMEM((ps, ROUTER_LANES), F32),
                        pltpu.VMEM((ps, d), F32), pltpu.SMEM((2 * N_GROUPS,), jnp.int32)],
        compiler_params=_cparams(("parallel", "parallel", "arbitrary")),
        name="moe_experts",
    )(h, rec, w_gate, w_up, w_down)


def _resid_ln_kernel(x_ref, f_ref, g_ref, lng_ref, lnb_ref, o_ref):
    y = DEEPNORM_ALPHA * x_ref[0] + g_ref[0] * f_ref[0].astype(F32)
    o_ref[0] = _layer_norm(y, lng_ref[...], lnb_ref[...])


def _resid_ln(x, f, gate, ln_g, ln_b):
    bsz, seq, d = x.shape
    tm = _pick(seq, (512, 256))
    vec = pl.BlockSpec((1, d), lambda b, i: (0, 0))
    tile = pl.BlockSpec((1, tm, d), lambda b, i: (b, i, 0))
    return pl.pallas_call(
        _resid_ln_kernel,
        grid=(bsz, seq // tm),
        in_specs=[tile, tile, pl.BlockSpec((1, 1, d), lambda b, i: (b, 0, 0)), vec, vec],
        out_specs=tile,
        out_shape=jax.ShapeDtypeStruct((bsz, seq, d), F32),
        compiler_params=_cparams(("parallel", "parallel")),
        name="resid_ln",
    )(x, f, gate, ln_g.reshape(1, d), ln_b.reshape(1, d))


def _moe_ln(x, sc, sh, gate, w_router, b_router, w_gate, w_up, w_down, ln_g, ln_b):
    h, rec = _moe_route(x, sc, sh, w_router, b_router)
    f = _moe_experts(h, rec, w_gate, w_up, w_down)
    return _resid_ln(x, f, gate, ln_g, ln_b)


def _alibi_slopes(n):
    return jnp.asarray(2.0 ** (-8.0 * np.arange(1, n + 1) / n), dtype=F32)


def _colscale(q_ranges):
    cs = np.ones((1, PROJ_N), np.float32)
    for lo, hi, scale in q_ranges:
        cs[:, lo:hi] = scale
    return jnp.asarray(cs)


def _prep_params(p):
    q = dict(p)
    for name in ("w_in_ab", "w_out_ab", "w_in_cd", "w_out_cd", "w_gate", "w_up", "w_down"):
        q[name] = p[name].astype(BF16)
    pad = ROUTER_LANES - N_GROUPS - N_EXPERTS
    q["w_router"] = jnp.pad(jnp.concatenate([p["w_rg"], p["w_re"]], axis=-1), ((0, 0), (0, 0), (0, pad)))
    q["b_router"] = jnp.pad(jnp.concatenate([p["b_rg"], p["b_re"]], axis=-1), ((0, 0), (0, pad)))[:, None, :]
    q["na_bias"] = [_na_bias_table(p["rpb_d"][i]) for i in range(p["rpb_d"].shape[0])]
    return q


def _trunk(x, ada, p):
    for l in range(DEPTH):
        sh1, sc1, g1, sh2, sc2, g2 = jnp.split(ada[l][:, None, :], 6, axis=-1)
        i = l // 2
        if l % 2 == 0:
            lam_init = 0.8 - 0.6 * math.exp(-0.3 * l)
            proj = _proj(x, sc1, sh1, p["w_in_ab"][i], _colscale(((0, 1024, ATTN_SCALE), (1536, 2560, ATTN_SCALE * LOG2E))))
            oa = _win_attention(proj, p["sink_a"][i], _alibi_slopes(WIN_Q_HEADS))
            lamvec = jnp.stack([p["lam_q1"][i], p["lam_k1"][i], p["lam_q2"][i], p["lam_k2"][i]])
            ob = _diff_attention(proj, _alibi_slopes(DIFF_HEADS), lamvec, p["subln_g"][i][None, :], lam_init)
            w_out = p["w_out_ab"][i]
        else:
            proj = _proj(x, sc1, sh1, p["w_in_cd"][i], _colscale(((1536, 2560, ATTN_SCALE),)))
            qk = _normrope(proj, p["qnorm_c"][i], p["knorm_c"][i])
            oa = _gqa_attention(qk, proj)
            ob = _na_attention(proj, p["na_bias"][i])
            w_out = p["w_out_cd"][i]
        x = _outproj_ln(oa, ob, w_out, x, g1, p["ln_g"][l, 0], p["ln_b"][l, 0])
        x = _moe_ln(x, sc2, sh2, g2, p["w_router"][l], p["b_router"][l],
                    p["w_gate"][l], p["w_up"][l], p["w_down"][l], p["ln_g"][l, 1], p["ln_b"][l, 1])
    return x


def kernel(x_prompt, x_sample, c_prompt, c_sample, w_ada, b_ada, ln_g, ln_b, w_in_ab, w_out_ab, sink_a,
           lam_q1, lam_k1, lam_q2, lam_k2, subln_g, w_in_cd, w_out_cd, qnorm_c, knorm_c, rpb_d,
           w_rg, b_rg, w_re, b_re, w_gate, w_up, w_down):
    p = _prep_params(dict(
        ln_g=ln_g, ln_b=ln_b, w_in_ab=w_in_ab, w_out_ab=w_out_ab, sink_a=sink_a,
        lam_q1=lam_q1, lam_k1=lam_k1, lam_q2=lam_q2, lam_k2=lam_k2, subln_g=subln_g,
        w_in_cd=w_in_cd, w_out_cd=w_out_cd, qnorm_c=qnorm_c, knorm_c=knorm_c, rpb_d=rpb_d,
        w_rg=w_rg, b_rg=b_rg, w_re=w_re, b_re=b_re, w_gate=w_gate, w_up=w_up, w_down=w_down))
    nb_p, nb_s = c_prompt.shape[0], c_sample.shape[0]
    c_all = jnp.concatenate([c_prompt, c_sample, jnp.zeros((8 - nb_p - nb_s, D_MODEL), F32)], axis=0)
    ada = _ada(c_all, w_ada, b_ada)
    y_prompt = _trunk(x_prompt, ada[:, :nb_p], p)
    y_sample = _trunk(x_sample, ada[:, nb_p:nb_p + nb_s], p)
    return (y_prompt, y_sample)
```

```python
import functools
import math

import numpy as np
import jax
import jax.numpy as jnp
from jax import lax
from jax.experimental import pallas as pl
from jax.experimental.pallas import tpu as pltpu

F32 = jnp.float32
BF16 = jnp.bfloat16
HIGHEST = lax.Precision.HIGHEST

D_MODEL = 2048
DEPTH = 2
HEAD_DIM = 128
WINDOW = 128
GRID_W = 64
WIN_Q_HEADS = 8
WIN_KV_HEADS = 2
GQA_GROUP = 4
DIFF_HEADS = 4
DIFF_V_DIM = 256
AX_Q_HEADS = 8
AX_KV_HEADS = 2
NA_HEADS = 8
NA_ROWS = 8
NA_COLS = 16
ROPE_THETA = 10000.0
N_GROUPS = 4
EXPERTS_PER_GROUP = 4
N_EXPERTS = 16
D_EXPERT = 512
LN_EPS = 1e-5
RMS_EPS = 1e-6
NEG_INF = -1e30
DEEPNORM_ALPHA = (2 * DEPTH) ** 0.25
PROJ_N = 4608
ATTN_SCALE = HEAD_DIM ** -0.5
LOG2E = math.log2(math.e)
LANES = 128
ROUTER_LANES = LANES
SORT_ALIGN = 16
VMEM_LIMIT = 56 * 1024 * 1024

GQA_KEY_SUB = 256
DIFF_KEY_SUB = 512
NA_TQ = NA_ROWS * GRID_W
NA_WIN = 2 * NA_ROWS * GRID_W


def _cparams(sem):
    return pltpu.CompilerParams(dimension_semantics=sem, vmem_limit_bytes=VMEM_LIMIT)


def _pick(n, cands):
    for c in cands:
        if n % c == 0:
            return c
    raise ValueError(f"no tile in {cands} divides {n}")


def _layer_norm(y, g, b):
    mu = jnp.mean(y, axis=-1, keepdims=True)
    yc = y - mu
    var = jnp.mean(yc * yc, axis=-1, keepdims=True)
    return yc * lax.rsqrt(var + LN_EPS) * g + b


def _silu(a):
    return a / (1.0 + jnp.exp(-a))


def _ada_kernel(c_ref, w_ref, b_ref, o_ref):
    a = _silu(c_ref[...])
    o_ref[0] = jnp.dot(a, w_ref[0], precision=HIGHEST, preferred_element_type=F32) + b_ref[0]


def _ada(c_all, w_ada, b_ada):
    n = 6 * D_MODEL
    tn = 1024
    return pl.pallas_call(
        _ada_kernel,
        grid=(DEPTH, n // tn),
        in_specs=[pl.BlockSpec((8, D_MODEL), lambda l, j: (0, 0)),
                  pl.BlockSpec((1, D_MODEL, tn), lambda l, j: (l, 0, j)),
                  pl.BlockSpec((1, 1, tn), lambda l, j: (l, 0, j))],
        out_specs=pl.BlockSpec((1, 8, tn), lambda l, j: (l, 0, j)),
        out_shape=jax.ShapeDtypeStruct((DEPTH, 8, n), F32),
        compiler_params=_cparams(("parallel", "parallel")),
        name="ada",
    )(c_all, w_ada, b_ada.reshape(DEPTH, 1, n))


def _proj_kernel(x_ref, sc_ref, sh_ref, w_ref, cs_ref, o_ref, h_ref):
    @pl.when(pl.program_id(2) == 0)
    def _():
        h_ref[...] = (x_ref[0] * (1.0 + sc_ref[0]) + sh_ref[0]).astype(BF16)

    acc = jnp.dot(h_ref[...], w_ref[...], preferred_element_type=F32)
    o_ref[0] = (acc * cs_ref[...]).astype(o_ref.dtype)


def _proj(x, sc, sh, w, colscale):
    bsz, seq, d = x.shape
    n = w.shape[1]
    tm = _pick(seq, (512, 256))
    tn = 2304
    return pl.pallas_call(
        _proj_kernel,
        grid=(bsz, seq // tm, n // tn),
        in_specs=[pl.BlockSpec((1, tm, d), lambda b, i, j: (b, i, 0)),
                  pl.BlockSpec((1, 1, d), lambda b, i, j: (b, 0, 0)),
                  pl.BlockSpec((1, 1, d), lambda b, i, j: (b, 0, 0)),
                  pl.BlockSpec((d, tn), lambda b, i, j: (0, j)),
                  pl.BlockSpec((1, tn), lambda b, i, j: (0, j))],
        out_specs=pl.BlockSpec((1, tm, tn), lambda b, i, j: (b, i, j)),
        out_shape=jax.ShapeDtypeStruct((bsz, seq, n), BF16),
        scratch_shapes=[pltpu.VMEM((tm, d), BF16)],
        compiler_params=_cparams(("parallel", "parallel", "arbitrary")),
        name="proj",
    )(x, sc, sh, w, colscale)


def _win_kernel(sink_ref, slope_ref, q_ref, kp_ref, km_ref, kn_ref, vp_ref, vm_ref, vn_ref, o_ref, *, tq, seq):
    i = pl.program_id(1)
    kvh = pl.program_id(2)
    k = jnp.concatenate([kp_ref[0], km_ref[0], kn_ref[0]], axis=0)
    v = jnp.concatenate([vp_ref[0], vm_ref[0], vn_ref[0]], axis=0)
    w = tq + 2 * WINDOW
    qrel = lax.broadcasted_iota(jnp.int32, (tq, w), 0)
    krel = lax.broadcasted_iota(jnp.int32, (tq, w), 1) - WINDOW
    dist = jnp.abs(qrel - krel)
    kabs = krel + i * tq
    valid = (dist <= WINDOW) & (kabs >= 0) & (kabs < seq)
    distf = dist.astype(F32)
    for g in range(GQA_GROUP):
        h = kvh * GQA_GROUP + g
        qg = q_ref[0, :, g * HEAD_DIM:(g + 1) * HEAD_DIM]
        s = lax.dot_general(qg, k, (((1,), (1,)), ((), ())), preferred_element_type=F32)
        s = jnp.where(valid, s - slope_ref[h] * distf, NEG_INF)
        snk = sink_ref[h]
        m = jnp.maximum(jnp.max(s, axis=-1, keepdims=True), snk)
        p = jnp.exp(s - m)
        l = jnp.sum(p, axis=-1, keepdims=True) + jnp.exp(snk - m)
        o = jnp.dot(p.astype(BF16), v, preferred_element_type=F32) / l
        o_ref[0, :, g * HEAD_DIM:(g + 1) * HEAD_DIM] = o.astype(o_ref.dtype)


def _win_attention(proj, sink, slopes):
    bsz, seq, _ = proj.shape
    tq = 256
    r = tq // WINDOW
    nb = seq // WINDOW
    kcol = WIN_Q_HEADS
    vcol = WIN_Q_HEADS + WIN_KV_HEADS

    def prev(col):
        return pl.BlockSpec((1, WINDOW, HEAD_DIM), lambda b, i, k: (b, jnp.maximum(i * r - 1, 0), col + k))

    def main(col):
        return pl.BlockSpec((1, tq, HEAD_DIM), lambda b, i, k: (b, i, col + k))

    def nxt(col):
        return pl.BlockSpec((1, WINDOW, HEAD_DIM), lambda b, i, k: (b, jnp.minimum((i + 1) * r, nb - 1), col + k))

    smem = pl.BlockSpec(memory_space=pltpu.SMEM)
    return pl.pallas_call(
        functools.partial(_win_kernel, tq=tq, seq=seq),
        grid=(bsz, seq // tq, WIN_KV_HEADS),
        in_specs=[smem, smem,
                  pl.BlockSpec((1, tq, GQA_GROUP * HEAD_DIM), lambda b, i, k: (b, i, k)),
                  prev(kcol), main(kcol), nxt(kcol), prev(vcol), main(vcol), nxt(vcol)],
        out_specs=pl.BlockSpec((1, tq, GQA_GROUP * HEAD_DIM), lambda b, i, k: (b, i, k)),
        out_shape=jax.ShapeDtypeStruct((bsz, seq, WIN_Q_HEADS * HEAD_DIM), BF16),
        compiler_params=_cparams(("parallel", "parallel", "parallel")),
        name="win_attn",
    )(sink, slopes, proj, proj, proj, proj, proj, proj, proj)


def _split3(x):
    p1 = x.astype(BF16)
    r1 = x - p1.astype(F32)
    p2 = r1.astype(BF16)
    return p1, p2, (r1 - p2.astype(F32)).astype(BF16)


def _diff_kernel(slope_ref, q_ref, k1_ref, k2_ref, vt_ref, lam_ref, g_ref, o_ref, bm_ref, m_ref, l_ref, acc_ref,
                 *, tq, tk, cq, nk, lam_init):
    h = pl.program_id(1)
    i = pl.program_id(2)
    j = pl.program_id(3)
    sigma = slope_ref[h] * LOG2E

    @pl.when(j == 0)
    def _():
        m_ref[...] = jnp.full_like(m_ref, NEG_INF)
        l_ref[...] = jnp.zeros_like(l_ref)
        acc_ref[...] = jnp.zeros_like(acc_ref)
        dbase = (lax.broadcasted_iota(jnp.int32, (DIFF_KEY_SUB, cq), 1)
                 - lax.broadcasted_iota(jnp.int32, (DIFF_KEY_SUB, cq), 0)).astype(F32)
        bm_ref[...] = dbase * sigma

    def sweep(frame):
        for c in range(tq // cq):
            cols = slice(c * cq, (c + 1) * cq)
            for mi, k_ref in enumerate((k1_ref, k2_ref)):
                qs = q_ref[0, 0, mi, :, cols]
                m = m_ref[mi, :, cols]
                l = l_ref[mi, :, cols]
                acc = acc_ref[mi, :, cols]
                for kb in range(tk // DIFF_KEY_SUB):
                    rows = slice(kb * DIFF_KEY_SUB, (kb + 1) * DIFF_KEY_SUB)
                    soff = sigma * (i * tq + c * cq - j * tk - kb * DIFF_KEY_SUB).astype(F32)
                    s = jnp.dot(k_ref[0, 0, rows, :], qs, preferred_element_type=F32)
                    t, shift = frame(s, soff)
                    m_new = jnp.maximum(m, jnp.max(t, axis=0, keepdims=True) + shift)
                    alpha = jnp.exp2(m - m_new)
                    p = jnp.exp2(t - (m_new - shift))
                    l = alpha * l + jnp.sum(p, axis=0, keepdims=True)
                    acc = alpha * acc + jnp.dot(vt_ref[0, 0, :, rows], p.astype(BF16), preferred_element_type=F32)
                    m = m_new
                m_ref[mi, :, cols] = m
                l_ref[mi, :, cols] = l
                acc_ref[mi, :, cols] = acc

    @pl.when(i * tq >= (j + 1) * tk)
    def _():
        sweep(lambda s, soff: (s - bm_ref[...], -soff))

    @pl.when((i + 1) * tq <= j * tk)
    def _():
        sweep(lambda s, soff: (s + bm_ref[...], soff))

    @pl.when((i * tq < (j + 1) * tk) & ((i + 1) * tq > j * tk))
    def _():
        sweep(lambda s, soff: (s - jnp.abs(bm_ref[...] + soff), 0.0))

    @pl.when(j == nk - 1)
    def _():
        lv = lam_ref[...]
        lam = (jnp.exp(jnp.sum(lv[0:1] * lv[1:2], axis=-1, keepdims=True))
               - jnp.exp(jnp.sum(lv[2:3] * lv[3:4], axis=-1, keepdims=True)) + lam_init)
        ot = acc_ref[0] / l_ref[0] - lam * (acc_ref[1] / l_ref[1])
        o = ot.T
        r = lax.rsqrt(jnp.mean(o * o, axis=-1, keepdims=True) + RMS_EPS)
        o_ref[0] = (o * r * g_ref[...] * (1.0 - lam_init)).astype(o_ref.dtype)


def _diff_attention(proj, slopes, lamvec, subln_g, lam_init):
    bsz, seq, _ = proj.shape
    tq = _pick(seq, (1024, 512))
    tk = _pick(seq, (1024, 512))
    cq = 256
    nk = seq // tk
    qt = proj[:, :, 1536:2560].reshape(bsz, seq, DIFF_HEADS, 2, HEAD_DIM).transpose(0, 2, 3, 4, 1)
    vt = proj[:, :, 3584:4608].reshape(bsz, seq, DIFF_HEADS, DIFF_V_DIM).transpose(0, 2, 3, 1)
    kh = proj[:, :, 2560:3584].reshape(bsz, seq, 2 * DIFF_HEADS, HEAD_DIM).transpose(0, 2, 1, 3)
    smem = pl.BlockSpec(memory_space=pltpu.SMEM)

    def kspec(off):
        return pl.BlockSpec((1, 1, tk, HEAD_DIM), lambda b, h, i, j: (b, 2 * h + off, j, 0))

    return pl.pallas_call(
        functools.partial(_diff_kernel, tq=tq, tk=tk, cq=cq, nk=nk, lam_init=lam_init),
        grid=(bsz, DIFF_HEADS, seq // tq, nk),
        in_specs=[smem,
                  pl.BlockSpec((1, 1, 2, HEAD_DIM, tq), lambda b, h, i, j: (b, h, 0, 0, i)),
                  kspec(0), kspec(1),
                  pl.BlockSpec((1, 1, DIFF_V_DIM, tk), lambda b, h, i, j: (b, h, 0, j)),
                  pl.BlockSpec((4, HEAD_DIM), lambda b, h, i, j: (0, 0)),
                  pl.BlockSpec((1, DIFF_V_DIM), lambda b, h, i, j: (0, 0))],
        out_specs=pl.BlockSpec((1, tq, DIFF_V_DIM), lambda b, h, i, j: (b, i, h)),
        out_shape=jax.ShapeDtypeStruct((bsz, seq, DIFF_HEADS * DIFF_V_DIM), BF16),
        scratch_shapes=[pltpu.VMEM((DIFF_KEY_SUB, cq), F32),
                        pltpu.VMEM((2, 1, tq), F32), pltpu.VMEM((2, 1, tq), F32),
                        pltpu.VMEM((2, DIFF_V_DIM, tq), F32)],
        compiler_params=_cparams(("parallel", "parallel", "parallel", "arbitrary")),
        name="diff_attn",
    )(slopes, qt, kh, kh, vt, lamvec, subln_g)


def _normrope_kernel(x_ref, cos_ref, sin_ref, gq_ref, gk_ref, o_ref, *, tm):
    j = pl.program_id(2)
    is_q = j < AX_Q_HEADS // 2
    g = jnp.where(is_q, gq_ref[...], gk_ref[...])
    post = jnp.where(is_q, ATTN_SCALE * LOG2E, 1.0).astype(F32)
    lane = lax.broadcasted_iota(jnp.int32, (tm, HEAD_DIM), 1)
    first = (lane % 64) < 32
    cos = cos_ref[...]
    sin = sin_ref[...]
    for hh in range(2):
        x = x_ref[0, :, hh * HEAD_DIM:(hh + 1) * HEAD_DIM].astype(F32)
        xn = x * lax.rsqrt(jnp.mean(x * x, axis=-1, keepdims=True) + RMS_EPS) * g
        partner = jnp.where(first, pltpu.roll(xn, 96, 1), pltpu.roll(xn, 32, 1))
        y = (xn * cos + partner * sin) * post
        o_ref[0, :, hh * HEAD_DIM:(hh + 1) * HEAD_DIM] = y.astype(o_ref.dtype)


def _rope_tables(seq):
    half = HEAD_DIM // 2
    inv = jnp.asarray(1.0 / (ROPE_THETA ** (np.arange(0, half, 2) / half)), dtype=F32)
    t = jnp.arange(seq)
    ang_r = (t // GRID_W).astype(F32)[:, None] * inv[None, :]
    ang_c = (t % GRID_W).astype(F32)[:, None] * inv[None, :]
    cr, sr, cc, sn = jnp.cos(ang_r), jnp.sin(ang_r), jnp.cos(ang_c), jnp.sin(ang_c)
    return (jnp.concatenate([cr, cr, cc, cc], axis=-1), jnp.concatenate([-sr, sr, -sn, sn], axis=-1))


def _normrope(proj, qnorm_g, knorm_g):
    bsz, seq, _ = proj.shape
    tm = _pick(seq, (2048, 1024, 512))
    cos, sin = _rope_tables(seq)
    nw = (AX_Q_HEADS + AX_KV_HEADS) * HEAD_DIM
    return pl.pallas_call(
        functools.partial(_normrope_kernel, tm=tm),
        grid=(bsz, seq // tm, nw // 256),
        in_specs=[pl.BlockSpec((1, tm, 256), lambda b, i, j: (b, i, j)),
                  pl.BlockSpec((tm, HEAD_DIM), lambda b, i, j: (i, 0)),
                  pl.BlockSpec((tm, HEAD_DIM), lambda b, i, j: (i, 0)),
                  pl.BlockSpec((1, HEAD_DIM), lambda b, i, j: (0, 0)),
                  pl.BlockSpec((1, HEAD_DIM), lambda b, i, j: (0, 0))],
        out_specs=pl.BlockSpec((1, tm, 256), lambda b, i, j: (b, i, j)),
        out_shape=jax.ShapeDtypeStruct((bsz, seq, nw), BF16),
        compiler_params=_cparams(("parallel", "parallel", "parallel")),
        name="normrope",
    )(proj, cos, sin, qnorm_g.reshape(1, HEAD_DIM), knorm_g.reshape(1, HEAD_DIM))


def _gqa_kernel(q_ref, k_ref, vt_ref, o_ref, m_ref, l_ref, acc_ref, *, tq, cq, nk):
    j = pl.program_id(3)

    @pl.when(j == 0)
    def _():
        m_ref[...] = jnp.full_like(m_ref, NEG_INF)
        l_ref[...] = jnp.zeros_like(l_ref)
        acc_ref[...] = jnp.zeros_like(acc_ref)

    tk = k_ref.shape[2]
    per = tq // cq
    for c in range(GQA_GROUP * per):
        g, sb = divmod(c, per)
        cols = slice(c * cq, (c + 1) * cq)
        qs = q_ref[0, g, :, sb * cq:(sb + 1) * cq]
        m = m_ref[:, cols]
        l = l_ref[:, cols]
        acc = acc_ref[:, cols]
        for kb in range(tk // GQA_KEY_SUB):
            rows = slice(kb * GQA_KEY_SUB, (kb + 1) * GQA_KEY_SUB)
            s = jnp.dot(k_ref[0, 0, rows, :], qs, preferred_element_type=F32)
            m_new = jnp.maximum(m, jnp.max(s, axis=0, keepdims=True))
            alpha = jnp.exp2(m - m_new)
            p = jnp.exp2(s - m_new)
            l = alpha * l + jnp.sum(p, axis=0, keepdims=True)
            acc = alpha * acc + jnp.dot(vt_ref[0, 0, :, rows], p.astype(BF16), preferred_element_type=F32)
            m = m_new
        m_ref[:, cols] = m
        l_ref[:, cols] = l
        acc_ref[:, cols] = acc

    @pl.when(j == nk - 1)
    def _():
        ot = acc_ref[...] / l_ref[...]
        for g in range(GQA_GROUP):
            o_ref[0, :, g * HEAD_DIM:(g + 1) * HEAD_DIM] = ot[:, g * tq:(g + 1) * tq].T.astype(o_ref.dtype)


def _gqa_attention(qk, proj):
    bsz, seq, _ = qk.shape
    tq = _pick(seq, (512,))
    tk = _pick(seq, (1024, 512))
    cq = 128
    nk = seq // tk
    gw = GQA_GROUP * HEAD_DIM
    qt = qk[:, :, :AX_Q_HEADS * HEAD_DIM].reshape(bsz, seq, AX_Q_HEADS, HEAD_DIM).transpose(0, 2, 3, 1)
    vt = proj[:, :, 1280:1536].reshape(bsz, seq, AX_KV_HEADS, HEAD_DIM).transpose(0, 2, 3, 1)
    kh = qk[:, :, AX_Q_HEADS * HEAD_DIM:].reshape(bsz, seq, AX_KV_HEADS, HEAD_DIM).transpose(0, 2, 1, 3)
    return pl.pallas_call(
        functools.partial(_gqa_kernel, tq=tq, cq=cq, nk=nk),
        grid=(bsz, AX_KV_HEADS, seq // tq, nk),
        in_specs=[pl.BlockSpec((1, GQA_GROUP, HEAD_DIM, tq), lambda b, h, i, j: (b, h, 0, i)),
                  pl.BlockSpec((1, 1, tk, HEAD_DIM), lambda b, h, i, j: (b, h, j, 0)),
                  pl.BlockSpec((1, 1, HEAD_DIM, tk), lambda b, h, i, j: (b, h, 0, j))],
        out_specs=pl.BlockSpec((1, tq, gw), lambda b, h, i, j: (b, i, h)),
        out_shape=jax.ShapeDtypeStruct((bsz, seq, AX_Q_HEADS * HEAD_DIM), BF16),
        scratch_shapes=[pltpu.VMEM((1, GQA_GROUP * tq), F32), pltpu.VMEM((1, GQA_GROUP * tq), F32),
                        pltpu.VMEM((HEAD_DIM, GQA_GROUP * tq), F32)],
        compiler_params=_cparams(("parallel", "parallel", "parallel", "arbitrary")),
        name="gqa_attn",
    )(qt, kh, vt)


def _na_bias_table(rpb):
    qi = np.arange(NA_ROWS)[:, None, None, None]
    qc = np.arange(GRID_W)[None, :, None, None]
    kj = np.arange(2 * NA_ROWS)[None, None, :, None]
    kc = np.arange(GRID_W)[None, None, None, :]
    c_start = np.clip(qc - NA_COLS // 2, 0, GRID_W - NA_COLS)
    col_ok = (kc >= c_start) & (kc < c_start + NA_COLS)
    rel_c = np.clip(kc - qc + NA_COLS - 1, 0, 2 * NA_COLS - 2)
    onehot_c = jnp.asarray(rel_c[0, :, 0, :, None] == np.arange(2 * NA_COLS - 1), dtype=F32)
    tabs = []
    for off, clamp in ((0, np.maximum), (-NA_ROWS // 2, None), (-NA_ROWS, np.minimum)):
        rs = qi - NA_ROWS // 2
        if clamp is not None:
            rs = clamp(rs, 0)
        kr = off + kj
        ok = (kr >= rs) & (kr < rs + NA_ROWS) & col_ok
        rel_r = np.clip(kr - qi + NA_ROWS - 1, 0, 2 * NA_ROWS - 2)[:, 0, :, 0]
        onehot_r = jnp.asarray(rel_r[:, :, None] == np.arange(2 * NA_ROWS - 1), dtype=F32)
        bias = jnp.einsum("ija,hab,cdb->hicjd", onehot_r, rpb.astype(F32), onehot_c, precision=HIGHEST)
        tabs.append(jnp.where(ok[None], bias, NEG_INF).reshape(NA_HEADS, NA_TQ, NA_WIN))
    return jnp.stack(tabs)


def _na_kernel(q_ref, k_ref, v_ref, bias_ref, o_ref, *, rows):
    i = pl.program_id(2)
    ws = jnp.clip(i * NA_ROWS - NA_ROWS // 2, 0, rows - 2 * NA_ROWS)
    start = pl.multiple_of(ws * GRID_W, GRID_W)
    k = k_ref[0, pl.ds(start, NA_WIN), :]
    v = v_ref[0, pl.ds(start, NA_WIN), :]
    s = lax.dot_general(q_ref[0], k, (((1,), (1,)), ((), ())), preferred_element_type=F32) + bias_ref[0, 0]
    m = jnp.max(s, axis=-1, keepdims=True)
    p = jnp.exp(s - m)
    l = jnp.sum(p, axis=-1, keepdims=True)
    o_ref[0] = (jnp.dot(p.astype(BF16), v, preferred_element_type=F32) / l).astype(o_ref.dtype)


def _na_attention(proj, bias_tab):
    bsz, seq, _ = proj.shape
    rows = seq // GRID_W
    nq = seq // NA_TQ
    assert rows >= 3 * NA_ROWS and seq % NA_TQ == 0
    qcol, kcol, vcol = 1536 // HEAD_DIM, 2560 // HEAD_DIM, 3584 // HEAD_DIM

    def case(i):
        return jnp.where(i == 0, 0, jnp.where(i == nq - 1, 2, 1))

    return pl.pallas_call(
        functools.partial(_na_kernel, rows=rows),
        grid=(bsz, NA_HEADS, nq),
        in_specs=[pl.BlockSpec((1, NA_TQ, HEAD_DIM), lambda b, h, i: (b, i, qcol + h)),
                  pl.BlockSpec((1, seq, HEAD_DIM), lambda b, h, i: (b, 0, kcol + h)),
                  pl.BlockSpec((1, seq, HEAD_DIM), lambda b, h, i: (b, 0, vcol + h)),
                  pl.BlockSpec((1, 1, NA_TQ, NA_WIN), lambda b, h, i: (case(i), h, 0, 0))],
        out_specs=pl.BlockSpec((1, NA_TQ, HEAD_DIM), lambda b, h, i: (b, i, h)),
        out_shape=jax.ShapeDtypeStruct((bsz, seq, NA_HEADS * HEAD_DIM), BF16),
        compiler_params=_cparams(("parallel", "parallel", "arbitrary")),
        name="na_attn",
    )(proj, proj, proj, bias_tab)


def _outproj_kernel(oa_ref, ob_ref, wa_ref, wb_ref, x_ref, g_ref, lng_ref, lnb_ref, o_ref):
    m = (jnp.dot(oa_ref[0], wa_ref[...], preferred_element_type=F32)
         + jnp.dot(ob_ref[0], wb_ref[...], preferred_element_type=F32))
    y = DEEPNORM_ALPHA * x_ref[0] + g_ref[0] * m
    o_ref[0] = _layer_norm(y, lng_ref[...], lnb_ref[...])


def _outproj_ln(oa, ob, w_out, x, gate, ln_g, ln_b):
    bsz, seq, d = x.shape
    half = oa.shape[-1]
    tm = _pick(seq, (512, 256))
    vec = pl.BlockSpec((1, d), lambda b, i: (0, 0))
    return pl.pallas_call(
        _outproj_kernel,
        grid=(bsz, seq // tm),
        in_specs=[pl.BlockSpec((1, tm, half), lambda b, i: (b, i, 0)),
                  pl.BlockSpec((1, tm, half), lambda b, i: (b, i, 0)),
                  pl.BlockSpec((half, d), lambda b, i: (0, 0)),
                  pl.BlockSpec((half, d), lambda b, i: (1, 0)),
                  pl.BlockSpec((1, tm, d), lambda b, i: (b, i, 0)),
                  pl.BlockSpec((1, 1, d), lambda b, i: (b, 0, 0)),
                  vec, vec],
        out_specs=pl.BlockSpec((1, tm, d), lambda b, i: (b, i, 0)),
        out_shape=jax.ShapeDtypeStruct((bsz, seq, d), F32),
        compiler_params=_cparams(("parallel", "parallel")),
        name="outproj_ln",
    )(oa, ob, w_out, w_out, x, gate, ln_g.reshape(1, d), ln_b.reshape(1, d))


def _route(logits):
    lane = lax.broadcasted_iota(jnp.int32, logits.shape, 1)
    big = jnp.int32(2 ** 30)
    is_g = lane < N_GROUPS
    mg = jnp.max(jnp.where(is_g, logits, -jnp.inf), axis=-1, keepdims=True)
    g_idx = jnp.min(jnp.where(is_g & (logits == mg), lane, big), axis=-1, keepdims=True)
    p_g = 1.0 / jnp.sum(jnp.where(is_g, jnp.exp(logits - mg), 0.0), axis=-1, keepdims=True)
    lo = N_GROUPS + EXPERTS_PER_GROUP * g_idx
    in_grp = (lane >= lo) & (lane < lo + EXPERTS_PER_GROUP)
    v1 = jnp.max(jnp.where(in_grp, logits, -jnp.inf), axis=-1, keepdims=True)
    i1 = jnp.min(jnp.where(in_grp & (logits == v1), lane, big), axis=-1, keepdims=True)
    rest = in_grp & (lane != i1)
    v2 = jnp.max(jnp.where(rest, logits, -jnp.inf), axis=-1, keepdims=True)
    i2 = jnp.min(jnp.where(rest & (logits == v2), lane, big), axis=-1, keepdims=True)
    e2 = jnp.exp(v2 - v1)
    w1 = p_g / (1.0 + e2)
    w2 = p_g * e2 / (1.0 + e2)
    gates = jnp.where(lane == i1, w1, 0.0) + jnp.where(lane == i2, w2, 0.0)
    return jnp.where(lane == 0, g_idx.astype(F32), gates)


def _moe_route_kernel(x_ref, sc_ref, sh_ref, wr_ref, br_ref, h_ref, gate_ref):
    h = x_ref[0] * (1.0 + sc_ref[0]) + sh_ref[0]
    h_ref[0] = h.astype(BF16)
    logits = jnp.dot(h, wr_ref[...], precision=HIGHEST, preferred_element_type=F32) + br_ref[...]
    gate_ref[0] = _route(logits)


def _moe_route(x, sc, sh, w_router, b_router):
    bsz, seq, d = x.shape
    tm = _pick(seq, (512, 256))
    mod = pl.BlockSpec((1, 1, d), lambda b, i: (b, 0, 0))
    return pl.pallas_call(
        _moe_route_kernel,
        grid=(bsz, seq // tm),
        in_specs=[pl.BlockSpec((1, tm, d), lambda b, i: (b, i, 0)), mod, mod,
                  pl.BlockSpec((d, ROUTER_LANES), lambda b, i: (0, 0)),
                  pl.BlockSpec((1, ROUTER_LANES), lambda b, i: (0, 0))],
        out_specs=[pl.BlockSpec((1, tm, d), lambda b, i: (b, i, 0)),
                   pl.BlockSpec((1, tm, ROUTER_LANES), lambda b, i: (b, i, 0))],
        out_shape=[jax.ShapeDtypeStruct((bsz, seq, d), BF16),
                   jax.ShapeDtypeStruct((bsz, seq, ROUTER_LANES), F32)],
        compiler_params=_cparams(("parallel", "parallel")),
        name="moe_route",
    )(x, sc, sh, w_router, b_router)


def _moe_expert_kernel(h_ref, gate_ref, wg_ref, wu_ref, wd_ref, o_ref, pmt_ref, hs_ref, gs_ref, ys_ref, bounds_ref,
                       *, tm, chunk):
    e = pl.program_id(2)

    @pl.when(e == 0)
    def _():
        rec = gate_ref[0]
        lane = lax.broadcasted_iota(jnp.int32, (tm, ROUTER_LANES), 1)
        onehot = jnp.where((lane < N_GROUPS) & (lane == rec[:, 0:1].astype(jnp.int32)), 1.0, 0.0)
        below = (lax.broadcasted_iota(jnp.int32, (tm, tm), 1)
                 < lax.broadcasted_iota(jnp.int32, (tm, tm), 0)).astype(BF16)
        rank = jnp.dot(below, onehot.astype(BF16), preferred_element_type=F32)
        cnt = jnp.sum(onehot, axis=0, keepdims=True)
        padded = jnp.floor((cnt + (SORT_ALIGN - 1)) * (1.0 / SORT_ALIGN)) * SORT_ALIGN
        lane1 = lax.broadcasted_iota(jnp.int32, (1, ROUTER_LANES), 1)
        starts = jnp.zeros((1, ROUTER_LANES), F32)
        for g in range(N_GROUPS):
            first = jnp.sum(jnp.where(lane1 < g, padded, 0.0))
            size = jnp.sum(jnp.where(lane1 == g, cnt, 0.0))
            starts = starts + jnp.where(lane1 == g, first, 0.0)
            bounds_ref[2 * g] = first.astype(jnp.int32)
            bounds_ref[2 * g + 1] = (size.astype(jnp.int32) + chunk - 1) // chunk
        pos = jnp.sum(onehot * (rank + starts), axis=1, keepdims=True).astype(jnp.int32)
        pmt = (lax.broadcasted_iota(jnp.int32, pmt_ref.shape, 1) == pos).astype(BF16)
        pmt_ref[...] = pmt
        dn = (((0,), (0,)), ((), ()))
        hs_ref[...] = lax.dot_general(pmt, h_ref[0], dn, preferred_element_type=F32).astype(BF16)
        gs = jnp.zeros(gs_ref.shape, F32)
        for piece in _split3(rec):
            gs = gs + lax.dot_general(pmt, piece, dn, preferred_element_type=F32)
        gs_ref[...] = gs
        ys_ref[...] = jnp.zeros_like(ys_ref)

    grp = e // EXPERTS_PER_GROUP
    lane_c = lax.broadcasted_iota(jnp.int32, (chunk, ROUTER_LANES), 1)
    first_row = bounds_ref[2 * grp]

    def body(c, carry):
        rows = pl.ds(pl.multiple_of(first_row + c * chunk, SORT_ALIGN), chunk)
        hb = hs_ref[rows, :]
        a = jnp.dot(hb, wg_ref[0], preferred_element_type=F32)
        u = jnp.dot(hb, wu_ref[0], preferred_element_type=F32)
        ge = jnp.sum(jnp.where(lane_c == e + N_GROUPS, gs_ref[rows, :], 0.0), axis=-1, keepdims=True)
        act = (_silu(a) * u * ge).astype(BF16)
        ys_ref[rows, :] += jnp.dot(act, wd_ref[0], preferred_element_type=F32)
        return carry

    lax.fori_loop(0, bounds_ref[2 * grp + 1], body, 0)

    @pl.when(e == N_EXPERTS - 1)
    def _():
        o_ref[0] = jnp.dot(pmt_ref[...], ys_ref[...].astype(BF16), preferred_element_type=F32).astype(o_ref.dtype)


def _moe_experts(h, rec, w_gate, w_up, w_down):
    bsz, seq, d = h.shape
    tm = _pick(seq, (1024, 512))
    chunk = 128
    ps = tm + 2 * chunk
    assert N_GROUPS * (SORT_ALIGN - 1) + chunk <= 2 * chunk
    return pl.pallas_call(
        functools.partial(_moe_expert_kernel, tm=tm, chunk=chunk),
        grid=(bsz, seq // tm, N_EXPERTS),
        in_specs=[pl.BlockSpec((1, tm, d), lambda b, i, e: (b, i, 0)),
                  pl.BlockSpec((1, tm, ROUTER_LANES), lambda b, i, e: (b, i, 0)),
                  pl.BlockSpec((1, d, D_EXPERT), lambda b, i, e: (e, 0, 0)),
                  pl.BlockSpec((1, d, D_EXPERT), lambda b, i, e: (e, 0, 0)),
                  pl.BlockSpec((1, D_EXPERT, d), lambda b, i, e: (e, 0, 0))],
        out_specs=pl.BlockSpec((1, tm, d), lambda b, i, e: (b, i, 0)),
        out_shape=jax.ShapeDtypeStruct((bsz, seq, d), BF16),
        scratch_shapes=[pltpu.VMEM((tm, ps), BF16), pltpu.VMEM((ps, d), BF16), pltpu.VMEM((ps, ROUTER_LANES), F32),
                        pltpu.VMEM((ps, d), F32), pltpu.SMEM((2 * N_GROUPS,), jnp.int32)],
        compiler_params=_cparams(("parallel", "parallel", "arbitrary")),
        name="moe_experts",
    )(h, rec, w_gate, w_up, w_down)


def _resid_ln_kernel(x_ref, f_ref, g_ref, lng_ref, lnb_ref, o_ref):
    y = DEEPNORM_ALPHA * x_ref[0] + g_ref[0] * f_ref[0].astype(F32)
    o_ref[0] = _layer_norm(y, lng_ref[...], lnb_ref[...])


def _resid_ln(x, f, gate, ln_g, ln_b):
    bsz, seq, d = x.shape
    tm = _pick(seq, (512, 256))
    vec = pl.BlockSpec((1, d), lambda b, i: (0, 0))
    tile = pl.BlockSpec((1, tm, d), lambda b, i: (b, i, 0))
    return pl.pallas_call(
        _resid_ln_kernel,
        grid=(bsz, seq // tm),
        in_specs=[tile, tile, pl.BlockSpec((1, 1, d), lambda b, i: (b, 0, 0)), vec, vec],
        out_specs=tile,
        out_shape=jax.ShapeDtypeStruct((bsz, seq, d), F32),
        compiler_params=_cparams(("parallel", "parallel")),
        name="resid_ln",
    )(x, f, gate, ln_g.reshape(1, d), ln_b.reshape(1, d))


def _moe_ln(x, sc, sh, gate, w_router, b_router, w_gate, w_up, w_down, ln_g, ln_b):
    h, rec = _moe_route(x, sc, sh, w_router, b_router)
    f = _moe_experts(h, rec, w_gate, w_up, w_down)
    return _resid_ln(x, f, gate, ln_g, ln_b)


def _alibi_slopes(n):
    return jnp.asarray(2.0 ** (-8.0 * np.arange(1, n + 1) / n), dtype=F32)


def _colscale(q_ranges):
    cs = np.ones((1, PROJ_N), np.float32)
    for lo, hi, scale in q_ranges:
        cs[:, lo:hi] = scale
    return jnp.asarray(cs)


def _prep_params(p):
    q = dict(p)
    for name in ("w_in_ab", "w_out_ab", "w_in_cd", "w_out_cd", "w_gate", "w_up", "w_down"):
        q[name] = p[name].astype(BF16)
    pad = ROUTER_LANES - N_GROUPS - N_EXPERTS
    q["w_router"] = jnp.pad(jnp.concatenate([p["w_rg"], p["w_re"]], axis=-1), ((0, 0), (0, 0), (0, pad)))
    q["b_router"] = jnp.pad(jnp.concatenate([p["b_rg"], p["b_re"]], axis=-1), ((0, 0), (0, pad)))[:, None, :]
    q["na_bias"] = [_na_bias_table(p["rpb_d"][i]) for i in range(p["rpb_d"].shape[0])]
    return q


def _trunk(x, ada, p):
    for l in range(DEPTH):
        sh1, sc1, g1, sh2, sc2, g2 = jnp.split(ada[l][:, None, :], 6, axis=-1)
        i = l // 2
        if l % 2 == 0:
            lam_init = 0.8 - 0.6 * math.exp(-0.3 * l)
            proj = _proj(x, sc1, sh1, p["w_in_ab"][i], _colscale(((0, 1024, ATTN_SCALE), (1536, 2560, ATTN_SCALE * LOG2E))))
            oa = _win_attention(proj, p["sink_a"][i], _alibi_slopes(WIN_Q_HEADS))
            lamvec = jnp.stack([p["lam_q1"][i], p["lam_k1"][i], p["lam_q2"][i], p["lam_k2"][i]])
            ob = _diff_attention(proj, _alibi_slopes(DIFF_HEADS), lamvec, p["subln_g"][i][None, :], lam_init)
            w_out = p["w_out_ab"][i]
        else:
            proj = _proj(x, sc1, sh1, p["w_in_cd"][i], _colscale(((1536, 2560, ATTN_SCALE),)))
            qk = _normrope(proj, p["qnorm_c"][i], p["knorm_c"][i])
            oa = _gqa_attention(qk, proj)
            ob = _na_attention(proj, p["na_bias"][i])
            w_out = p["w_out_cd"][i]
        x = _outproj_ln(oa, ob, w_out, x, g1, p["ln_g"][l, 0], p["ln_b"][l, 0])
        x = _moe_ln(x, sc2, sh2, g2, p["w_router"][l], p["b_router"][l],
                    p["w_gate"][l], p["w_up"][l], p["w_down"][l], p["ln_g"][l, 1], p["ln_b"][l, 1])
    return x


def kernel(x_prompt, x_sample, c_prompt, c_sample, w_ada, b_ada, ln_g, ln_b, w_in_ab, w_out_ab, sink_a,
           lam_q1, lam_k1, lam_q2, lam_k2, subln_g, w_in_cd, w_out_cd, qnorm_c, knorm_c, rpb_d,
           w_rg, b_rg, w_re, b_re, w_gate, w_up, w_down):
    p = _prep_params(dict(
        ln_g=ln_g, ln_b=ln_b, w_in_ab=w_in_ab, w_out_ab=w_out_ab, sink_a=sink_a,
        lam_q1=lam_q1, lam_k1=lam_k1, lam_q2=lam_q2, lam_k2=lam_k2, subln_g=subln_g,
        w_in_cd=w_in_cd, w_out_cd=w_out_cd, qnorm_c=qnorm_c, knorm_c=knorm_c, rpb_d=rpb_d,
        w_rg=w_rg, b_rg=b_rg, w_re=w_re, b_re=b_re, w_gate=w_gate, w_up=w_up, w_down=w_down))
    nb_p, nb_s = c_prompt.shape[0], c_sample.shape[0]
    c_all = jnp.concatenate([c_prompt, c_sample, jnp.zeros((8 - nb_p - nb_s, D_MODEL), F32)], axis=0)
    ada = _ada(c_all, w_ada, b_ada)
    y_prompt = _trunk(x_prompt, ada[:, :nb_p], p)
    y_sample = _trunk(x_sample, ada[:, nb_p:nb_p + nb_s], p)
    return (y_prompt, y_sample)
```

```python
import functools
import math

import numpy as np
import jax
import jax.numpy as jnp
from jax import lax
from jax.experimental import pallas as pl
from jax.experimental.pallas import tpu as pltpu

F32 = jnp.float32
BF16 = jnp.bfloat16
HIGHEST = lax.Precision.HIGHEST

D_MODEL = 2048
DEPTH = 2
HEAD_DIM = 128
WINDOW = 128
GRID_W = 64
WIN_Q_HEADS = 8
WIN_KV_HEADS = 2
GQA_GROUP = 4
DIFF_HEADS = 4
DIFF_V_DIM = 256
AX_Q_HEADS = 8
AX_KV_HEADS = 2
NA_HEADS = 8
NA_ROWS = 8
NA_COLS = 16
ROPE_THETA = 10000.0
N_GROUPS = 4
EXPERTS_PER_GROUP = 4
N_EXPERTS = 16
D_EXPERT = 512
LN_EPS = 1e-5
RMS_EPS = 1e-6
NEG_INF = -1e30
DEEPNORM_ALPHA = (2 * DEPTH) ** 0.25
PROJ_N = 4608
ATTN_SCALE = HEAD_DIM ** -0.5
LOG2E = math.log2(math.e)
LANES = 128
ROUTER_LANES = LANES
SORT_ALIGN = 16
VMEM_LIMIT = 56 * 1024 * 1024

GQA_KEY_SUB = 256
DIFF_KEY_SUB = 512
NA_TQ = NA_ROWS * GRID_W
NA_WIN = 2 * NA_ROWS * GRID_W


def _cparams(sem):
    return pltpu.CompilerParams(dimension_semantics=sem, vmem_limit_bytes=VMEM_LIMIT)


def _pick(n, cands):
    for c in cands:
        if n % c == 0:
            return c
    raise ValueError(f"no tile in {cands} divides {n}")


def _layer_norm(y, g, b):
    mu = jnp.mean(y, axis=-1, keepdims=True)
    yc = y - mu
    var = jnp.mean(yc * yc, axis=-1, keepdims=True)
    return yc * lax.rsqrt(var + LN_EPS) * g + b


def _silu(a):
    return a / (1.0 + jnp.exp(-a))


def _ada_kernel(c_ref, w_ref, b_ref, o_ref):
    a = _silu(c_ref[...])
    o_ref[0] = jnp.dot(a, w_ref[0], precision=HIGHEST, preferred_element_type=F32) + b_ref[0]


def _ada(c_all, w_ada, b_ada):
    n = 6 * D_MODEL
    tn = 1024
    return pl.pallas_call(
        _ada_kernel,
        grid=(DEPTH, n // tn),
        in_specs=[pl.BlockSpec((8, D_MODEL), lambda l, j: (0, 0)),
                  pl.BlockSpec((1, D_MODEL, tn), lambda l, j: (l, 0, j)),
                  pl.BlockSpec((1, 1, tn), lambda l, j: (l, 0, j))],
        out_specs=pl.BlockSpec((1, 8, tn), lambda l, j: (l, 0, j)),
        out_shape=jax.ShapeDtypeStruct((DEPTH, 8, n), F32),
        compiler_params=_cparams(("parallel", "parallel")),
        name="ada",
    )(c_all, w_ada, b_ada.reshape(DEPTH, 1, n))


def _proj_kernel(x_ref, sc_ref, sh_ref, w_ref, cs_ref, o_ref, h_ref):
    @pl.when(pl.program_id(2) == 0)
    def _():
        h_ref[...] = (x_ref[0] * (1.0 + sc_ref[0]) + sh_ref[0]).astype(BF16)

    acc = jnp.dot(h_ref[...], w_ref[...], preferred_element_type=F32)
    o_ref[0] = (acc * cs_ref[...]).astype(o_ref.dtype)


def _proj(x, sc, sh, w, colscale):
    bsz, seq, d = x.shape
    n = w.shape[1]
    tm = _pick(seq, (512, 256))
    tn = 2304
    return pl.pallas_call(
        _proj_kernel,
        grid=(bsz, seq // tm, n // tn),
        in_specs=[pl.BlockSpec((1, tm, d), lambda b, i, j: (b, i, 0)),
                  pl.BlockSpec((1, 1, d), lambda b, i, j: (b, 0, 0)),
                  pl.BlockSpec((1, 1, d), lambda b, i, j: (b, 0, 0)),
                  pl.BlockSpec((d, tn), lambda b, i, j: (0, j)),
                  pl.BlockSpec((1, tn), lambda b, i, j: (0, j))],
        out_specs=pl.BlockSpec((1, tm, tn), lambda b, i, j: (b, i, j)),
        out_shape=jax.ShapeDtypeStruct((bsz, seq, n), BF16),
        scratch_shapes=[pltpu.VMEM((tm, d), BF16)],
        compiler_params=_cparams(("parallel", "parallel", "arbitrary")),
        name="proj",
    )(x, sc, sh, w, colscale)


def _win_kernel(sink_ref, slope_ref, q_ref, kp_ref, km_ref, kn_ref, vp_ref, vm_ref, vn_ref, o_ref, *, tq, seq):
    i = pl.program_id(1)
    kvh = pl.program_id(2)
    k = jnp.concatenate([kp_ref[0], km_ref[0], kn_ref[0]], axis=0)
    v = jnp.concatenate([vp_ref[0], vm_ref[0], vn_ref[0]], axis=0)
    w = tq + 2 * WINDOW
    qrel = lax.broadcasted_iota(jnp.int32, (tq, w), 0)
    krel = lax.broadcasted_iota(jnp.int32, (tq, w), 1) - WINDOW
    dist = jnp.abs(qrel - krel)
    kabs = krel + i * tq
    valid = (dist <= WINDOW) & (kabs >= 0) & (kabs < seq)
    distf = dist.astype(F32)
    for g in range(GQA_GROUP):
        h = kvh * GQA_GROUP + g
        qg = q_ref[0, :, g * HEAD_DIM:(g + 1) * HEAD_DIM]
        s = lax.dot_general(qg, k, (((1,), (1,)), ((), ())), preferred_element_type=F32)
        s = jnp.where(valid, s - slope_ref[h] * distf, NEG_INF)
        snk = sink_ref[h]
        m = jnp.maximum(jnp.max(s, axis=-1, keepdims=True), snk)
        p = jnp.exp(s - m)
        l = jnp.sum(p, axis=-1, keepdims=True) + jnp.exp(snk - m)
        o = jnp.dot(p.astype(BF16), v, preferred_element_type=F32) / l
        o_ref[0, :, g * HEAD_DIM:(g + 1) * HEAD_DIM] = o.astype(o_ref.dtype)


def _win_attention(proj, sink, slopes):
    bsz, seq, _ = proj.shape
    tq = 256
    r = tq // WINDOW
    nb = seq // WINDOW
    kcol = WIN_Q_HEADS
    vcol = WIN_Q_HEADS + WIN_KV_HEADS

    def prev(col):
        return pl.BlockSpec((1, WINDOW, HEAD_DIM), lambda b, i, k: (b, jnp.maximum(i * r - 1, 0), col + k))

    def main(col):
        return pl.BlockSpec((1, tq, HEAD_DIM), lambda b, i, k: (b, i, col + k))

    def nxt(col):
        return pl.BlockSpec((1, WINDOW, HEAD_DIM), lambda b, i, k: (b, jnp.minimum((i + 1) * r, nb - 1), col + k))

    smem = pl.BlockSpec(memory_space=pltpu.SMEM)
    return pl.pallas_call(
        functools.partial(_win_kernel, tq=tq, seq=seq),
        grid=(bsz, seq // tq, WIN_KV_HEADS),
        in_specs=[smem, smem,
                  pl.BlockSpec((1, tq, GQA_GROUP * HEAD_DIM), lambda b, i, k: (b, i, k)),
                  prev(kcol), main(kcol), nxt(kcol), prev(vcol), main(vcol), nxt(vcol)],
        out_specs=pl.BlockSpec((1, tq, GQA_GROUP * HEAD_DIM), lambda b, i, k: (b, i, k)),
        out_shape=jax.ShapeDtypeStruct((bsz, seq, WIN_Q_HEADS * HEAD_DIM), BF16),
        compiler_params=_cparams(("parallel", "parallel", "parallel")),
        name="win_attn",
    )(sink, slopes, proj, proj, proj, proj, proj, proj, proj)


def _split3(x):
    p1 = x.astype(BF16)
    r1 = x - p1.astype(F32)
    p2 = r1.astype(BF16)
    return p1, p2, (r1 - p2.astype(F32)).astype(BF16)


def _diff_kernel(slope_ref, q_ref, k1_ref, k2_ref, vt_ref, lam_ref, g_ref, o_ref, bm_ref, m_ref, l_ref, acc_ref,
                 *, tq, tk, cq, nk, lam_init):
    h = pl.program_id(1)
    i = pl.program_id(2)
    j = pl.program_id(3)
    sigma = slope_ref[h] * LOG2E

    @pl.when(j == 0)
    def _():
        m_ref[...] = jnp.full_like(m_ref, NEG_INF)
        l_ref[...] = jnp.zeros_like(l_ref)
        acc_ref[...] = jnp.zeros_like(acc_ref)
        dbase = (lax.broadcasted_iota(jnp.int32, (DIFF_KEY_SUB, cq), 1)
                 - lax.broadcasted_iota(jnp.int32, (DIFF_KEY_SUB, cq), 0)).astype(F32)
        bm_ref[...] = dbase * sigma

    def sweep(frame):
        for c in range(tq // cq):
            cols = slice(c * cq, (c + 1) * cq)
            for mi, k_ref in enumerate((k1_ref, k2_ref)):
                qs = q_ref[0, 0, mi, :, cols]
                m = m_ref[mi, :, cols]
                l = l_ref[mi, :, cols]
                acc = acc_ref[mi, :, cols]
                for kb in range(tk // DIFF_KEY_SUB):
                    rows = slice(kb * DIFF_KEY_SUB, (kb + 1) * DIFF_KEY_SUB)
                    soff = sigma * (i * tq + c * cq - j * tk - kb * DIFF_KEY_SUB).astype(F32)
                    s = jnp.dot(k_ref[0, 0, rows, :], qs, preferred_element_type=F32)
                    t, shift = frame(s, soff)
                    m_new = jnp.maximum(m, jnp.max(t, axis=0, keepdims=True) + shift)
                    alpha = jnp.exp2(m - m_new)
                    p = jnp.exp2(t - (m_new - shift))
                    l = alpha * l + jnp.sum(p, axis=0, keepdims=True)
                    acc = alpha * acc + jnp.dot(vt_ref[0, 0, :, rows], p.astype(BF16), preferred_element_type=F32)
                    m = m_new
                m_ref[mi, :, cols] = m
                l_ref[mi, :, cols] = l
                acc_ref[mi, :, cols] = acc

    @pl.when(i * tq >= (j + 1) * tk)
    def _():
        sweep(lambda s, soff: (s - bm_ref[...], -soff))

    @pl.when((i + 1) * tq <= j * tk)
    def _():
        sweep(lambda s, soff: (s + bm_ref[...], soff))

    @pl.when((i * tq < (j + 1) * tk) & ((i + 1) * tq > j * tk))
    def _():
        sweep(lambda s, soff: (s - jnp.abs(bm_ref[...] + soff), 0.0))

    @pl.when(j == nk - 1)
    def _():
        lv = lam_ref[...]
        lam = (jnp.exp(jnp.sum(lv[0:1] * lv[1:2], axis=-1, keepdims=True))
               - jnp.exp(jnp.sum(lv[2:3] * lv[3:4], axis=-1, keepdims=True)) + lam_init)
        ot = acc_ref[0] / l_ref[0] - lam * (acc_ref[1] / l_ref[1])
        o = ot.T
        r = lax.rsqrt(jnp.mean(o * o, axis=-1, keepdims=True) + RMS_EPS)
        o_ref[0] = (o * r * g_ref[...] * (1.0 - lam_init)).astype(o_ref.dtype)


def _diff_attention(proj, slopes, lamvec, subln_g, lam_init):
    bsz, seq, _ = proj.shape
    tq = _pick(seq, (1024, 512))
    tk = _pick(seq, (2048, 1024, 512))
    cq = 256
    nk = seq // tk
    qt = proj[:, :, 1536:2560].reshape(bsz, seq, DIFF_HEADS, 2, HEAD_DIM).transpose(0, 2, 3, 4, 1)
    vt = proj[:, :, 3584:4608].reshape(bsz, seq, DIFF_HEADS, DIFF_V_DIM).transpose(0, 2, 3, 1)
    kh = proj[:, :, 2560:3584].reshape(bsz, seq, 2 * DIFF_HEADS, HEAD_DIM).transpose(0, 2, 1, 3)
    smem = pl.BlockSpec(memory_space=pltpu.SMEM)

    def kspec(off):
        return pl.BlockSpec((1, 1, tk, HEAD_DIM), lambda b, h, i, j: (b, 2 * h + off, j, 0))

    return pl.pallas_call(
        functools.partial(_diff_kernel, tq=tq, tk=tk, cq=cq, nk=nk, lam_init=lam_init),
        grid=(bsz, DIFF_HEADS, seq // tq, nk),
        in_specs=[smem,
                  pl.BlockSpec((1, 1, 2, HEAD_DIM, tq), lambda b, h, i, j: (b, h, 0, 0, i)),
                  kspec(0), kspec(1),
                  pl.BlockSpec((1, 1, DIFF_V_DIM, tk), lambda b, h, i, j: (b, h, 0, j)),
                  pl.BlockSpec((4, HEAD_DIM), lambda b, h, i, j: (0, 0)),
                  pl.BlockSpec((1, DIFF_V_DIM), lambda b, h, i, j: (0, 0))],
        out_specs=pl.BlockSpec((1, tq, DIFF_V_DIM), lambda b, h, i, j: (b, i, h)),
        out_shape=jax.ShapeDtypeStruct((bsz, seq, DIFF_HEADS * DIFF_V_DIM), BF16),
        scratch_shapes=[pltpu.VMEM((DIFF_KEY_SUB, cq), F32),
                        pltpu.VMEM((2, 1, tq), F32), pltpu.VMEM((2, 1, tq), F32),
                        pltpu.VMEM((2, DIFF_V_DIM, tq), F32)],
        compiler_params=_cparams(("parallel", "parallel", "parallel", "arbitrary")),
        name="diff_attn",
    )(slopes, qt, kh, kh, vt, lamvec, subln_g)


def _normrope_kernel(x_ref, cos_ref, sin_ref, gq_ref, gk_ref, o_ref, *, tm):
    j = pl.program_id(2)
    is_q = j < AX_Q_HEADS // 2
    g = jnp.where(is_q, gq_ref[...], gk_ref[...])
    post = jnp.where(is_q, ATTN_SCALE * LOG2E, 1.0).astype(F32)
    lane = lax.broadcasted_iota(jnp.int32, (tm, HEAD_DIM), 1)
    first = (lane % 64) < 32
    cos = cos_ref[...]
    sin = sin_ref[...]
    for hh in range(2):
        x = x_ref[0, :, hh * HEAD_DIM:(hh + 1) * HEAD_DIM].astype(F32)
        xn = x * lax.rsqrt(jnp.mean(x * x, axis=-1, keepdims=True) + RMS_EPS) * g
        partner = jnp.where(first, pltpu.roll(xn, 96, 1), pltpu.roll(xn, 32, 1))
        y = (xn * cos + partner * sin) * post
        o_ref[0, :, hh * HEAD_DIM:(hh + 1) * HEAD_DIM] = y.astype(o_ref.dtype)


def _rope_tables(seq):
    half = HEAD_DIM // 2
    inv = jnp.asarray(1.0 / (ROPE_THETA ** (np.arange(0, half, 2) / half)), dtype=F32)
    t = jnp.arange(seq)
    ang_r = (t // GRID_W).astype(F32)[:, None] * inv[None, :]
    ang_c = (t % GRID_W).astype(F32)[:, None] * inv[None, :]
    cr, sr, cc, sn = jnp.cos(ang_r), jnp.sin(ang_r), jnp.cos(ang_c), jnp.sin(ang_c)
    return (jnp.concatenate([cr, cr, cc, cc], axis=-1), jnp.concatenate([-sr, sr, -sn, sn], axis=-1))


def _normrope(proj, qnorm_g, knorm_g):
    bsz, seq, _ = proj.shape
    tm = _pick(seq, (2048, 1024, 512))
    cos, sin = _rope_tables(seq)
    nw = (AX_Q_HEADS + AX_KV_HEADS) * HEAD_DIM
    return pl.pallas_call(
        functools.partial(_normrope_kernel, tm=tm),
        grid=(bsz, seq // tm, nw // 256),
        in_specs=[pl.BlockSpec((1, tm, 256), lambda b, i, j: (b, i, j)),
                  pl.BlockSpec((tm, HEAD_DIM), lambda b, i, j: (i, 0)),
                  pl.BlockSpec((tm, HEAD_DIM), lambda b, i, j: (i, 0)),
                  pl.BlockSpec((1, HEAD_DIM), lambda b, i, j: (0, 0)),
                  pl.BlockSpec((1, HEAD_DIM), lambda b, i, j: (0, 0))],
        out_specs=pl.BlockSpec((1, tm, 256), lambda b, i, j: (b, i, j)),
        out_shape=jax.ShapeDtypeStruct((bsz, seq, nw), BF16),
        compiler_params=_cparams(("parallel", "parallel", "parallel")),
        name="normrope",
    )(proj, cos, sin, qnorm_g.reshape(1, HEAD_DIM), knorm_g.reshape(1, HEAD_DIM))


def _gqa_kernel(q_ref, k_ref, vt_ref, o_ref, m_ref, l_ref, acc_ref, *, tq, cq, nk):
    j = pl.program_id(3)

    @pl.when(j == 0)
    def _():
        m_ref[...] = jnp.full_like(m_ref, NEG_INF)
        l_ref[...] = jnp.zeros_like(l_ref)
        acc_ref[...] = jnp.zeros_like(acc_ref)

    tk = k_ref.shape[2]
    per = tq // cq
    for c in range(GQA_GROUP * per):
        g, sb = divmod(c, per)
        cols = slice(c * cq, (c + 1) * cq)
        qs = q_ref[0, g, :, sb * cq:(sb + 1) * cq]
        m = m_ref[:, cols]
        l = l_ref[:, cols]
        acc = acc_ref[:, cols]
        for kb in range(tk // GQA_KEY_SUB):
            rows = slice(kb * GQA_KEY_SUB, (kb + 1) * GQA_KEY_SUB)
            s = jnp.dot(k_ref[0, 0, rows, :], qs, preferred_element_type=F32)
            m_new = jnp.maximum(m, jnp.max(s, axis=0, keepdims=True))
            alpha = jnp.exp2(m - m_new)
            p = jnp.exp2(s - m_new)
            l = alpha * l + jnp.sum(p, axis=0, keepdims=True)
            acc = alpha * acc + jnp.dot(vt_ref[0, 0, :, rows], p.astype(BF16), preferred_element_type=F32)
            m = m_new
        m_ref[:, cols] = m
        l_ref[:, cols] = l
        acc_ref[:, cols] = acc

    @pl.when(j == nk - 1)
    def _():
        ot = acc_ref[...] / l_ref[...]
        for g in range(GQA_GROUP):
            o_ref[0, :, g * HEAD_DIM:(g + 1) * HEAD_DIM] = ot[:, g * tq:(g + 1) * tq].T.astype(o_ref.dtype)


def _gqa_attention(qk, proj):
    bsz, seq, _ = qk.shape
    tq = _pick(seq, (1024, 512))
    tk = _pick(seq, (1024, 512))
    cq = 128
    nk = seq // tk
    gw = GQA_GROUP * HEAD_DIM
    qt = qk[:, :, :AX_Q_HEADS * HEAD_DIM].reshape(bsz, seq, AX_Q_HEADS, HEAD_DIM).transpose(0, 2, 3, 1)
    vt = proj[:, :, 1280:1536].reshape(bsz, seq, AX_KV_HEADS, HEAD_DIM).transpose(0, 2, 3, 1)
    kh = qk[:, :, AX_Q_HEADS * HEAD_DIM:].reshape(bsz, seq, AX_KV_HEADS, HEAD_DIM).transpose(0, 2, 1, 3)
    return pl.pallas_call(
        functools.partial(_gqa_kernel, tq=tq, cq=cq, nk=nk),
        grid=(bsz, AX_KV_HEADS, seq // tq, nk),
        in_specs=[pl.BlockSpec((1, GQA_GROUP, HEAD_DIM, tq), lambda b, h, i, j: (b, h, 0, i)),
                  pl.BlockSpec((1, 1, tk, HEAD_DIM), lambda b, h, i, j: (b, h, j, 0)),
                  pl.BlockSpec((1, 1, HEAD_DIM, tk), lambda b, h, i, j: (b, h, 0, j))],
        out_specs=pl.BlockSpec((1, tq, gw), lambda b, h, i, j: (b, i, h)),
        out_shape=jax.ShapeDtypeStruct((bsz, seq, AX_Q_HEADS * HEAD_DIM), BF16),
        scratch_shapes=[pltpu.VMEM((1, GQA_GROUP * tq), F32), pltpu.VMEM((1, GQA_GROUP * tq), F32),
                        pltpu.VMEM((HEAD_DIM, GQA_GROUP * tq), F32)],
        compiler_params=_cparams(("parallel", "parallel", "parallel", "arbitrary")),
        name="gqa_attn",
    )(qt, kh, vt)


def _na_bias_table(rpb):
    qi = np.arange(NA_ROWS)[:, None, None, None]
    qc = np.arange(GRID_W)[None, :, None, None]
    kj = np.arange(2 * NA_ROWS)[None, None, :, None]
    kc = np.arange(GRID_W)[None, None, None, :]
    c_start = np.clip(qc - NA_COLS // 2, 0, GRID_W - NA_COLS)
    col_ok = (kc >= c_start) & (kc < c_start + NA_COLS)
    rel_c = np.clip(kc - qc + NA_COLS - 1, 0, 2 * NA_COLS - 2)
    onehot_c = jnp.asarray(rel_c[0, :, 0, :, None] == np.arange(2 * NA_COLS - 1), dtype=F32)
    tabs = []
    for off, clamp in ((0, np.maximum), (-NA_ROWS // 2, None), (-NA_ROWS, np.minimum)):
        rs = qi - NA_ROWS // 2
        if clamp is not None:
            rs = clamp(rs, 0)
        kr = off + kj
        ok = (kr >= rs) & (kr < rs + NA_ROWS) & col_ok
        rel_r = np.clip(kr - qi + NA_ROWS - 1, 0, 2 * NA_ROWS - 2)[:, 0, :, 0]
        onehot_r = jnp.asarray(rel_r[:, :, None] == np.arange(2 * NA_ROWS - 1), dtype=F32)
        bias = jnp.einsum("ija,hab,cdb->hicjd", onehot_r, rpb.astype(F32), onehot_c, precision=HIGHEST)
        tabs.append(jnp.where(ok[None], bias, NEG_INF).reshape(NA_HEADS, NA_TQ, NA_WIN))
    return jnp.stack(tabs)


def _na_kernel(q_ref, k_ref, v_ref, bias_ref, o_ref, *, rows):
    i = pl.program_id(2)
    ws = jnp.clip(i * NA_ROWS - NA_ROWS // 2, 0, rows - 2 * NA_ROWS)
    start = pl.multiple_of(ws * GRID_W, GRID_W)
    k = k_ref[0, pl.ds(start, NA_WIN), :]
    v = v_ref[0, pl.ds(start, NA_WIN), :]
    s = lax.dot_general(q_ref[0], k, (((1,), (1,)), ((), ())), preferred_element_type=F32) + bias_ref[0, 0]
    m = jnp.max(s, axis=-1, keepdims=True)
    p = jnp.exp(s - m)
    l = jnp.sum(p, axis=-1, keepdims=True)
    o_ref[0] = (jnp.dot(p.astype(BF16), v, preferred_element_type=F32) / l).astype(o_ref.dtype)


def _na_attention(proj, bias_tab):
    bsz, seq, _ = proj.shape
    rows = seq // GRID_W
    nq = seq // NA_TQ
    assert rows >= 3 * NA_ROWS and seq % NA_TQ == 0
    qcol, kcol, vcol = 1536 // HEAD_DIM, 2560 // HEAD_DIM, 3584 // HEAD_DIM

    def case(i):
        return jnp.where(i == 0, 0, jnp.where(i == nq - 1, 2, 1))

    return pl.pallas_call(
        functools.partial(_na_kernel, rows=rows),
        grid=(bsz, NA_HEADS, nq),
        in_specs=[pl.BlockSpec((1, NA_TQ, HEAD_DIM), lambda b, h, i: (b, i, qcol + h)),
                  pl.BlockSpec((1, seq, HEAD_DIM), lambda b, h, i: (b, 0, kcol + h)),
                  pl.BlockSpec((1, seq, HEAD_DIM), lambda b, h, i: (b, 0, vcol + h)),
                  pl.BlockSpec((1, 1, NA_TQ, NA_WIN), lambda b, h, i: (case(i), h, 0, 0))],
        out_specs=pl.BlockSpec((1, NA_TQ, HEAD_DIM), lambda b, h, i: (b, i, h)),
        out_shape=jax.ShapeDtypeStruct((bsz, seq, NA_HEADS * HEAD_DIM), BF16),
        compiler_params=_cparams(("parallel", "parallel", "arbitrary")),
        name="na_attn",
    )(proj, proj, proj, bias_tab)


def _outproj_kernel(oa_ref, ob_ref, wa_ref, wb_ref, x_ref, g_ref, lng_ref, lnb_ref, o_ref):
    m = (jnp.dot(oa_ref[0], wa_ref[...], preferred_element_type=F32)
         + jnp.dot(ob_ref[0], wb_ref[...], preferred_element_type=F32))
    y = DEEPNORM_ALPHA * x_ref[0] + g_ref[0] * m
    o_ref[0] = _layer_norm(y, lng_ref[...], lnb_ref[...])


def _outproj_ln(oa, ob, w_out, x, gate, ln_g, ln_b):
    bsz, seq, d = x.shape
    half = oa.shape[-1]
    tm = _pick(seq, (512, 256))
    vec = pl.BlockSpec((1, d), lambda b, i: (0, 0))
    return pl.pallas_call(
        _outproj_kernel,
        grid=(bsz, seq // tm),
        in_specs=[pl.BlockSpec((1, tm, half), lambda b, i: (b, i, 0)),
                  pl.BlockSpec((1, tm, half), lambda b, i: (b, i, 0)),
                  pl.BlockSpec((half, d), lambda b, i: (0, 0)),
                  pl.BlockSpec((half, d), lambda b, i: (1, 0)),
                  pl.BlockSpec((1, tm, d), lambda b, i: (b, i, 0)),
                  pl.BlockSpec((1, 1, d), lambda b, i: (b, 0, 0)),
                  vec, vec],
        out_specs=pl.BlockSpec((1, tm, d), lambda b, i: (b, i, 0)),
        out_shape=jax.ShapeDtypeStruct((bsz, seq, d), F32),
        compiler_params=_cparams(("parallel", "parallel")),
        name="outproj_ln",
    )(oa, ob, w_out, w_out, x, gate, ln_g.reshape(1, d), ln_b.reshape(1, d))


def _route(logits):
    lane = lax.broadcasted_iota(jnp.int32, logits.shape, 1)
    big = jnp.int32(2 ** 30)
    is_g = lane < N_GROUPS
    mg = jnp.max(jnp.where(is_g, logits, -jnp.inf), axis=-1, keepdims=True)
    g_idx = jnp.min(jnp.where(is_g & (logits == mg), lane, big), axis=-1, keepdims=True)
    p_g = 1.0 / jnp.sum(jnp.where(is_g, jnp.exp(logits - mg), 0.0), axis=-1, keepdims=True)
    lo = N_GROUPS + EXPERTS_PER_GROUP * g_idx
    in_grp = (lane >= lo) & (lane < lo + EXPERTS_PER_GROUP)
    v1 = jnp.max(jnp.where(in_grp, logits, -jnp.inf), axis=-1, keepdims=True)
    i1 = jnp.min(jnp.where(in_grp & (logits == v1), lane, big), axis=-1, keepdims=True)
    rest = in_grp & (lane != i1)
    v2 = jnp.max(jnp.where(rest, logits, -jnp.inf), axis=-1, keepdims=True)
    i2 = jnp.min(jnp.where(rest & (logits == v2), lane, big), axis=-1, keepdims=True)
    e2 = jnp.exp(v2 - v1)
    w1 = p_g / (1.0 + e2)
    w2 = p_g * e2 / (1.0 + e2)
    gates = jnp.where(lane == i1, w1, 0.0) + jnp.where(lane == i2, w2, 0.0)
    return jnp.where(lane == 0, g_idx.astype(F32), gates)


def _moe_route_kernel(x_ref, sc_ref, sh_ref, wr_ref, br_ref, h_ref, gate_ref):
    h = x_ref[0] * (1.0 + sc_ref[0]) + sh_ref[0]
    h_ref[0] = h.astype(BF16)
    logits = jnp.dot(h, wr_ref[...], precision=HIGHEST, preferred_element_type=F32) + br_ref[...]
    gate_ref[0] = _route(logits)


def _moe_route(x, sc, sh, w_router, b_router):
    bsz, seq, d = x.shape
    tm = _pick(seq, (512, 256))
    mod = pl.BlockSpec((1, 1, d), lambda b, i: (b, 0, 0))
    return pl.pallas_call(
        _moe_route_kernel,
        grid=(bsz, seq // tm),
        in_specs=[pl.BlockSpec((1, tm, d), lambda b, i: (b, i, 0)), mod, mod,
                  pl.BlockSpec((d, ROUTER_LANES), lambda b, i: (0, 0)),
                  pl.BlockSpec((1, ROUTER_LANES), lambda b, i: (0, 0))],
        out_specs=[pl.BlockSpec((1, tm, d), lambda b, i: (b, i, 0)),
                   pl.BlockSpec((1, tm, ROUTER_LANES), lambda b, i: (b, i, 0))],
        out_shape=[jax.ShapeDtypeStruct((bsz, seq, d), BF16),
                   jax.ShapeDtypeStruct((bsz, seq, ROUTER_LANES), F32)],
        compiler_params=_cparams(("parallel", "parallel")),
        name="moe_route",
    )(x, sc, sh, w_router, b_router)


def _moe_expert_kernel(h_ref, gate_ref, wg_ref, wu_ref, wd_ref, o_ref, pmt_ref, hs_ref, gs_ref, ys_ref, bounds_ref,
                       *, tm, chunk):
    e = pl.program_id(2)

    @pl.when(e == 0)
    def _():
        rec = gate_ref[0]
        lane = lax.broadcasted_iota(jnp.int32, (tm, ROUTER_LANES), 1)
        onehot = jnp.where((lane < N_GROUPS) & (lane == rec[:, 0:1].astype(jnp.int32)), 1.0, 0.0)
        below = (lax.broadcasted_iota(jnp.int32, (tm, tm), 1)
                 < lax.broadcasted_iota(jnp.int32, (tm, tm), 0)).astype(BF16)
        rank = jnp.dot(below, onehot.astype(BF16), preferred_element_type=F32)
        cnt = jnp.sum(onehot, axis=0, keepdims=True)
        padded = jnp.floor((cnt + (SORT_ALIGN - 1)) * (1.0 / SORT_ALIGN)) * SORT_ALIGN
        lane1 = lax.broadcasted_iota(jnp.int32, (1, ROUTER_LANES), 1)
        starts = jnp.zeros((1, ROUTER_LANES), F32)
        for g in range(N_GROUPS):
            first = jnp.sum(jnp.where(lane1 < g, padded, 0.0))
            size = jnp.sum(jnp.where(lane1 == g, cnt, 0.0))
            starts = starts + jnp.where(lane1 == g, first, 0.0)
            bounds_ref[2 * g] = first.astype(jnp.int32)
            bounds_ref[2 * g + 1] = (size.astype(jnp.int32) + chunk - 1) // chunk
        pos = jnp.sum(onehot * (rank + starts), axis=1, keepdims=True).astype(jnp.int32)
        pmt = (lax.broadcasted_iota(jnp.int32, pmt_ref.shape, 1) == pos).astype(BF16)
        pmt_ref[...] = pmt
        dn = (((0,), (0,)), ((), ()))
        hs_ref[...] = lax.dot_general(pmt, h_ref[0], dn, preferred_element_type=F32).astype(BF16)
        gs = jnp.zeros(gs_ref.shape, F32)
        for piece in _split3(rec):
            gs = gs + lax.dot_general(pmt, piece, dn, preferred_element_type=F32)
        gs_ref[...] = gs
        ys_ref[...] = jnp.zeros_like(ys_ref)

    grp = e // EXPERTS_PER_GROUP
    lane_c = lax.broadcasted_iota(jnp.int32, (chunk, ROUTER_LANES), 1)
    first_row = bounds_ref[2 * grp]

    def body(c, carry):
        rows = pl.ds(pl.multiple_of(first_row + c * chunk, SORT_ALIGN), chunk)
        hb = hs_ref[rows, :]
        a = jnp.dot(hb, wg_ref[0], preferred_element_type=F32)
        u = jnp.dot(hb, wu_ref[0], preferred_element_type=F32)
        ge = jnp.sum(jnp.where(lane_c == e + N_GROUPS, gs_ref[rows, :], 0.0), axis=-1, keepdims=True)
        act = (_silu(a) * u * ge).astype(BF16)
        ys_ref[rows, :] += jnp.dot(act, wd_ref[0], preferred_element_type=F32)
        return carry

    lax.fori_loop(0, bounds_ref[2 * grp + 1], body, 0)

    @pl.when(e == N_EXPERTS - 1)
    def _():
        o_ref[0] = jnp.dot(pmt_ref[...], ys_ref[...].astype(BF16), preferred_element_type=F32).astype(o_ref.dtype)


def _moe_experts(h, rec, w_gate, w_up, w_down):
    bsz, seq, d = h.shape
    tm = _pick(seq, (1024, 512))
    chunk = 128
    ps = tm + 2 * chunk
    assert N_GROUPS * (SORT_ALIGN - 1) + chunk <= 2 * chunk
    return pl.pallas_call(
        functools.partial(_moe_expert_kernel, tm=tm, chunk=chunk),
        grid=(bsz, seq // tm, N_EXPERTS),
        in_specs=[pl.BlockSpec((1, tm, d), lambda b, i, e: (b, i, 0)),
                  pl.BlockSpec((1, tm, ROUTER_LANES), lambda b, i, e: (b, i, 0)),
                  pl.BlockSpec((1, d, D_EXPERT), lambda b, i, e: (e, 0, 0)),
                  pl.BlockSpec((1, d, D_EXPERT), lambda b, i, e: (e, 0, 0)),
                  pl.BlockSpec((1, D_EXPERT, d), lambda b, i, e: (e, 0, 0))],
        out_specs=pl.BlockSpec((1, tm, d), lambda b, i, e: (b, i, 0)),
        out_shape=jax.ShapeDtypeStruct((bsz, seq, d), BF16),
        scratch_shapes=[pltpu.VMEM((tm, ps), BF16), pltpu.VMEM((ps, d), BF16), pltpu.VMEM((ps, ROUTER_LANES), F32),
                        pltpu.VMEM((ps, d), F32), pltpu.SMEM((2 * N_GROUPS,), jnp.int32)],
        compiler_params=_cparams(("parallel", "parallel", "arbitrary")),
        name="moe_experts",
    )(h, rec, w_gate, w_up, w_down)


def _resid_ln_kernel(x_ref, f_ref, g_ref, lng_ref, lnb_ref, o_ref):
    y = DEEPNORM_ALPHA * x_ref[0] + g_ref[0] * f_ref[0].astype(F32)
    o_ref[0] = _layer_norm(y, lng_ref[...], lnb_ref[...])


def _resid_ln(x, f, gate, ln_g, ln_b):
    bsz, seq, d = x.shape
    tm = _pick(seq, (512, 256))
    vec = pl.BlockSpec((1, d), lambda b, i: (0, 0))
    tile = pl.BlockSpec((1, tm, d), lambda b, i: (b, i, 0))
    return pl.pallas_call(
        _resid_ln_kernel,
        grid=(bsz, seq // tm),
        in_specs=[tile, tile, pl.BlockSpec((1, 1, d), lambda b, i: (b, 0, 0)), vec, vec],
        out_specs=tile,
        out_shape=jax.ShapeDtypeStruct((bsz, seq, d), F32),
        compiler_params=_cparams(("parallel", "parallel")),
        name="resid_ln",
    )(x, f, gate, ln_g.reshape(1, d), ln_b.reshape(1, d))


def _moe_ln(x, sc, sh, gate, w_router, b_router, w_gate, w_up, w_down, ln_g, ln_b):
    h, rec = _moe_route(x, sc, sh, w_router, b_router)
    f = _moe_experts(h, rec, w_gate, w_up, w_down)
    return _resid_ln(x, f, gate, ln_g, ln_b)


def _alibi_slopes(n):
    return jnp.asarray(2.0 ** (-8.0 * np.arange(1, n + 1) / n), dtype=F32)


def _colscale(q_ranges):
    cs = np.ones((1, PROJ_N), np.float32)
    for lo, hi, scale in q_ranges:
        cs[:, lo:hi] = scale
    return jnp.asarray(cs)


def _prep_params(p):
    q = dict(p)
    for name in ("w_in_ab", "w_out_ab", "w_in_cd", "w_out_cd", "w_gate", "w_up", "w_down"):
        q[name] = p[name].astype(BF16)
    pad = ROUTER_LANES - N_GROUPS - N_EXPERTS
    q["w_router"] = jnp.pad(jnp.concatenate([p["w_rg"], p["w_re"]], axis=-1), ((0, 0), (0, 0), (0, pad)))
    q["b_router"] = jnp.pad(jnp.concatenate([p["b_rg"], p["b_re"]], axis=-1), ((0, 0), (0, pad)))[:, None, :]
    q["na_bias"] = [_na_bias_table(p["rpb_d"][i]) for i in range(p["rpb_d"].shape[0])]
    return q


def _trunk(x, ada, p):
    for l in range(DEPTH):
        sh1, sc1, g1, sh2, sc2, g2 = jnp.split(ada[l][:, None, :], 6, axis=-1)
        i = l // 2
        if l % 2 == 0:
            lam_init = 0.8 - 0.6 * math.exp(-0.3 * l)
            proj = _proj(x, sc1, sh1, p["w_in_ab"][i], _colscale(((0, 1024, ATTN_SCALE), (1536, 2560, ATTN_SCALE * LOG2E))))
            oa = _win_attention(proj, p["sink_a"][i], _alibi_slopes(WIN_Q_HEADS))
            lamvec = jnp.stack([p["lam_q1"][i], p["lam_k1"][i], p["lam_q2"][i], p["lam_k2"][i]])
            ob = _diff_attention(proj, _alibi_slopes(DIFF_HEADS), lamvec, p["subln_g"][i][None, :], lam_init)
            w_out = p["w_out_ab"][i]
        else:
            proj = _proj(x, sc1, sh1, p["w_in_cd"][i], _colscale(((1536, 2560, ATTN_SCALE),)))
            qk = _normrope(proj, p["qnorm_c"][i], p["knorm_c"][i])
            oa = _gqa_attention(qk, proj)
            ob = _na_attention(proj, p["na_bias"][i])
            w_out = p["w_out_cd"][i]
        x = _outproj_ln(oa, ob, w_out, x, g1, p["ln_g"][l, 0], p["ln_b"][l, 0])
        x = _moe_ln(x, sc2, sh2, g2, p["w_router"][l], p["b_router"][l],
                    p["w_gate"][l], p["w_up"][l], p["w_down"][l], p["ln_g"][l, 1], p["ln_b"][l, 1])
    return x


def kernel(x_prompt, x_sample, c_prompt, c_sample, w_ada, b_ada, ln_g, ln_b, w_in_ab, w_out_ab, sink_a,
           lam_q1, lam_k1, lam_q2, lam_k2, subln_g, w_in_cd, w_out_cd, qnorm_c, knorm_c, rpb_d,
           w_rg, b_rg, w_re, b_re, w_gate, w_up, w_down):
    p = _prep_params(dict(
        ln_g=ln_g, ln_b=ln_b, w_in_ab=w_in_ab, w_out_ab=w_out_ab, sink_a=sink_a,
        lam_q1=lam_q1, lam_k1=lam_k1, lam_q2=lam_q2, lam_k2=lam_k2, subln_g=subln_g,
        w_in_cd=w_in_cd, w_out_cd=w_out_cd, qnorm_c=qnorm_c, knorm_c=knorm_c, rpb_d=rpb_d,
        w_rg=w_rg, b_rg=b_rg, w_re=w_re, b_re=b_re, w_gate=w_gate, w_up=w_up, w_down=w_down))
    nb_p, nb_s = c_prompt.shape[0], c_sample.shape[0]
    c_all = jnp.concatenate([c_prompt, c_sample, jnp.zeros((8 - nb_p - nb_s, D_MODEL), F32)], axis=0)
    ada = _ada(c_all, w_ada, b_ada)
    y_prompt = _trunk(x_prompt, ada[:, :nb_p], p)
    y_sample = _trunk(x_sample, ada[:, nb_p:nb_p + nb_s], p)
    return (y_prompt, y_sample)
```

```python
import functools
import math

import numpy as np
import jax
import jax.numpy as jnp
from jax import lax
from jax.experimental import pallas as pl
from jax.experimental.pallas import tpu as pltpu

F32 = jnp.float32
BF16 = jnp.bfloat16
HIGHEST = lax.Precision.HIGHEST

D_MODEL = 2048
DEPTH = 2
HEAD_DIM = 128
WINDOW = 128
GRID_W = 64
WIN_Q_HEADS = 8
WIN_KV_HEADS = 2
GQA_GROUP = 4
DIFF_HEADS = 4
DIFF_V_DIM = 256
AX_Q_HEADS = 8
AX_KV_HEADS = 2
NA_HEADS = 8
NA_ROWS = 8
NA_COLS = 16
ROPE_THETA = 10000.0
N_GROUPS = 4
EXPERTS_PER_GROUP = 4
N_EXPERTS = 16
D_EXPERT = 512
LN_EPS = 1e-5
RMS_EPS = 1e-6
NEG_INF = -1e30
DEEPNORM_ALPHA = (2 * DEPTH) ** 0.25
PROJ_N = 4608
ATTN_SCALE = HEAD_DIM ** -0.5
LOG2E = math.log2(math.e)
LANES = 128
ROUTER_LANES = LANES
SORT_ALIGN = 16
VMEM_LIMIT = 56 * 1024 * 1024

GQA_KEY_SUB = 256
DIFF_KEY_SUB = 512
NA_TQ = NA_ROWS * GRID_W
NA_WIN = 2 * NA_ROWS * GRID_W


def _cparams(sem):
    return pltpu.CompilerParams(dimension_semantics=sem, vmem_limit_bytes=VMEM_LIMIT)


def _pick(n, cands):
    for c in cands:
        if n % c == 0:
            return c
    raise ValueError(f"no tile in {cands} divides {n}")


def _layer_norm(y, g, b):
    mu = jnp.mean(y, axis=-1, keepdims=True)
    yc = y - mu
    var = jnp.mean(yc * yc, axis=-1, keepdims=True)
    return yc * lax.rsqrt(var + LN_EPS) * g + b


def _silu(a):
    return a / (1.0 + jnp.exp(-a))


def _ada_kernel(c_ref, w_ref, b_ref, o_ref):
    a = _silu(c_ref[...])
    o_ref[0] = jnp.dot(a, w_ref[0], precision=HIGHEST, preferred_element_type=F32) + b_ref[0]


def _ada(c_all, w_ada, b_ada):
    n = 6 * D_MODEL
    tn = 1024
    return pl.pallas_call(
        _ada_kernel,
        grid=(DEPTH, n // tn),
        in_specs=[pl.BlockSpec((8, D_MODEL), lambda l, j: (0, 0)),
                  pl.BlockSpec((1, D_MODEL, tn), lambda l, j: (l, 0, j)),
                  pl.BlockSpec((1, 1, tn), lambda l, j: (l, 0, j))],
        out_specs=pl.BlockSpec((1, 8, tn), lambda l, j: (l, 0, j)),
        out_shape=jax.ShapeDtypeStruct((DEPTH, 8, n), F32),
        compiler_params=_cparams(("parallel", "parallel")),
        name="ada",
    )(c_all, w_ada, b_ada.reshape(DEPTH, 1, n))


def _proj_kernel(x_ref, sc_ref, sh_ref, w_ref, cs_ref, o_ref, h_ref):
    @pl.when(pl.program_id(2) == 0)
    def _():
        h_ref[...] = (x_ref[0] * (1.0 + sc_ref[0]) + sh_ref[0]).astype(BF16)

    acc = jnp.dot(h_ref[...], w_ref[...], preferred_element_type=F32)
    o_ref[0] = (acc * cs_ref[...]).astype(o_ref.dtype)


def _proj(x, sc, sh, w, colscale):
    bsz, seq, d = x.shape
    n = w.shape[1]
    tm = _pick(seq, (512, 256))
    tn = 2304
    return pl.pallas_call(
        _proj_kernel,
        grid=(bsz, seq // tm, n // tn),
        in_specs=[pl.BlockSpec((1, tm, d), lambda b, i, j: (b, i, 0)),
                  pl.BlockSpec((1, 1, d), lambda b, i, j: (b, 0, 0)),
                  pl.BlockSpec((1, 1, d), lambda b, i, j: (b, 0, 0)),
                  pl.BlockSpec((d, tn), lambda b, i, j: (0, j)),
                  pl.BlockSpec((1, tn), lambda b, i, j: (0, j))],
        out_specs=pl.BlockSpec((1, tm, tn), lambda b, i, j: (b, i, j)),
        out_shape=jax.ShapeDtypeStruct((bsz, seq, n), BF16),
        scratch_shapes=[pltpu.VMEM((tm, d), BF16)],
        compiler_params=_cparams(("parallel", "parallel", "arbitrary")),
        name="proj",
    )(x, sc, sh, w, colscale)


def _win_kernel(sink_ref, slope_ref, q_ref, kp_ref, km_ref, kn_ref, vp_ref, vm_ref, vn_ref, o_ref, *, tq, seq):
    i = pl.program_id(1)
    kvh = pl.program_id(2)
    k = jnp.concatenate([kp_ref[0], km_ref[0], kn_ref[0]], axis=0)
    v = jnp.concatenate([vp_ref[0], vm_ref[0], vn_ref[0]], axis=0)
    w = tq + 2 * WINDOW
    qrel = lax.broadcasted_iota(jnp.int32, (tq, w), 0)
    krel = lax.broadcasted_iota(jnp.int32, (tq, w), 1) - WINDOW
    dist = jnp.abs(qrel - krel)
    kabs = krel + i * tq
    valid = (dist <= WINDOW) & (kabs >= 0) & (kabs < seq)
    distf = dist.astype(F32)
    for g in range(GQA_GROUP):
        h = kvh * GQA_GROUP + g
        qg = q_ref[0, :, g * HEAD_DIM:(g + 1) * HEAD_DIM]
        s = lax.dot_general(qg, k, (((1,), (1,)), ((), ())), preferred_element_type=F32)
        s = jnp.where(valid, s - slope_ref[h] * distf, NEG_INF)
        snk = sink_ref[h]
        m = jnp.maximum(jnp.max(s, axis=-1, keepdims=True), snk)
        p = jnp.exp(s - m)
        l = jnp.sum(p, axis=-1, keepdims=True) + jnp.exp(snk - m)
        o = jnp.dot(p.astype(BF16), v, preferred_element_type=F32) / l
        o_ref[0, :, g * HEAD_DIM:(g + 1) * HEAD_DIM] = o.astype(o_ref.dtype)


def _win_attention(proj, sink, slopes):
    bsz, seq, _ = proj.shape
    tq = 256
    r = tq // WINDOW
    nb = seq // WINDOW
    kcol = WIN_Q_HEADS
    vcol = WIN_Q_HEADS + WIN_KV_HEADS

    def prev(col):
        return pl.BlockSpec((1, WINDOW, HEAD_DIM), lambda b, i, k: (b, jnp.maximum(i * r - 1, 0), col + k))

    def main(col):
        return pl.BlockSpec((1, tq, HEAD_DIM), lambda b, i, k: (b, i, col + k))

    def nxt(col):
        return pl.BlockSpec((1, WINDOW, HEAD_DIM), lambda b, i, k: (b, jnp.minimum((i + 1) * r, nb - 1), col + k))

    smem = pl.BlockSpec(memory_space=pltpu.SMEM)
    return pl.pallas_call(
        functools.partial(_win_kernel, tq=tq, seq=seq),
        grid=(bsz, seq // tq, WIN_KV_HEADS),
        in_specs=[smem, smem,
                  pl.BlockSpec((1, tq, GQA_GROUP * HEAD_DIM), lambda b, i, k: (b, i, k)),
                  prev(kcol), main(kcol), nxt(kcol), prev(vcol), main(vcol), nxt(vcol)],
        out_specs=pl.BlockSpec((1, tq, GQA_GROUP * HEAD_DIM), lambda b, i, k: (b, i, k)),
        out_shape=jax.ShapeDtypeStruct((bsz, seq, WIN_Q_HEADS * HEAD_DIM), BF16),
        compiler_params=_cparams(("parallel", "parallel", "parallel")),
        name="win_attn",
    )(sink, slopes, proj, proj, proj, proj, proj, proj, proj)


def _split3(x):
    p1 = x.astype(BF16)
    r1 = x - p1.astype(F32)
    p2 = r1.astype(BF16)
    return p1, p2, (r1 - p2.astype(F32)).astype(BF16)


def _diff_kernel(slope_ref, q_ref, k1_ref, k2_ref, vt_ref, lam_ref, g_ref, o_ref, bm_ref, m_ref, l_ref, acc_ref,
                 *, tq, tk, cq, nk, lam_init):
    h = pl.program_id(1)
    i = pl.program_id(2)
    j = pl.program_id(3)
    sigma = slope_ref[h] * LOG2E

    @pl.when(j == 0)
    def _():
        m_ref[...] = jnp.full_like(m_ref, NEG_INF)
        l_ref[...] = jnp.zeros_like(l_ref)
        acc_ref[...] = jnp.zeros_like(acc_ref)
        dbase = (lax.broadcasted_iota(jnp.int32, (DIFF_KEY_SUB, cq), 1)
                 - lax.broadcasted_iota(jnp.int32, (DIFF_KEY_SUB, cq), 0)).astype(F32)
        bm_ref[...] = dbase * sigma

    def sweep(frame):
        for c in range(tq // cq):
            cols = slice(c * cq, (c + 1) * cq)
            for mi, k_ref in enumerate((k1_ref, k2_ref)):
                qs = q_ref[0, 0, mi, :, cols]
                m = m_ref[mi, :, cols]
                l = l_ref[mi, :, cols]
                acc = acc_ref[mi, :, cols]
                for kb in range(tk // DIFF_KEY_SUB):
                    rows = slice(kb * DIFF_KEY_SUB, (kb + 1) * DIFF_KEY_SUB)
                    soff = sigma * (i * tq + c * cq - j * tk - kb * DIFF_KEY_SUB).astype(F32)
                    s = jnp.dot(k_ref[0, 0, rows, :], qs, preferred_element_type=F32)
                    t, shift = frame(s, soff)
                    m_new = jnp.maximum(m, jnp.max(t, axis=0, keepdims=True) + shift)
                    alpha = jnp.exp2(m - m_new)
                    p = jnp.exp2(t - (m_new - shift))
                    l = alpha * l + jnp.sum(p, axis=0, keepdims=True)
                    acc = alpha * acc + jnp.dot(vt_ref[0, 0, :, rows], p.astype(BF16), preferred_element_type=F32)
                    m = m_new
                m_ref[mi, :, cols] = m
                l_ref[mi, :, cols] = l
                acc_ref[mi, :, cols] = acc

    @pl.when(i * tq >= (j + 1) * tk)
    def _():
        sweep(lambda s, soff: (s - bm_ref[...], -soff))

    @pl.when((i + 1) * tq <= j * tk)
    def _():
        sweep(lambda s, soff: (s + bm_ref[...], soff))

    @pl.when((i * tq < (j + 1) * tk) & ((i + 1) * tq > j * tk))
    def _():
        sweep(lambda s, soff: (s - jnp.abs(bm_ref[...] + soff), 0.0))

    @pl.when(j == nk - 1)
    def _():
        lv = lam_ref[...]
        lam = (jnp.exp(jnp.sum(lv[0:1] * lv[1:2], axis=-1, keepdims=True))
               - jnp.exp(jnp.sum(lv[2:3] * lv[3:4], axis=-1, keepdims=True)) + lam_init)
        ot = acc_ref[0] / l_ref[0] - lam * (acc_ref[1] / l_ref[1])
        o = ot.T
        r = lax.rsqrt(jnp.mean(o * o, axis=-1, keepdims=True) + RMS_EPS)
        o_ref[0] = (o * r * g_ref[...] * (1.0 - lam_init)).astype(o_ref.dtype)


def _diff_attention(proj, slopes, lamvec, subln_g, lam_init):
    bsz, seq, _ = proj.shape
    tq = _pick(seq, (1024, 512))
    tk = _pick(seq, (2048, 1024, 512))
    cq = 256
    nk = seq // tk
    qt = proj[:, :, 1536:2560].reshape(bsz, seq, DIFF_HEADS, 2, HEAD_DIM).transpose(0, 2, 3, 4, 1)
    vt = proj[:, :, 3584:4608].reshape(bsz, seq, DIFF_HEADS, DIFF_V_DIM).transpose(0, 2, 3, 1)
    kh = proj[:, :, 2560:3584].reshape(bsz, seq, 2 * DIFF_HEADS, HEAD_DIM).transpose(0, 2, 1, 3)
    smem = pl.BlockSpec(memory_space=pltpu.SMEM)

    def kspec(off):
        return pl.BlockSpec((1, 1, tk, HEAD_DIM), lambda b, h, i, j: (b, 2 * h + off, j, 0))

    return pl.pallas_call(
        functools.partial(_diff_kernel, tq=tq, tk=tk, cq=cq, nk=nk, lam_init=lam_init),
        grid=(bsz, DIFF_HEADS, seq // tq, nk),
        in_specs=[smem,
                  pl.BlockSpec((1, 1, 2, HEAD_DIM, tq), lambda b, h, i, j: (b, h, 0, 0, i)),
                  kspec(0), kspec(1),
                  pl.BlockSpec((1, 1, DIFF_V_DIM, tk), lambda b, h, i, j: (b, h, 0, j)),
                  pl.BlockSpec((4, HEAD_DIM), lambda b, h, i, j: (0, 0)),
                  pl.BlockSpec((1, DIFF_V_DIM), lambda b, h, i, j: (0, 0))],
        out_specs=pl.BlockSpec((1, tq, DIFF_V_DIM), lambda b, h, i, j: (b, i, h)),
        out_shape=jax.ShapeDtypeStruct((bsz, seq, DIFF_HEADS * DIFF_V_DIM), BF16),
        scratch_shapes=[pltpu.VMEM((DIFF_KEY_SUB, cq), F32),
                        pltpu.VMEM((2, 1, tq), F32), pltpu.VMEM((2, 1, tq), F32),
                        pltpu.VMEM((2, DIFF_V_DIM, tq), F32)],
        compiler_params=_cparams(("parallel", "parallel", "parallel", "arbitrary")),
        name="diff_attn",
    )(slopes, qt, kh, kh, vt, lamvec, subln_g)


def _normrope_kernel(x_ref, cos_ref, sin_ref, gq_ref, gk_ref, o_ref, *, tm):
    j = pl.program_id(2)
    is_q = j < AX_Q_HEADS // 2
    g = jnp.where(is_q, gq_ref[...], gk_ref[...])
    post = jnp.where(is_q, ATTN_SCALE * LOG2E, 1.0).astype(F32)
    lane = lax.broadcasted_iota(jnp.int32, (tm, HEAD_DIM), 1)
    first = (lane % 64) < 32
    cos = cos_ref[...]
    sin = sin_ref[...]
    for hh in range(2):
        x = x_ref[0, :, hh * HEAD_DIM:(hh + 1) * HEAD_DIM].astype(F32)
        xn = x * lax.rsqrt(jnp.mean(x * x, axis=-1, keepdims=True) + RMS_EPS) * g
        partner = jnp.where(first, pltpu.roll(xn, 96, 1), pltpu.roll(xn, 32, 1))
        y = (xn * cos + partner * sin) * post
        o_ref[0, :, hh * HEAD_DIM:(hh + 1) * HEAD_DIM] = y.astype(o_ref.dtype)


def _rope_tables(seq):
    half = HEAD_DIM // 2
    inv = jnp.asarray(1.0 / (ROPE_THETA ** (np.arange(0, half, 2) / half)), dtype=F32)
    t = jnp.arange(seq)
    ang_r = (t // GRID_W).astype(F32)[:, None] * inv[None, :]
    ang_c = (t % GRID_W).astype(F32)[:, None] * inv[None, :]
    cr, sr, cc, sn = jnp.cos(ang_r), jnp.sin(ang_r), jnp.cos(ang_c), jnp.sin(ang_c)
    return (jnp.concatenate([cr, cr, cc, cc], axis=-1), jnp.concatenate([-sr, sr, -sn, sn], axis=-1))


def _normrope(proj, qnorm_g, knorm_g):
    bsz, seq, _ = proj.shape
    tm = _pick(seq, (2048, 1024, 512))
    cos, sin = _rope_tables(seq)
    nw = (AX_Q_HEADS + AX_KV_HEADS) * HEAD_DIM
    return pl.pallas_call(
        functools.partial(_normrope_kernel, tm=tm),
        grid=(bsz, seq // tm, nw // 256),
        in_specs=[pl.BlockSpec((1, tm, 256), lambda b, i, j: (b, i, j)),
                  pl.BlockSpec((tm, HEAD_DIM), lambda b, i, j: (i, 0)),
                  pl.BlockSpec((tm, HEAD_DIM), lambda b, i, j: (i, 0)),
                  pl.BlockSpec((1, HEAD_DIM), lambda b, i, j: (0, 0)),
                  pl.BlockSpec((1, HEAD_DIM), lambda b, i, j: (0, 0))],
        out_specs=pl.BlockSpec((1, tm, 256), lambda b, i, j: (b, i, j)),
        out_shape=jax.ShapeDtypeStruct((bsz, seq, nw), BF16),
        compiler_params=_cparams(("parallel", "parallel", "parallel")),
        name="normrope",
    )(proj, cos, sin, qnorm_g.reshape(1, HEAD_DIM), knorm_g.reshape(1, HEAD_DIM))


def _gqa_kernel(q_ref, k_ref, vt_ref, o_ref, m_ref, l_ref, acc_ref, *, tq, cq, nk):
    j = pl.program_id(3)

    @pl.when(j == 0)
    def _():
        m_ref[...] = jnp.full_like(m_ref, NEG_INF)
        l_ref[...] = jnp.zeros_like(l_ref)
        acc_ref[...] = jnp.zeros_like(acc_ref)

    tk = k_ref.shape[2]
    per = tq // cq
    for c in range(GQA_GROUP * per):
        g, sb = divmod(c, per)
        cols = slice(c * cq, (c + 1) * cq)
        qs = q_ref[0, g, :, sb * cq:(sb + 1) * cq]
        m = m_ref[:, cols]
        l = l_ref[:, cols]
        acc = acc_ref[:, cols]
        for kb in range(tk // GQA_KEY_SUB):
            rows = slice(kb * GQA_KEY_SUB, (kb + 1) * GQA_KEY_SUB)
            s = jnp.dot(k_ref[0, 0, rows, :], qs, preferred_element_type=F32)
            m_new = jnp.maximum(m, jnp.max(s, axis=0, keepdims=True))
            alpha = jnp.exp2(m - m_new)
            p = jnp.exp2(s - m_new)
            l = alpha * l + jnp.sum(p, axis=0, keepdims=True)
            acc = alpha * acc + jnp.dot(vt_ref[0, 0, :, rows], p.astype(BF16), preferred_element_type=F32)
            m = m_new
        m_ref[:, cols] = m
        l_ref[:, cols] = l
        acc_ref[:, cols] = acc

    @pl.when(j == nk - 1)
    def _():
        ot = acc_ref[...] / l_ref[...]
        for g in range(GQA_GROUP):
            o_ref[0, :, g * HEAD_DIM:(g + 1) * HEAD_DIM] = ot[:, g * tq:(g + 1) * tq].T.astype(o_ref.dtype)


def _gqa_attention(qk, proj):
    bsz, seq, _ = qk.shape
    tq = _pick(seq, (1024, 512))
    tk = _pick(seq, (2048, 1024, 512))
    cq = 128
    nk = seq // tk
    gw = GQA_GROUP * HEAD_DIM
    qt = qk[:, :, :AX_Q_HEADS * HEAD_DIM].reshape(bsz, seq, AX_Q_HEADS, HEAD_DIM).transpose(0, 2, 3, 1)
    vt = proj[:, :, 1280:1536].reshape(bsz, seq, AX_KV_HEADS, HEAD_DIM).transpose(0, 2, 3, 1)
    kh = qk[:, :, AX_Q_HEADS * HEAD_DIM:].reshape(bsz, seq, AX_KV_HEADS, HEAD_DIM).transpose(0, 2, 1, 3)
    return pl.pallas_call(
        functools.partial(_gqa_kernel, tq=tq, cq=cq, nk=nk),
        grid=(bsz, AX_KV_HEADS, seq // tq, nk),
        in_specs=[pl.BlockSpec((1, GQA_GROUP, HEAD_DIM, tq), lambda b, h, i, j: (b, h, 0, i)),
                  pl.BlockSpec((1, 1, tk, HEAD_DIM), lambda b, h, i, j: (b, h, j, 0)),
                  pl.BlockSpec((1, 1, HEAD_DIM, tk), lambda b, h, i, j: (b, h, 0, j))],
        out_specs=pl.BlockSpec((1, tq, gw), lambda b, h, i, j: (b, i, h)),
        out_shape=jax.ShapeDtypeStruct((bsz, seq, AX_Q_HEADS * HEAD_DIM), BF16),
        scratch_shapes=[pltpu.VMEM((1, GQA_GROUP * tq), F32), pltpu.VMEM((1, GQA_GROUP * tq), F32),
                        pltpu.VMEM((HEAD_DIM, GQA_GROUP * tq), F32)],
        compiler_params=_cparams(("parallel", "parallel", "parallel", "arbitrary")),
        name="gqa_attn",
    )(qt, kh, vt)


def _na_bias_table(rpb):
    qi = np.arange(NA_ROWS)[:, None, None, None]
    qc = np.arange(GRID_W)[None, :, None, None]
    kj = np.arange(2 * NA_ROWS)[None, None, :, None]
    kc = np.arange(GRID_W)[None, None, None, :]
    c_start = np.clip(qc - NA_COLS // 2, 0, GRID_W - NA_COLS)
    col_ok = (kc >= c_start) & (kc < c_start + NA_COLS)
    rel_c = np.clip(kc - qc + NA_COLS - 1, 0, 2 * NA_COLS - 2)
    onehot_c = jnp.asarray(rel_c[0, :, 0, :, None] == np.arange(2 * NA_COLS - 1), dtype=F32)
    tabs = []
    for off, clamp in ((0, np.maximum), (-NA_ROWS // 2, None), (-NA_ROWS, np.minimum)):
        rs = qi - NA_ROWS // 2
        if clamp is not None:
            rs = clamp(rs, 0)
        kr = off + kj
        ok = (kr >= rs) & (kr < rs + NA_ROWS) & col_ok
        rel_r = np.clip(kr - qi + NA_ROWS - 1, 0, 2 * NA_ROWS - 2)[:, 0, :, 0]
        onehot_r = jnp.asarray(rel_r[:, :, None] == np.arange(2 * NA_ROWS - 1), dtype=F32)
        bias = jnp.einsum("ija,hab,cdb->hicjd", onehot_r, rpb.astype(F32), onehot_c, precision=HIGHEST)
        tabs.append(jnp.where(ok[None], bias, NEG_INF).reshape(NA_HEADS, NA_TQ, NA_WIN))
    return jnp.stack(tabs)


def _na_kernel(q_ref, k_ref, v_ref, bias_ref, o_ref, *, rows):
    i = pl.program_id(2)
    ws = jnp.clip(i * NA_ROWS - NA_ROWS // 2, 0, rows - 2 * NA_ROWS)
    start = pl.multiple_of(ws * GRID_W, GRID_W)
    k = k_ref[0, pl.ds(start, NA_WIN), :]
    v = v_ref[0, pl.ds(start, NA_WIN), :]
    for r in range(NA_TQ // LANES):
        rows = slice(r * LANES, (r + 1) * LANES)
        s = (lax.dot_general(q_ref[0, rows, :], k, (((1,), (1,)), ((), ())), preferred_element_type=F32)
             + bias_ref[0, 0, rows, :])
        m = jnp.max(s, axis=-1, keepdims=True)
        p = jnp.exp(s - m)
        l = jnp.sum(p, axis=-1, keepdims=True)
        o_ref[0, rows, :] = (jnp.dot(p.astype(BF16), v, preferred_element_type=F32) / l).astype(o_ref.dtype)


def _na_attention(proj, bias_tab):
    bsz, seq, _ = proj.shape
    rows = seq // GRID_W
    nq = seq // NA_TQ
    assert rows >= 3 * NA_ROWS and seq % NA_TQ == 0
    qcol, kcol, vcol = 1536 // HEAD_DIM, 2560 // HEAD_DIM, 3584 // HEAD_DIM

    def case(i):
        return jnp.where(i == 0, 0, jnp.where(i == nq - 1, 2, 1))

    return pl.pallas_call(
        functools.partial(_na_kernel, rows=rows),
        grid=(bsz, NA_HEADS, nq),
        in_specs=[pl.BlockSpec((1, NA_TQ, HEAD_DIM), lambda b, h, i: (b, i, qcol + h)),
                  pl.BlockSpec((1, seq, HEAD_DIM), lambda b, h, i: (b, 0, kcol + h)),
                  pl.BlockSpec((1, seq, HEAD_DIM), lambda b, h, i: (b, 0, vcol + h)),
                  pl.BlockSpec((1, 1, NA_TQ, NA_WIN), lambda b, h, i: (case(i), h, 0, 0))],
        out_specs=pl.BlockSpec((1, NA_TQ, HEAD_DIM), lambda b, h, i: (b, i, h)),
        out_shape=jax.ShapeDtypeStruct((bsz, seq, NA_HEADS * HEAD_DIM), BF16),
        compiler_params=_cparams(("parallel", "parallel", "arbitrary")),
        name="na_attn",
    )(proj, proj, proj, bias_tab)


def _outproj_kernel(oa_ref, ob_ref, wa_ref, wb_ref, x_ref, g_ref, lng_ref, lnb_ref, o_ref):
    m = (jnp.dot(oa_ref[0], wa_ref[...], preferred_element_type=F32)
         + jnp.dot(ob_ref[0], wb_ref[...], preferred_element_type=F32))
    y = DEEPNORM_ALPHA * x_ref[0] + g_ref[0] * m
    o_ref[0] = _layer_norm(y, lng_ref[...], lnb_ref[...])


def _outproj_ln(oa, ob, w_out, x, gate, ln_g, ln_b):
    bsz, seq, d = x.shape
    half = oa.shape[-1]
    tm = _pick(seq, (512, 256))
    vec = pl.BlockSpec((1, d), lambda b, i: (0, 0))
    return pl.pallas_call(
        _outproj_kernel,
        grid=(bsz, seq // tm),
        in_specs=[pl.BlockSpec((1, tm, half), lambda b, i: (b, i, 0)),
                  pl.BlockSpec((1, tm, half), lambda b, i: (b, i, 0)),
                  pl.BlockSpec((half, d), lambda b, i: (0, 0)),
                  pl.BlockSpec((half, d), lambda b, i: (1, 0)),
                  pl.BlockSpec((1, tm, d), lambda b, i: (b, i, 0)),
                  pl.BlockSpec((1, 1, d), lambda b, i: (b, 0, 0)),
                  vec, vec],
        out_specs=pl.BlockSpec((1, tm, d), lambda b, i: (b, i, 0)),
        out_shape=jax.ShapeDtypeStruct((bsz, seq, d), F32),
        compiler_params=_cparams(("parallel", "parallel")),
        name="outproj_ln",
    )(oa, ob, w_out, w_out, x, gate, ln_g.reshape(1, d), ln_b.reshape(1, d))


def _route(logits):
    lane = lax.broadcasted_iota(jnp.int32, logits.shape, 1)
    big = jnp.int32(2 ** 30)
    is_g = lane < N_GROUPS
    mg = jnp.max(jnp.where(is_g, logits, -jnp.inf), axis=-1, keepdims=True)
    g_idx = jnp.min(jnp.where(is_g & (logits == mg), lane, big), axis=-1, keepdims=True)
    p_g = 1.0 / jnp.sum(jnp.where(is_g, jnp.exp(logits - mg), 0.0), axis=-1, keepdims=True)
    lo = N_GROUPS + EXPERTS_PER_GROUP * g_idx
    in_grp = (lane >= lo) & (lane < lo + EXPERTS_PER_GROUP)
    v1 = jnp.max(jnp.where(in_grp, logits, -jnp.inf), axis=-1, keepdims=True)
    i1 = jnp.min(jnp.where(in_grp & (logits == v1), lane, big), axis=-1, keepdims=True)
    rest = in_grp & (lane != i1)
    v2 = jnp.max(jnp.where(rest, logits, -jnp.inf), axis=-1, keepdims=True)
    i2 = jnp.min(jnp.where(rest & (logits == v2), lane, big), axis=-1, keepdims=True)
    e2 = jnp.exp(v2 - v1)
    w1 = p_g / (1.0 + e2)
    w2 = p_g * e2 / (1.0 + e2)
    gates = jnp.where(lane == i1, w1, 0.0) + jnp.where(lane == i2, w2, 0.0)
    return jnp.where(lane == 0, g_idx.astype(F32), gates)


def _moe_route_kernel(x_ref, sc_ref, sh_ref, wr_ref, br_ref, h_ref, gate_ref):
    h = x_ref[0] * (1.0 + sc_ref[0]) + sh_ref[0]
    h_ref[0] = h.astype(BF16)
    logits = jnp.dot(h, wr_ref[...], precision=HIGHEST, preferred_element_type=F32) + br_ref[...]
    gate_ref[0] = _route(logits)


def _moe_route(x, sc, sh, w_router, b_router):
    bsz, seq, d = x.shape
    tm = _pick(seq, (512, 256))
    mod = pl.BlockSpec((1, 1, d), lambda b, i: (b, 0, 0))
    return pl.pallas_call(
        _moe_route_kernel,
        grid=(bsz, seq // tm),
        in_specs=[pl.BlockSpec((1, tm, d), lambda b, i: (b, i, 0)), mod, mod,
                  pl.BlockSpec((d, ROUTER_LANES), lambda b, i: (0, 0)),
                  pl.BlockSpec((1, ROUTER_LANES), lambda b, i: (0, 0))],
        out_specs=[pl.BlockSpec((1, tm, d), lambda b, i: (b, i, 0)),
                   pl.BlockSpec((1, tm, ROUTER_LANES), lambda b, i: (b, i, 0))],
        out_shape=[jax.ShapeDtypeStruct((bsz, seq, d), BF16),
                   jax.ShapeDtypeStruct((bsz, seq, ROUTER_LANES), F32)],
        compiler_params=_cparams(("parallel", "parallel")),
        name="moe_route",
    )(x, sc, sh, w_router, b_router)


def _moe_expert_kernel(h_ref, gate_ref, wg_ref, wu_ref, wd_ref, o_ref, pmt_ref, hs_ref, gs_ref, ys_ref, bounds_ref,
                       *, tm, chunk):
    e = pl.program_id(2)

    @pl.when(e == 0)
    def _():
        rec = gate_ref[0]
        lane = lax.broadcasted_iota(jnp.int32, (tm, ROUTER_LANES), 1)
        onehot = jnp.where((lane < N_GROUPS) & (lane == rec[:, 0:1].astype(jnp.int32)), 1.0, 0.0)
        below = (lax.broadcasted_iota(jnp.int32, (tm, tm), 1)
                 < lax.broadcasted_iota(jnp.int32, (tm, tm), 0)).astype(BF16)
        rank = jnp.dot(below, onehot.astype(BF16), preferred_element_type=F32)
        cnt = jnp.sum(onehot, axis=0, keepdims=True)
        padded = jnp.floor((cnt + (SORT_ALIGN - 1)) * (1.0 / SORT_ALIGN)) * SORT_ALIGN
        lane1 = lax.broadcasted_iota(jnp.int32, (1, ROUTER_LANES), 1)
        starts = jnp.zeros((1, ROUTER_LANES), F32)
        for g in range(N_GROUPS):
            first = jnp.sum(jnp.where(lane1 < g, padded, 0.0))
            size = jnp.sum(jnp.where(lane1 == g, cnt, 0.0))
            starts = starts + jnp.where(lane1 == g, first, 0.0)
            bounds_ref[2 * g] = first.astype(jnp.int32)
            bounds_ref[2 * g + 1] = (size.astype(jnp.int32) + chunk - 1) // chunk
        pos = jnp.sum(onehot * (rank + starts), axis=1, keepdims=True).astype(jnp.int32)
        pmt = (lax.broadcasted_iota(jnp.int32, pmt_ref.shape, 1) == pos).astype(BF16)
        pmt_ref[...] = pmt
        dn = (((0,), (0,)), ((), ()))
        hs_ref[...] = lax.dot_general(pmt, h_ref[0], dn, preferred_element_type=F32).astype(BF16)
        gs = jnp.zeros(gs_ref.shape, F32)
        for piece in _split3(rec):
            gs = gs + lax.dot_general(pmt, piece, dn, preferred_element_type=F32)
        gs_ref[...] = gs
        ys_ref[...] = jnp.zeros_like(ys_ref)

    grp = e // EXPERTS_PER_GROUP
    lane_c = lax.broadcasted_iota(jnp.int32, (chunk, ROUTER_LANES), 1)
    first_row = bounds_ref[2 * grp]

    def body(c, carry):
        rows = pl.ds(pl.multiple_of(first_row + c * chunk, SORT_ALIGN), chunk)
        hb = hs_ref[rows, :]
        a = jnp.dot(hb, wg_ref[0], preferred_element_type=F32)
        u = jnp.dot(hb, wu_ref[0], preferred_element_type=F32)
        ge = jnp.sum(jnp.where(lane_c == e + N_GROUPS, gs_ref[rows, :], 0.0), axis=-1, keepdims=True)
        act = (_silu(a) * u * ge).astype(BF16)
        ys_ref[rows, :] += jnp.dot(act, wd_ref[0], preferred_element_type=F32)
        return carry

    lax.fori_loop(0, bounds_ref[2 * grp + 1], body, 0)

    @pl.when(e == N_EXPERTS - 1)
    def _():
        o_ref[0] = jnp.dot(pmt_ref[...], ys_ref[...].astype(BF16), preferred_element_type=F32).astype(o_ref.dtype)


def _moe_experts(h, rec, w_gate, w_up, w_down):
    bsz, seq, d = h.shape
    tm = _pick(seq, (1024, 512))
    chunk = 128
    ps = tm + 2 * chunk
    assert N_GROUPS * (SORT_ALIGN - 1) + chunk <= 2 * chunk
    return pl.pallas_call(
        functools.partial(_moe_expert_kernel, tm=tm, chunk=chunk),
        grid=(bsz, seq // tm, N_EXPERTS),
        in_specs=[pl.BlockSpec((1, tm, d), lambda b, i, e: (b, i, 0)),
                  pl.BlockSpec((1, tm, ROUTER_LANES), lambda b, i, e: (b, i, 0)),
                  pl.BlockSpec((1, d, D_EXPERT), lambda b, i, e: (e, 0, 0)),
                  pl.BlockSpec((1, d, D_EXPERT), lambda b, i, e: (e, 0, 0)),
                  pl.BlockSpec((1, D_EXPERT, d), lambda b, i, e: (e, 0, 0))],
        out_specs=pl.BlockSpec((1, tm, d), lambda b, i, e: (b, i, 0)),
        out_shape=jax.ShapeDtypeStruct((bsz, seq, d), BF16),
        scratch_shapes=[pltpu.VMEM((tm, ps), BF16), pltpu.VMEM((ps, d), BF16), pltpu.VMEM((ps, ROUTER_LANES), F32),
                        pltpu.VMEM((ps, d), F32), pltpu.SMEM((2 * N_GROUPS,), jnp.int32)],
        compiler_params=_cparams(("parallel", "parallel", "arbitrary")),
        name="moe_experts",
    )(h, rec, w_gate, w_up, w_down)


def _resid_ln_kernel(x_ref, f_ref, g_ref, lng_ref, lnb_ref, o_ref):
    y = DEEPNORM_ALPHA * x_ref[0] + g_ref[0] * f_ref[0].astype(F32)
    o_ref[0] = _layer_norm(y, lng_ref[...], lnb_ref[...])


def _resid_ln(x, f, gate, ln_g, ln_b):
    bsz, seq, d = x.shape
    tm = _pick(seq, (512, 256))
    vec = pl.BlockSpec((1, d), lambda b, i: (0, 0))
    tile = pl.BlockSpec((1, tm, d), lambda b, i: (b, i, 0))
    return pl.pallas_call(
        _resid_ln_kernel,
        grid=(bsz, seq // tm),
        in_specs=[tile, tile, pl.BlockSpec((1, 1, d), lambda b, i: (b, 0, 0)), vec, vec],
        out_specs=tile,
        out_shape=jax.ShapeDtypeStruct((bsz, seq, d), F32),
        compiler_params=_cparams(("parallel", "parallel")),
        name="resid_ln",
    )(x, f, gate, ln_g.reshape(1, d), ln_b.reshape(1, d))


def _moe_ln(x, sc, sh, gate, w_router, b_router, w_gate, w_up, w_down, ln_g, ln_b):
    h, rec = _moe_route(x, sc, sh, w_router, b_router)
    f = _moe_experts(h, rec, w_gate, w_up, w_down)
    return _resid_ln(x, f, gate, ln_g, ln_b)


def _alibi_slopes(n):
    return jnp.asarray(2.0 ** (-8.0 * np.arange(1, n + 1) / n), dtype=F32)


def _colscale(q_ranges):
    cs = np.ones((1, PROJ_N), np.float32)
    for lo, hi, scale in q_ranges:
        cs[:, lo:hi] = scale
    return jnp.asarray(cs)


def _prep_params(p):
    q = dict(p)
    for name in ("w_in_ab", "w_out_ab", "w_in_cd", "w_out_cd", "w_gate", "w_up", "w_down"):
        q[name] = p[name].astype(BF16)
    pad = ROUTER_LANES - N_GROUPS - N_EXPERTS
    q["w_router"] = jnp.pad(jnp.concatenate([p["w_rg"], p["w_re"]], axis=-1), ((0, 0), (0, 0), (0, pad)))
    q["b_router"] = jnp.pad(jnp.concatenate([p["b_rg"], p["b_re"]], axis=-1), ((0, 0), (0, pad)))[:, None, :]
    q["na_bias"] = [_na_bias_table(p["rpb_d"][i]) for i in range(p["rpb_d"].shape[0])]
    return q


def _trunk(x, ada, p):
    for l in range(DEPTH):
        sh1, sc1, g1, sh2, sc2, g2 = jnp.split(ada[l][:, None, :], 6, axis=-1)
        i = l // 2
        if l % 2 == 0:
            lam_init = 0.8 - 0.6 * math.exp(-0.3 * l)
            proj = _proj(x, sc1, sh1, p["w_in_ab"][i], _colscale(((0, 1024, ATTN_SCALE), (1536, 2560, ATTN_SCALE * LOG2E))))
            oa = _win_attention(proj, p["sink_a"][i], _alibi_slopes(WIN_Q_HEADS))
            lamvec = jnp.stack([p["lam_q1"][i], p["lam_k1"][i], p["lam_q2"][i], p["lam_k2"][i]])
            ob = _diff_attention(proj, _alibi_slopes(DIFF_HEADS), lamvec, p["subln_g"][i][None, :], lam_init)
            w_out = p["w_out_ab"][i]
        else:
            proj = _proj(x, sc1, sh1, p["w_in_cd"][i], _colscale(((1536, 2560, ATTN_SCALE),)))
            qk = _normrope(proj, p["qnorm_c"][i], p["knorm_c"][i])
            oa = _gqa_attention(qk, proj)
            ob = _na_attention(proj, p["na_bias"][i])
            w_out = p["w_out_cd"][i]
        x = _outproj_ln(oa, ob, w_out, x, g1, p["ln_g"][l, 0], p["ln_b"][l, 0])
        x = _moe_ln(x, sc2, sh2, g2, p["w_router"][l], p["b_router"][l],
                    p["w_gate"][l], p["w_up"][l], p["w_down"][l], p["ln_g"][l, 1], p["ln_b"][l, 1])
    return x


def kernel(x_prompt, x_sample, c_prompt, c_sample, w_ada, b_ada, ln_g, ln_b, w_in_ab, w_out_ab, sink_a,
           lam_q1, lam_k1, lam_q2, lam_k2, subln_g, w_in_cd, w_out_cd, qnorm_c, knorm_c, rpb_d,
           w_rg, b_rg, w_re, b_re, w_gate, w_up, w_down):
    p = _prep_params(dict(
        ln_g=ln_g, ln_b=ln_b, w_in_ab=w_in_ab, w_out_ab=w_out_ab, sink_a=sink_a,
        lam_q1=lam_q1, lam_k1=lam_k1, lam_q2=lam_q2, lam_k2=lam_k2, subln_g=subln_g,
        w_in_cd=w_in_cd, w_out_cd=w_out_cd, qnorm_c=qnorm_c, knorm_c=knorm_c, rpb_d=rpb_d,
        w_rg=w_rg, b_rg=b_rg, w_re=w_re, b_re=b_re, w_gate=w_gate, w_up=w_up, w_down=w_down))
    nb_p, nb_s = c_prompt.shape[0], c_sample.shape[0]
    c_all = jnp.concatenate([c_prompt, c_sample, jnp.zeros((8 - nb_p - nb_s, D_MODEL), F32)], axis=0)
    ada = _ada(c_all, w_ada, b_ada)
    y_prompt = _trunk(x_prompt, ada[:, :nb_p], p)
    y_sample = _trunk(x_sample, ada[:, nb_p:nb_p + nb_s], p)
    return (y_prompt, y_sample)
```

```python
import functools
import math

import numpy as np
import jax
import jax.numpy as jnp
from jax import lax
from jax.experimental import pallas as pl
from jax.experimental.pallas import tpu as pltpu

F32 = jnp.float32
BF16 = jnp.bfloat16
HIGHEST = lax.Precision.HIGHEST

D_MODEL = 2048
DEPTH = 2
HEAD_DIM = 128
WINDOW = 128
GRID_W = 64
WIN_Q_HEADS = 8
WIN_KV_HEADS = 2
GQA_GROUP = 4
DIFF_HEADS = 4
DIFF_V_DIM = 256
AX_Q_HEADS = 8
AX_KV_HEADS = 2
NA_HEADS = 8
NA_ROWS = 8
NA_COLS = 16
ROPE_THETA = 10000.0
N_GROUPS = 4
EXPERTS_PER_GROUP = 4
N_EXPERTS = 16
D_EXPERT = 512
LN_EPS = 1e-5
RMS_EPS = 1e-6
NEG_INF = -1e30
DEEPNORM_ALPHA = (2 * DEPTH) ** 0.25
PROJ_N = 4608
ATTN_SCALE = HEAD_DIM ** -0.5
LOG2E = math.log2(math.e)
LANES = 128
ROUTER_LANES = LANES
SORT_ALIGN = 16
VMEM_LIMIT = 56 * 1024 * 1024

GQA_KEY_SUB = 256
DIFF_KEY_SUB = 512
NA_TQ = NA_ROWS * GRID_W
NA_WIN = 2 * NA_ROWS * GRID_W


def _cparams(sem):
    return pltpu.CompilerParams(dimension_semantics=sem, vmem_limit_bytes=VMEM_LIMIT)


def _pick(n, cands):
    for c in cands:
        if n % c == 0:
            return c
    raise ValueError(f"no tile in {cands} divides {n}")


def _layer_norm(y, g, b):
    mu = jnp.mean(y, axis=-1, keepdims=True)
    yc = y - mu
    var = jnp.mean(yc * yc, axis=-1, keepdims=True)
    return yc * lax.rsqrt(var + LN_EPS) * g + b


def _silu(a):
    return a / (1.0 + jnp.exp(-a))


def _ada_kernel(c_ref, w_ref, b_ref, o_ref):
    a = _silu(c_ref[...])
    o_ref[0] = jnp.dot(a, w_ref[0], precision=HIGHEST, preferred_element_type=F32) + b_ref[0]


def _ada(c_all, w_ada, b_ada):
    n = 6 * D_MODEL
    tn = 1024
    return pl.pallas_call(
        _ada_kernel,
        grid=(DEPTH, n // tn),
        in_specs=[pl.BlockSpec((8, D_MODEL), lambda l, j: (0, 0)),
                  pl.BlockSpec((1, D_MODEL, tn), lambda l, j: (l, 0, j)),
                  pl.BlockSpec((1, 1, tn), lambda l, j: (l, 0, j))],
        out_specs=pl.BlockSpec((1, 8, tn), lambda l, j: (l, 0, j)),
        out_shape=jax.ShapeDtypeStruct((DEPTH, 8, n), F32),
        compiler_params=_cparams(("parallel", "parallel")),
        name="ada",
    )(c_all, w_ada, b_ada.reshape(DEPTH, 1, n))


def _proj_kernel(x_ref, sc_ref, sh_ref, w_ref, cs_ref, o_ref, h_ref):
    @pl.when(pl.program_id(2) == 0)
    def _():
        h_ref[...] = (x_ref[0] * (1.0 + sc_ref[0]) + sh_ref[0]).astype(BF16)

    acc = jnp.dot(h_ref[...], w_ref[...], preferred_element_type=F32)
    o_ref[0] = (acc * cs_ref[...]).astype(o_ref.dtype)


def _proj(x, sc, sh, w, colscale):
    bsz, seq, d = x.shape
    n = w.shape[1]
    tm = _pick(seq, (512, 256))
    tn = 2304
    return pl.pallas_call(
        _proj_kernel,
        grid=(bsz, seq // tm, n // tn),
        in_specs=[pl.BlockSpec((1, tm, d), lambda b, i, j: (b, i, 0)),
                  pl.BlockSpec((1, 1, d), lambda b, i, j: (b, 0, 0)),
                  pl.BlockSpec((1, 1, d), lambda b, i, j: (b, 0, 0)),
                  pl.BlockSpec((d, tn), lambda b, i, j: (0, j)),
                  pl.BlockSpec((1, tn), lambda b, i, j: (0, j))],
        out_specs=pl.BlockSpec((1, tm, tn), lambda b, i, j: (b, i, j)),
        out_shape=jax.ShapeDtypeStruct((bsz, seq, n), BF16),
        scratch_shapes=[pltpu.VMEM((tm, d), BF16)],
        compiler_params=_cparams(("parallel", "parallel", "arbitrary")),
        name="proj",
    )(x, sc, sh, w, colscale)


def _win_kernel(sink_ref, slope_ref, q_ref, kp_ref, km_ref, kn_ref, vp_ref, vm_ref, vn_ref, o_ref, *, tq, seq):
    i = pl.program_id(1)
    kvh = pl.program_id(2)
    k = jnp.concatenate([kp_ref[0], km_ref[0], kn_ref[0]], axis=0)
    v = jnp.concatenate([vp_ref[0], vm_ref[0], vn_ref[0]], axis=0)
    w = tq + 2 * WINDOW
    qrel = lax.broadcasted_iota(jnp.int32, (tq, w), 0)
    krel = lax.broadcasted_iota(jnp.int32, (tq, w), 1) - WINDOW
    dist = jnp.abs(qrel - krel)
    kabs = krel + i * tq
    valid = (dist <= WINDOW) & (kabs >= 0) & (kabs < seq)
    distf = dist.astype(F32)
    for g in range(GQA_GROUP):
        h = kvh * GQA_GROUP + g
        qg = q_ref[0, :, g * HEAD_DIM:(g + 1) * HEAD_DIM]
        s = lax.dot_general(qg, k, (((1,), (1,)), ((), ())), preferred_element_type=F32)
        s = jnp.where(valid, s - slope_ref[h] * distf, NEG_INF)
        snk = sink_ref[h]
        m = jnp.maximum(jnp.max(s, axis=-1, keepdims=True), snk)
        p = jnp.exp(s - m)
        l = jnp.sum(p, axis=-1, keepdims=True) + jnp.exp(snk - m)
        o = jnp.dot(p.astype(BF16), v, preferred_element_type=F32) / l
        o_ref[0, :, g * HEAD_DIM:(g + 1) * HEAD_DIM] = o.astype(o_ref.dtype)


def _win_attention(proj, sink, slopes):
    bsz, seq, _ = proj.shape
    tq = 256
    r = tq // WINDOW
    nb = seq // WINDOW
    kcol = WIN_Q_HEADS
    vcol = WIN_Q_HEADS + WIN_KV_HEADS

    def prev(col):
        return pl.BlockSpec((1, WINDOW, HEAD_DIM), lambda b, i, k: (b, jnp.maximum(i * r - 1, 0), col + k))

    def main(col):
        return pl.BlockSpec((1, tq, HEAD_DIM), lambda b, i, k: (b, i, col + k))

    def nxt(col):
        return pl.BlockSpec((1, WINDOW, HEAD_DIM), lambda b, i, k: (b, jnp.minimum((i + 1) * r, nb - 1), col + k))

    smem = pl.BlockSpec(memory_space=pltpu.SMEM)
    return pl.pallas_call(
        functools.partial(_win_kernel, tq=tq, seq=seq),
        grid=(bsz, seq // tq, WIN_KV_HEADS),
        in_specs=[smem, smem,
                  pl.BlockSpec((1, tq, GQA_GROUP * HEAD_DIM), lambda b, i, k: (b, i, k)),
                  prev(kcol), main(kcol), nxt(kcol), prev(vcol), main(vcol), nxt(vcol)],
        out_specs=pl.BlockSpec((1, tq, GQA_GROUP * HEAD_DIM), lambda b, i, k: (b, i, k)),
        out_shape=jax.ShapeDtypeStruct((bsz, seq, WIN_Q_HEADS * HEAD_DIM), BF16),
        compiler_params=_cparams(("parallel", "parallel", "parallel")),
        name="win_attn",
    )(sink, slopes, proj, proj, proj, proj, proj, proj, proj)


def _split3(x):
    p1 = x.astype(BF16)
    r1 = x - p1.astype(F32)
    p2 = r1.astype(BF16)
    return p1, p2, (r1 - p2.astype(F32)).astype(BF16)


def _diff_kernel(slope_ref, q_ref, k1_ref, k2_ref, vt_ref, lam_ref, g_ref, o_ref, bm_ref, m_ref, l_ref, acc_ref,
                 *, tq, tk, cq, nk, lam_init):
    h = pl.program_id(1)
    i = pl.program_id(2)
    j = pl.program_id(3)
    sigma = slope_ref[h] * LOG2E

    @pl.when(j == 0)
    def _():
        m_ref[...] = jnp.full_like(m_ref, NEG_INF)
        l_ref[...] = jnp.zeros_like(l_ref)
        acc_ref[...] = jnp.zeros_like(acc_ref)
        dbase = (lax.broadcasted_iota(jnp.int32, (DIFF_KEY_SUB, cq), 1)
                 - lax.broadcasted_iota(jnp.int32, (DIFF_KEY_SUB, cq), 0)).astype(F32)
        bm_ref[...] = dbase * sigma

    def sweep(frame):
        for c in range(tq // cq):
            cols = slice(c * cq, (c + 1) * cq)
            for mi, k_ref in enumerate((k1_ref, k2_ref)):
                qs = q_ref[0, 0, mi, :, cols]
                m = m_ref[mi, :, cols]
                l = l_ref[mi, :, cols]
                acc = acc_ref[mi, :, cols]
                for kb in range(tk // DIFF_KEY_SUB):
                    rows = slice(kb * DIFF_KEY_SUB, (kb + 1) * DIFF_KEY_SUB)
                    soff = sigma * (i * tq + c * cq - j * tk - kb * DIFF_KEY_SUB).astype(F32)
                    s = jnp.dot(k_ref[0, rows, :], qs, preferred_element_type=F32)
                    t, shift = frame(s, soff)
                    m_new = jnp.maximum(m, jnp.max(t, axis=0, keepdims=True) + shift)
                    alpha = jnp.exp2(m - m_new)
                    p = jnp.exp2(t - (m_new - shift))
                    l = alpha * l + jnp.sum(p, axis=0, keepdims=True)
                    acc = alpha * acc + jnp.dot(vt_ref[0, 0, :, rows], p.astype(BF16), preferred_element_type=F32)
                    m = m_new
                m_ref[mi, :, cols] = m
                l_ref[mi, :, cols] = l
                acc_ref[mi, :, cols] = acc

    @pl.when(i * tq >= (j + 1) * tk)
    def _():
        sweep(lambda s, soff: (s - bm_ref[...], -soff))

    @pl.when((i + 1) * tq <= j * tk)
    def _():
        sweep(lambda s, soff: (s + bm_ref[...], soff))

    @pl.when((i * tq < (j + 1) * tk) & ((i + 1) * tq > j * tk))
    def _():
        sweep(lambda s, soff: (s - jnp.abs(bm_ref[...] + soff), 0.0))

    @pl.when(j == nk - 1)
    def _():
        lv = lam_ref[...]
        lam = (jnp.exp(jnp.sum(lv[0:1] * lv[1:2], axis=-1, keepdims=True))
               - jnp.exp(jnp.sum(lv[2:3] * lv[3:4], axis=-1, keepdims=True)) + lam_init)
        ot = acc_ref[0] / l_ref[0] - lam * (acc_ref[1] / l_ref[1])
        o = ot.T
        r = lax.rsqrt(jnp.mean(o * o, axis=-1, keepdims=True) + RMS_EPS)
        o_ref[0] = (o * r * g_ref[...] * (1.0 - lam_init)).astype(o_ref.dtype)


def _diff_attention(proj, slopes, lamvec, subln_g, lam_init):
    bsz, seq, _ = proj.shape
    tq = _pick(seq, (1024, 512))
    tk = _pick(seq, (2048, 1024, 512))
    cq = 256
    nk = seq // tk
    qt = proj[:, :, 1536:2560].reshape(bsz, seq, DIFF_HEADS, 2, HEAD_DIM).transpose(0, 2, 3, 4, 1)
    vt = proj[:, :, 3584:4608].reshape(bsz, seq, DIFF_HEADS, DIFF_V_DIM).transpose(0, 2, 3, 1)
    kcol = 2560 // HEAD_DIM
    smem = pl.BlockSpec(memory_space=pltpu.SMEM)

    def kspec(off):
        return pl.BlockSpec((1, tk, HEAD_DIM), lambda b, h, i, j: (b, j, kcol + 2 * h + off))

    return pl.pallas_call(
        functools.partial(_diff_kernel, tq=tq, tk=tk, cq=cq, nk=nk, lam_init=lam_init),
        grid=(bsz, DIFF_HEADS, seq // tq, nk),
        in_specs=[smem,
                  pl.BlockSpec((1, 1, 2, HEAD_DIM, tq), lambda b, h, i, j: (b, h, 0, 0, i)),
                  kspec(0), kspec(1),
                  pl.BlockSpec((1, 1, DIFF_V_DIM, tk), lambda b, h, i, j: (b, h, 0, j)),
                  pl.BlockSpec((4, HEAD_DIM), lambda b, h, i, j: (0, 0)),
                  pl.BlockSpec((1, DIFF_V_DIM), lambda b, h, i, j: (0, 0))],
        out_specs=pl.BlockSpec((1, tq, DIFF_V_DIM), lambda b, h, i, j: (b, i, h)),
        out_shape=jax.ShapeDtypeStruct((bsz, seq, DIFF_HEADS * DIFF_V_DIM), BF16),
        scratch_shapes=[pltpu.VMEM((DIFF_KEY_SUB, cq), F32),
                        pltpu.VMEM((2, 1, tq), F32), pltpu.VMEM((2, 1, tq), F32),
                        pltpu.VMEM((2, DIFF_V_DIM, tq), F32)],
        compiler_params=_cparams(("parallel", "parallel", "parallel", "arbitrary")),
        name="diff_attn",
    )(slopes, qt, proj, proj, vt, lamvec, subln_g)


def _normrope_kernel(x_ref, cos_ref, sin_ref, gq_ref, gk_ref, o_ref, *, tm):
    j = pl.program_id(2)
    is_q = j < AX_Q_HEADS // 2
    g = jnp.where(is_q, gq_ref[...], gk_ref[...])
    post = jnp.where(is_q, ATTN_SCALE * LOG2E, 1.0).astype(F32)
    lane = lax.broadcasted_iota(jnp.int32, (tm, HEAD_DIM), 1)
    first = (lane % 64) < 32
    cos = cos_ref[...]
    sin = sin_ref[...]
    for hh in range(2):
        x = x_ref[0, :, hh * HEAD_DIM:(hh + 1) * HEAD_DIM].astype(F32)
        xn = x * lax.rsqrt(jnp.mean(x * x, axis=-1, keepdims=True) + RMS_EPS) * g
        partner = jnp.where(first, pltpu.roll(xn, 96, 1), pltpu.roll(xn, 32, 1))
        y = (xn * cos + partner * sin) * post
        o_ref[0, :, hh * HEAD_DIM:(hh + 1) * HEAD_DIM] = y.astype(o_ref.dtype)


def _rope_tables(seq):
    half = HEAD_DIM // 2
    inv = jnp.asarray(1.0 / (ROPE_THETA ** (np.arange(0, half, 2) / half)), dtype=F32)
    t = jnp.arange(seq)
    ang_r = (t // GRID_W).astype(F32)[:, None] * inv[None, :]
    ang_c = (t % GRID_W).astype(F32)[:, None] * inv[None, :]
    cr, sr, cc, sn = jnp.cos(ang_r), jnp.sin(ang_r), jnp.cos(ang_c), jnp.sin(ang_c)
    return (jnp.concatenate([cr, cr, cc, cc], axis=-1), jnp.concatenate([-sr, sr, -sn, sn], axis=-1))


def _normrope(proj, qnorm_g, knorm_g):
    bsz, seq, _ = proj.shape
    tm = _pick(seq, (2048, 1024, 512))
    cos, sin = _rope_tables(seq)
    nw = (AX_Q_HEADS + AX_KV_HEADS) * HEAD_DIM
    return pl.pallas_call(
        functools.partial(_normrope_kernel, tm=tm),
        grid=(bsz, seq // tm, nw // 256),
        in_specs=[pl.BlockSpec((1, tm, 256), lambda b, i, j: (b, i, j)),
                  pl.BlockSpec((tm, HEAD_DIM), lambda b, i, j: (i, 0)),
                  pl.BlockSpec((tm, HEAD_DIM), lambda b, i, j: (i, 0)),
                  pl.BlockSpec((1, HEAD_DIM), lambda b, i, j: (0, 0)),
                  pl.BlockSpec((1, HEAD_DIM), lambda b, i, j: (0, 0))],
        out_specs=pl.BlockSpec((1, tm, 256), lambda b, i, j: (b, i, j)),
        out_shape=jax.ShapeDtypeStruct((bsz, seq, nw), BF16),
        compiler_params=_cparams(("parallel", "parallel", "parallel")),
        name="normrope",
    )(proj, cos, sin, qnorm_g.reshape(1, HEAD_DIM), knorm_g.reshape(1, HEAD_DIM))


def _gqa_kernel(q_ref, k_ref, vt_ref, o_ref, m_ref, l_ref, acc_ref, *, tq, cq, nk):
    j = pl.program_id(3)

    @pl.when(j == 0)
    def _():
        m_ref[...] = jnp.full_like(m_ref, NEG_INF)
        l_ref[...] = jnp.zeros_like(l_ref)
        acc_ref[...] = jnp.zeros_like(acc_ref)

    tk = k_ref.shape[1]
    per = tq // cq
    for c in range(GQA_GROUP * per):
        g, sb = divmod(c, per)
        cols = slice(c * cq, (c + 1) * cq)
        qs = q_ref[0, g, :, sb * cq:(sb + 1) * cq]
        m = m_ref[:, cols]
        l = l_ref[:, cols]
        acc = acc_ref[:, cols]
        for kb in range(tk // GQA_KEY_SUB):
            rows = slice(kb * GQA_KEY_SUB, (kb + 1) * GQA_KEY_SUB)
            s = jnp.dot(k_ref[0, rows, :], qs, preferred_element_type=F32)
            m_new = jnp.maximum(m, jnp.max(s, axis=0, keepdims=True))
            alpha = jnp.exp2(m - m_new)
            p = jnp.exp2(s - m_new)
            l = alpha * l + jnp.sum(p, axis=0, keepdims=True)
            acc = alpha * acc + jnp.dot(vt_ref[0, 0, :, rows], p.astype(BF16), preferred_element_type=F32)
            m = m_new
        m_ref[:, cols] = m
        l_ref[:, cols] = l
        acc_ref[:, cols] = acc

    @pl.when(j == nk - 1)
    def _():
        ot = acc_ref[...] / l_ref[...]
        for g in range(GQA_GROUP):
            o_ref[0, :, g * HEAD_DIM:(g + 1) * HEAD_DIM] = ot[:, g * tq:(g + 1) * tq].T.astype(o_ref.dtype)


def _gqa_attention(qk, proj):
    bsz, seq, _ = qk.shape
    tq = _pick(seq, (1024, 512))
    tk = _pick(seq, (2048, 1024, 512))
    cq = 128
    nk = seq // tk
    gw = GQA_GROUP * HEAD_DIM
    qt = qk[:, :, :AX_Q_HEADS * HEAD_DIM].reshape(bsz, seq, AX_Q_HEADS, HEAD_DIM).transpose(0, 2, 3, 1)
    vt = proj[:, :, 1280:1536].reshape(bsz, seq, AX_KV_HEADS, HEAD_DIM).transpose(0, 2, 3, 1)
    return pl.pallas_call(
        functools.partial(_gqa_kernel, tq=tq, cq=cq, nk=nk),
        grid=(bsz, AX_KV_HEADS, seq // tq, nk),
        in_specs=[pl.BlockSpec((1, GQA_GROUP, HEAD_DIM, tq), lambda b, h, i, j: (b, h, 0, i)),
                  pl.BlockSpec((1, tk, HEAD_DIM), lambda b, h, i, j: (b, j, AX_Q_HEADS + h)),
                  pl.BlockSpec((1, 1, HEAD_DIM, tk), lambda b, h, i, j: (b, h, 0, j))],
        out_specs=pl.BlockSpec((1, tq, gw), lambda b, h, i, j: (b, i, h)),
        out_shape=jax.ShapeDtypeStruct((bsz, seq, AX_Q_HEADS * HEAD_DIM), BF16),
        scratch_shapes=[pltpu.VMEM((1, GQA_GROUP * tq), F32), pltpu.VMEM((1, GQA_GROUP * tq), F32),
                        pltpu.VMEM((HEAD_DIM, GQA_GROUP * tq), F32)],
        compiler_params=_cparams(("parallel", "parallel", "parallel", "arbitrary")),
        name="gqa_attn",
    )(qt, qk, vt)


def _na_bias_table(rpb):
    qi = np.arange(NA_ROWS)[:, None, None, None]
    qc = np.arange(GRID_W)[None, :, None, None]
    kj = np.arange(2 * NA_ROWS)[None, None, :, None]
    kc = np.arange(GRID_W)[None, None, None, :]
    c_start = np.clip(qc - NA_COLS // 2, 0, GRID_W - NA_COLS)
    col_ok = (kc >= c_start) & (kc < c_start + NA_COLS)
    rel_c = np.clip(kc - qc + NA_COLS - 1, 0, 2 * NA_COLS - 2)
    onehot_c = jnp.asarray(rel_c[0, :, 0, :, None] == np.arange(2 * NA_COLS - 1), dtype=F32)
    tabs = []
    for off, clamp in ((0, np.maximum), (-NA_ROWS // 2, None), (-NA_ROWS, np.minimum)):
        rs = qi - NA_ROWS // 2
        if clamp is not None:
            rs = clamp(rs, 0)
        kr = off + kj
        ok = (kr >= rs) & (kr < rs + NA_ROWS) & col_ok
        rel_r = np.clip(kr - qi + NA_ROWS - 1, 0, 2 * NA_ROWS - 2)[:, 0, :, 0]
        onehot_r = jnp.asarray(rel_r[:, :, None] == np.arange(2 * NA_ROWS - 1), dtype=F32)
        bias = jnp.einsum("ija,hab,cdb->hicjd", onehot_r, rpb.astype(F32), onehot_c, precision=HIGHEST)
        tabs.append(jnp.where(ok[None], bias, NEG_INF).reshape(NA_HEADS, NA_TQ, NA_WIN))
    return jnp.stack(tabs)


def _na_kernel(q_ref, k_ref, v_ref, bias_ref, o_ref, *, rows):
    i = pl.program_id(2)
    ws = jnp.clip(i * NA_ROWS - NA_ROWS // 2, 0, rows - 2 * NA_ROWS)
    start = pl.multiple_of(ws * GRID_W, GRID_W)
    k = k_ref[0, pl.ds(start, NA_WIN), :]
    v = v_ref[0, pl.ds(start, NA_WIN), :]
    s = lax.dot_general(q_ref[0], k, (((1,), (1,)), ((), ())), preferred_element_type=F32) + bias_ref[0, 0]
    m = jnp.max(s, axis=-1, keepdims=True)
    p = jnp.exp(s - m)
    l = jnp.sum(p, axis=-1, keepdims=True)
    o_ref[0] = (jnp.dot(p.astype(BF16), v, preferred_element_type=F32) / l).astype(o_ref.dtype)


def _na_attention(proj, bias_tab):
    bsz, seq, _ = proj.shape
    rows = seq // GRID_W
    nq = seq // NA_TQ
    assert rows >= 3 * NA_ROWS and seq % NA_TQ == 0
    qcol, kcol, vcol = 1536 // HEAD_DIM, 2560 // HEAD_DIM, 3584 // HEAD_DIM

    def case(i):
        return jnp.where(i == 0, 0, jnp.where(i == nq - 1, 2, 1))

    return pl.pallas_call(
        functools.partial(_na_kernel, rows=rows),
        grid=(bsz, NA_HEADS, nq),
        in_specs=[pl.BlockSpec((1, NA_TQ, HEAD_DIM), lambda b, h, i: (b, i, qcol + h)),
                  pl.BlockSpec((1, seq, HEAD_DIM), lambda b, h, i: (b, 0, kcol + h)),
                  pl.BlockSpec((1, seq, HEAD_DIM), lambda b, h, i: (b, 0, vcol + h)),
                  pl.BlockSpec((1, 1, NA_TQ, NA_WIN), lambda b, h, i: (case(i), h, 0, 0))],
        out_specs=pl.BlockSpec((1, NA_TQ, HEAD_DIM), lambda b, h, i: (b, i, h)),
        out_shape=jax.ShapeDtypeStruct((bsz, seq, NA_HEADS * HEAD_DIM), BF16),
        compiler_params=_cparams(("parallel", "parallel", "arbitrary")),
        name="na_attn",
    )(proj, proj, proj, bias_tab)


def _outproj_kernel(oa_ref, ob_ref, wa_ref, wb_ref, x_ref, g_ref, lng_ref, lnb_ref, o_ref):
    m = (jnp.dot(oa_ref[0], wa_ref[...], preferred_element_type=F32)
         + jnp.dot(ob_ref[0], wb_ref[...], preferred_element_type=F32))
    y = DEEPNORM_ALPHA * x_ref[0] + g_ref[0] * m
    o_ref[0] = _layer_norm(y, lng_ref[...], lnb_ref[...])


def _outproj_ln(oa, ob, w_out, x, gate, ln_g, ln_b):
    bsz, seq, d = x.shape
    half = oa.shape[-1]
    tm = _pick(seq, (512, 256))
    vec = pl.BlockSpec((1, d), lambda b, i: (0, 0))
    return pl.pallas_call(
        _outproj_kernel,
        grid=(bsz, seq // tm),
        in_specs=[pl.BlockSpec((1, tm, half), lambda b, i: (b, i, 0)),
                  pl.BlockSpec((1, tm, half), lambda b, i: (b, i, 0)),
                  pl.BlockSpec((half, d), lambda b, i: (0, 0)),
                  pl.BlockSpec((half, d), lambda b, i: (1, 0)),
                  pl.BlockSpec((1, tm, d), lambda b, i: (b, i, 0)),
                  pl.BlockSpec((1, 1, d), lambda b, i: (b, 0, 0)),
                  vec, vec],
        out_specs=pl.BlockSpec((1, tm, d), lambda b, i: (b, i, 0)),
        out_shape=jax.ShapeDtypeStruct((bsz, seq, d), F32),
        compiler_params=_cparams(("parallel", "parallel")),
        name="outproj_ln",
    )(oa, ob, w_out, w_out, x, gate, ln_g.reshape(1, d), ln_b.reshape(1, d))


def _route(logits):
    lane = lax.broadcasted_iota(jnp.int32, logits.shape, 1)
    big = jnp.int32(2 ** 30)
    is_g = lane < N_GROUPS
    mg = jnp.max(jnp.where(is_g, logits, -jnp.inf), axis=-1, keepdims=True)
    g_idx = jnp.min(jnp.where(is_g & (logits == mg), lane, big), axis=-1, keepdims=True)
    p_g = 1.0 / jnp.sum(jnp.where(is_g, jnp.exp(logits - mg), 0.0), axis=-1, keepdims=True)
    lo = N_GROUPS + EXPERTS_PER_GROUP * g_idx
    in_grp = (lane >= lo) & (lane < lo + EXPERTS_PER_GROUP)
    v1 = jnp.max(jnp.where(in_grp, logits, -jnp.inf), axis=-1, keepdims=True)
    i1 = jnp.min(jnp.where(in_grp & (logits == v1), lane, big), axis=-1, keepdims=True)
    rest = in_grp & (lane != i1)
    v2 = jnp.max(jnp.where(rest, logits, -jnp.inf), axis=-1, keepdims=True)
    i2 = jnp.min(jnp.where(rest & (logits == v2), lane, big), axis=-1, keepdims=True)
    e2 = jnp.exp(v2 - v1)
    w1 = p_g / (1.0 + e2)
    w2 = p_g * e2 / (1.0 + e2)
    gates = jnp.where(lane == i1, w1, 0.0) + jnp.where(lane == i2, w2, 0.0)
    return jnp.where(lane == 0, g_idx.astype(F32), gates)


def _moe_route_kernel(x_ref, sc_ref, sh_ref, wr_ref, br_ref, h_ref, gate_ref):
    h = x_ref[0] * (1.0 + sc_ref[0]) + sh_ref[0]
    h_ref[0] = h.astype(BF16)
    logits = jnp.dot(h, wr_ref[...], precision=HIGHEST, preferred_element_type=F32) + br_ref[...]
    gate_ref[0] = _route(logits)


def _moe_route(x, sc, sh, w_router, b_router):
    bsz, seq, d = x.shape
    tm = _pick(seq, (512, 256))
    mod = pl.BlockSpec((1, 1, d), lambda b, i: (b, 0, 0))
    return pl.pallas_call(
        _moe_route_kernel,
        grid=(bsz, seq // tm),
        in_specs=[pl.BlockSpec((1, tm, d), lambda b, i: (b, i, 0)), mod, mod,
                  pl.BlockSpec((d, ROUTER_LANES), lambda b, i: (0, 0)),
                  pl.BlockSpec((1, ROUTER_LANES), lambda b, i: (0, 0))],
        out_specs=[pl.BlockSpec((1, tm, d), lambda b, i: (b, i, 0)),
                   pl.BlockSpec((1, tm, ROUTER_LANES), lambda b, i: (b, i, 0))],
        out_shape=[jax.ShapeDtypeStruct((bsz, seq, d), BF16),
                   jax.ShapeDtypeStruct((bsz, seq, ROUTER_LANES), F32)],
        compiler_params=_cparams(("parallel", "parallel")),
        name="moe_route",
    )(x, sc, sh, w_router, b_router)


def _moe_expert_kernel(h_ref, gate_ref, wg_ref, wu_ref, wd_ref, o_ref, pmt_ref, hs_ref, gs_ref, ys_ref, bounds_ref,
                       *, tm, chunk):
    e = pl.program_id(2)

    @pl.when(e == 0)
    def _():
        rec = gate_ref[0]
        lane = lax.broadcasted_iota(jnp.int32, (tm, ROUTER_LANES), 1)
        onehot = jnp.where((lane < N_GROUPS) & (lane == rec[:, 0:1].astype(jnp.int32)), 1.0, 0.0)
        below = (lax.broadcasted_iota(jnp.int32, (tm, tm), 1)
                 < lax.broadcasted_iota(jnp.int32, (tm, tm), 0)).astype(BF16)
        rank = jnp.dot(below, onehot.astype(BF16), preferred_element_type=F32)
        cnt = jnp.sum(onehot, axis=0, keepdims=True)
        padded = jnp.floor((cnt + (SORT_ALIGN - 1)) * (1.0 / SORT_ALIGN)) * SORT_ALIGN
        lane1 = lax.broadcasted_iota(jnp.int32, (1, ROUTER_LANES), 1)
        starts = jnp.zeros((1, ROUTER_LANES), F32)
        for g in range(N_GROUPS):
            first = jnp.sum(jnp.where(lane1 < g, padded, 0.0))
            size = jnp.sum(jnp.where(lane1 == g, cnt, 0.0))
            starts = starts + jnp.where(lane1 == g, first, 0.0)
            bounds_ref[2 * g] = first.astype(jnp.int32)
            bounds_ref[2 * g + 1] = (size.astype(jnp.int32) + chunk - 1) // chunk
        pos = jnp.sum(onehot * (rank + starts), axis=1, keepdims=True).astype(jnp.int32)
        pmt = (lax.broadcasted_iota(jnp.int32, pmt_ref.shape, 1) == pos).astype(BF16)
        pmt_ref[...] = pmt
        dn = (((0,), (0,)), ((), ()))
        hs_ref[...] = lax.dot_general(pmt, h_ref[0], dn, preferred_element_type=F32).astype(BF16)
        gs = jnp.zeros(gs_ref.shape, F32)
        for piece in _split3(rec):
            gs = gs + lax.dot_general(pmt, piece, dn, preferred_element_type=F32)
        gs_ref[...] = gs
        ys_ref[...] = jnp.zeros_like(ys_ref)

    grp = e // EXPERTS_PER_GROUP
    lane_c = lax.broadcasted_iota(jnp.int32, (chunk, ROUTER_LANES), 1)
    first_row = bounds_ref[2 * grp]

    def body(c, carry):
        rows = pl.ds(pl.multiple_of(first_row + c * chunk, SORT_ALIGN), chunk)
        hb = hs_ref[rows, :]
        a = jnp.dot(hb, wg_ref[0], preferred_element_type=F32)
        u = jnp.dot(hb, wu_ref[0], preferred_element_type=F32)
        ge = jnp.sum(jnp.where(lane_c == e + N_GROUPS, gs_ref[rows, :], 0.0), axis=-1, keepdims=True)
        act = (_silu(a) * u * ge).astype(BF16)
        ys_ref[rows, :] += jnp.dot(act, wd_ref[0], preferred_element_type=F32)
        return carry

    lax.fori_loop(0, bounds_ref[2 * grp + 1], body, 0)

    @pl.when(e == N_EXPERTS - 1)
    def _():
        o_ref[0] = jnp.dot(pmt_ref[...], ys_ref[...].astype(BF16), preferred_element_type=F32).astype(o_ref.dtype)


def _moe_experts(h, rec, w_gate, w_up, w_down):
    bsz, seq, d = h.shape
    tm = _pick(seq, (1024, 512))
    chunk = 128
    ps = tm + 2 * chunk
    assert N_GROUPS * (SORT_ALIGN - 1) + chunk <= 2 * chunk
    return pl.pallas_call(
        functools.partial(_moe_expert_kernel, tm=tm, chunk=chunk),
        grid=(bsz, seq // tm, N_EXPERTS),
        in_specs=[pl.BlockSpec((1, tm, d), lambda b, i, e: (b, i, 0)),
                  pl.BlockSpec((1, tm, ROUTER_LANES), lambda b, i, e: (b, i, 0)),
                  pl.BlockSpec((1, d, D_EXPERT), lambda b, i, e: (e, 0, 0)),
                  pl.BlockSpec((1, d, D_EXPERT), lambda b, i, e: (e, 0, 0)),
                  pl.BlockSpec((1, D_EXPERT, d), lambda b, i, e: (e, 0, 0))],
        out_specs=pl.BlockSpec((1, tm, d), lambda b, i, e: (b, i, 0)),
        out_shape=jax.ShapeDtypeStruct((bsz, seq, d), BF16),
        scratch_shapes=[pltpu.VMEM((tm, ps), BF16), pltpu.VMEM((ps, d), BF16), pltpu.VMEM((ps, ROUTER_LANES), F32),
                        pltpu.VMEM((ps, d), F32), pltpu.SMEM((2 * N_GROUPS,), jnp.int32)],
        compiler_params=_cparams(("parallel", "parallel", "arbitrary")),
        name="moe_experts",
    )(h, rec, w_gate, w_up, w_down)


def _resid_ln_kernel(x_ref, f_ref, g_ref, lng_ref, lnb_ref, o_ref):
    y = DEEPNORM_ALPHA * x_ref[0] + g_ref[0] * f_ref[0].astype(F32)
    o_ref[0] = _layer_norm(y, lng_ref[...], lnb_ref[...])


def _resid_ln(x, f, gate, ln_g, ln_b):
    bsz, seq, d = x.shape
    tm = _pick(seq, (512, 256))
    vec = pl.BlockSpec((1, d), lambda b, i: (0, 0))
    tile = pl.BlockSpec((1, tm, d), lambda b, i: (b, i, 0))
    return pl.pallas_call(
        _resid_ln_kernel,
        grid=(bsz, seq // tm),
        in_specs=[tile, tile, pl.BlockSpec((1, 1, d), lambda b, i: (b, 0, 0)), vec, vec],
        out_specs=tile,
        out_shape=jax.ShapeDtypeStruct((bsz, seq, d), F32),
        compiler_params=_cparams(("parallel", "parallel")),
        name="resid_ln",
    )(x, f, gate, ln_g.reshape(1, d), ln_b.reshape(1, d))


def _moe_ln(x, sc, sh, gate, w_router, b_router, w_gate, w_up, w_down, ln_g, ln_b):
    h, rec = _moe_route(x, sc, sh, w_router, b_router)
    f = _moe_experts(h, rec, w_gate, w_up, w_down)
    return _resid_ln(x, f, gate, ln_g, ln_b)


def _alibi_slopes(n):
    return jnp.asarray(2.0 ** (-8.0 * np.arange(1, n + 1) / n), dtype=F32)


def _colscale(q_ranges):
    cs = np.ones((1, PROJ_N), np.float32)
    for lo, hi, scale in q_ranges:
        cs[:, lo:hi] = scale
    return jnp.asarray(cs)


def _prep_params(p):
    q = dict(p)
    for name in ("w_in_ab", "w_out_ab", "w_in_cd", "w_out_cd", "w_gate", "w_up", "w_down"):
        q[name] = p[name].astype(BF16)
    pad = ROUTER_LANES - N_GROUPS - N_EXPERTS
    q["w_router"] = jnp.pad(jnp.concatenate([p["w_rg"], p["w_re"]], axis=-1), ((0, 0), (0, 0), (0, pad)))
    q["b_router"] = jnp.pad(jnp.concatenate([p["b_rg"], p["b_re"]], axis=-1), ((0, 0), (0, pad)))[:, None, :]
    q["na_bias"] = [_na_bias_table(p["rpb_d"][i]) for i in range(p["rpb_d"].shape[0])]
    return q


def _trunk(x, ada, p):
    for l in range(DEPTH):
        sh1, sc1, g1, sh2, sc2, g2 = jnp.split(ada[l][:, None, :], 6, axis=-1)
        i = l // 2
        if l % 2 == 0:
            lam_init = 0.8 - 0.6 * math.exp(-0.3 * l)
            proj = _proj(x, sc1, sh1, p["w_in_ab"][i], _colscale(((0, 1024, ATTN_SCALE), (1536, 2560, ATTN_SCALE * LOG2E))))
            oa = _win_attention(proj, p["sink_a"][i], _alibi_slopes(WIN_Q_HEADS))
            lamvec = jnp.stack([p["lam_q1"][i], p["lam_k1"][i], p["lam_q2"][i], p["lam_k2"][i]])
            ob = _diff_attention(proj, _alibi_slopes(DIFF_HEADS), lamvec, p["subln_g"][i][None, :], lam_init)
            w_out = p["w_out_ab"][i]
        else:
            proj = _proj(x, sc1, sh1, p["w_in_cd"][i], _colscale(((1536, 2560, ATTN_SCALE),)))
            qk = _normrope(proj, p["qnorm_c"][i], p["knorm_c"][i])
            oa = _gqa_attention(qk, proj)
            ob = _na_attention(proj, p["na_bias"][i])
            w_out = p["w_out_cd"][i]
        x = _outproj_ln(oa, ob, w_out, x, g1, p["ln_g"][l, 0], p["ln_b"][l, 0])
        x = _moe_ln(x, sc2, sh2, g2, p["w_router"][l], p["b_router"][l],
                    p["w_gate"][l], p["w_up"][l], p["w_down"][l], p["ln_g"][l, 1], p["ln_b"][l, 1])
    return x


def kernel(x_prompt, x_sample, c_prompt, c_sample, w_ada, b_ada, ln_g, ln_b, w_in_ab, w_out_ab, sink_a,
           lam_q1, lam_k1, lam_q2, lam_k2, subln_g, w_in_cd, w_out_cd, qnorm_c, knorm_c, rpb_d,
           w_rg, b_rg, w_re, b_re, w_gate, w_up, w_down):
    p = _prep_params(dict(
        ln_g=ln_g, ln_b=ln_b, w_in_ab=w_in_ab, w_out_ab=w_out_ab, sink_a=sink_a,
        lam_q1=lam_q1, lam_k1=lam_k1, lam_q2=lam_q2, lam_k2=lam_k2, subln_g=subln_g,
        w_in_cd=w_in_cd, w_out_cd=w_out_cd, qnorm_c=qnorm_c, knorm_c=knorm_c, rpb_d=rpb_d,
        w_rg=w_rg, b_rg=b_rg, w_re=w_re, b_re=b_re, w_gate=w_gate, w_up=w_up, w_down=w_down))
    nb_p, nb_s = c_prompt.shape[0], c_sample.shape[0]
    c_all = jnp.concatenate([c_prompt, c_sample, jnp.zeros((8 - nb_p - nb_s, D_MODEL), F32)], axis=0)
    ada = _ada(c_all, w_ada, b_ada)
    y_prompt = _trunk(x_prompt, ada[:, :nb_p], p)
    y_sample = _trunk(x_sample, ada[:, nb_p:nb_p + nb_s], p)
    return (y_prompt, y_sample)
```

```python
import functools
import math

import numpy as np
import jax
import jax.numpy as jnp
from jax import lax
from jax.experimental import pallas as pl
from jax.experimental.pallas import tpu as pltpu

F32 = jnp.float32
BF16 = jnp.bfloat16
HIGHEST = lax.Precision.HIGHEST

D_MODEL = 2048
DEPTH = 2
HEAD_DIM = 128
WINDOW = 128
GRID_W = 64
WIN_Q_HEADS = 8
WIN_KV_HEADS = 2
GQA_GROUP = 4
DIFF_HEADS = 4
DIFF_V_DIM = 256
AX_Q_HEADS = 8
AX_KV_HEADS = 2
NA_HEADS = 8
NA_ROWS = 8
NA_COLS = 16
ROPE_THETA = 10000.0
N_GROUPS = 4
EXPERTS_PER_GROUP = 4
N_EXPERTS = 16
D_EXPERT = 512
LN_EPS = 1e-5
RMS_EPS = 1e-6
NEG_INF = -1e30
DEEPNORM_ALPHA = (2 * DEPTH) ** 0.25
PROJ_N = 4608
ATTN_SCALE = HEAD_DIM ** -0.5
LOG2E = math.log2(math.e)
LANES = 128
ROUTER_LANES = LANES
SORT_ALIGN = 16
VMEM_LIMIT = 56 * 1024 * 1024

GQA_KEY_SUB = 256
DIFF_KEY_SUB = 512
NA_TQ = NA_ROWS * GRID_W
NA_WIN = 2 * NA_ROWS * GRID_W


def _cparams(sem):
    return pltpu.CompilerParams(dimension_semantics=sem, vmem_limit_bytes=VMEM_LIMIT)


def _pick(n, cands):
    for c in cands:
        if n % c == 0:
            return c
    raise ValueError(f"no tile in {cands} divides {n}")


def _layer_norm(y, g, b):
    mu = jnp.mean(y, axis=-1, keepdims=True)
    yc = y - mu
    var = jnp.mean(yc * yc, axis=-1, keepdims=True)
    return yc * lax.rsqrt(var + LN_EPS) * g + b


def _silu(a):
    return a / (1.0 + jnp.exp(-a))


def _ada_kernel(c_ref, w_ref, b_ref, o_ref):
    a = _silu(c_ref[...])
    o_ref[0] = jnp.dot(a, w_ref[0], precision=HIGHEST, preferred_element_type=F32) + b_ref[0]


def _ada(c_all, w_ada, b_ada):
    n = 6 * D_MODEL
    tn = 1024
    return pl.pallas_call(
        _ada_kernel,
        grid=(DEPTH, n // tn),
        in_specs=[pl.BlockSpec((8, D_MODEL), lambda l, j: (0, 0)),
                  pl.BlockSpec((1, D_MODEL, tn), lambda l, j: (l, 0, j)),
                  pl.BlockSpec((1, 1, tn), lambda l, j: (l, 0, j))],
        out_specs=pl.BlockSpec((1, 8, tn), lambda l, j: (l, 0, j)),
        out_shape=jax.ShapeDtypeStruct((DEPTH, 8, n), F32),
        compiler_params=_cparams(("parallel", "parallel")),
        name="ada",
    )(c_all, w_ada, b_ada.reshape(DEPTH, 1, n))


def _proj_kernel(x_ref, sc_ref, sh_ref, w_ref, cs_ref, o_ref, h_ref):
    @pl.when(pl.program_id(2) == 0)
    def _():
        h_ref[...] = (x_ref[0] * (1.0 + sc_ref[0]) + sh_ref[0]).astype(BF16)

    acc = jnp.dot(h_ref[...], w_ref[...], preferred_element_type=F32)
    o_ref[0] = (acc * cs_ref[...]).astype(o_ref.dtype)


def _proj(x, sc, sh, w, colscale):
    bsz, seq, d = x.shape
    n = w.shape[1]
    tm = _pick(seq, (1024, 512, 256))
    tn = 2304
    return pl.pallas_call(
        _proj_kernel,
        grid=(bsz, seq // tm, n // tn),
        in_specs=[pl.BlockSpec((1, tm, d), lambda b, i, j: (b, i, 0)),
                  pl.BlockSpec((1, 1, d), lambda b, i, j: (b, 0, 0)),
                  pl.BlockSpec((1, 1, d), lambda b, i, j: (b, 0, 0)),
                  pl.BlockSpec((d, tn), lambda b, i, j: (0, j)),
                  pl.BlockSpec((1, tn), lambda b, i, j: (0, j))],
        out_specs=pl.BlockSpec((1, tm, tn), lambda b, i, j: (b, i, j)),
        out_shape=jax.ShapeDtypeStruct((bsz, seq, n), BF16),
        scratch_shapes=[pltpu.VMEM((tm, d), BF16)],
        compiler_params=_cparams(("parallel", "parallel", "arbitrary")),
        name="proj",
    )(x, sc, sh, w, colscale)


def _win_kernel(sink_ref, slope_ref, q_ref, kp_ref, km_ref, kn_ref, vp_ref, vm_ref, vn_ref, o_ref, *, tq, seq):
    i = pl.program_id(1)
    kvh = pl.program_id(2)
    k = jnp.concatenate([kp_ref[0], km_ref[0], kn_ref[0]], axis=0)
    v = jnp.concatenate([vp_ref[0], vm_ref[0], vn_ref[0]], axis=0)
    w = tq + 2 * WINDOW
    qrel = lax.broadcasted_iota(jnp.int32, (tq, w), 0)
    krel = lax.broadcasted_iota(jnp.int32, (tq, w), 1) - WINDOW
    dist = jnp.abs(qrel - krel)
    kabs = krel + i * tq
    valid = (dist <= WINDOW) & (kabs >= 0) & (kabs < seq)
    distf = dist.astype(F32)
    for g in range(GQA_GROUP):
        h = kvh * GQA_GROUP + g
        qg = q_ref[0, :, g * HEAD_DIM:(g + 1) * HEAD_DIM]
        s = lax.dot_general(qg, k, (((1,), (1,)), ((), ())), preferred_element_type=F32)
        s = jnp.where(valid, s - slope_ref[h] * distf, NEG_INF)
        snk = sink_ref[h]
        m = jnp.maximum(jnp.max(s, axis=-1, keepdims=True), snk)
        p = jnp.exp(s - m)
        l = jnp.sum(p, axis=-1, keepdims=True) + jnp.exp(snk - m)
        o = jnp.dot(p.astype(BF16), v, preferred_element_type=F32) / l
        o_ref[0, :, g * HEAD_DIM:(g + 1) * HEAD_DIM] = o.astype(o_ref.dtype)


def _win_attention(proj, sink, slopes):
    bsz, seq, _ = proj.shape
    tq = 256
    r = tq // WINDOW
    nb = seq // WINDOW
    kcol = WIN_Q_HEADS
    vcol = WIN_Q_HEADS + WIN_KV_HEADS

    def prev(col):
        return pl.BlockSpec((1, WINDOW, HEAD_DIM), lambda b, i, k: (b, jnp.maximum(i * r - 1, 0), col + k))

    def main(col):
        return pl.BlockSpec((1, tq, HEAD_DIM), lambda b, i, k: (b, i, col + k))

    def nxt(col):
        return pl.BlockSpec((1, WINDOW, HEAD_DIM), lambda b, i, k: (b, jnp.minimum((i + 1) * r, nb - 1), col + k))

    smem = pl.BlockSpec(memory_space=pltpu.SMEM)
    return pl.pallas_call(
        functools.partial(_win_kernel, tq=tq, seq=seq),
        grid=(bsz, seq // tq, WIN_KV_HEADS),
        in_specs=[smem, smem,
                  pl.BlockSpec((1, tq, GQA_GROUP * HEAD_DIM), lambda b, i, k: (b, i, k)),
                  prev(kcol), main(kcol), nxt(kcol), prev(vcol), main(vcol), nxt(vcol)],
        out_specs=pl.BlockSpec((1, tq, GQA_GROUP * HEAD_DIM), lambda b, i, k: (b, i, k)),
        out_shape=jax.ShapeDtypeStruct((bsz, seq, WIN_Q_HEADS * HEAD_DIM), BF16),
        compiler_params=_cparams(("parallel", "parallel", "parallel")),
        name="win_attn",
    )(sink, slopes, proj, proj, proj, proj, proj, proj, proj)


def _split3(x):
    p1 = x.astype(BF16)
    r1 = x - p1.astype(F32)
    p2 = r1.astype(BF16)
    return p1, p2, (r1 - p2.astype(F32)).astype(BF16)


def _diff_kernel(slope_ref, q_ref, k1_ref, k2_ref, vt_ref, lam_ref, g_ref, o_ref, bm_ref, m_ref, l_ref, acc_ref,
                 *, tq, tk, cq, nk, lam_init):
    h = pl.program_id(1)
    i = pl.program_id(2)
    j = pl.program_id(3)
    sigma = slope_ref[h] * LOG2E

    @pl.when(j == 0)
    def _():
        m_ref[...] = jnp.full_like(m_ref, NEG_INF)
        l_ref[...] = jnp.zeros_like(l_ref)
        acc_ref[...] = jnp.zeros_like(acc_ref)
        dbase = (lax.broadcasted_iota(jnp.int32, (DIFF_KEY_SUB, cq), 1)
                 - lax.broadcasted_iota(jnp.int32, (DIFF_KEY_SUB, cq), 0)).astype(F32)
        bm_ref[...] = dbase * sigma

    def sweep(frame):
        for c in range(tq // cq):
            cols = slice(c * cq, (c + 1) * cq)
            for mi, k_ref in enumerate((k1_ref, k2_ref)):
                qs = q_ref[0, 0, mi, :, cols]
                m = m_ref[mi, :, cols]
                l = l_ref[mi, :, cols]
                acc = acc_ref[mi, :, cols]
                for kb in range(tk // DIFF_KEY_SUB):
                    rows = slice(kb * DIFF_KEY_SUB, (kb + 1) * DIFF_KEY_SUB)
                    soff = sigma * (i * tq + c * cq - j * tk - kb * DIFF_KEY_SUB).astype(F32)
                    s = jnp.dot(k_ref[0, rows, :], qs, preferred_element_type=F32)
                    t, shift = frame(s, soff)
                    m_new = jnp.maximum(m, jnp.max(t, axis=0, keepdims=True) + shift)
                    alpha = jnp.exp2(m - m_new)
                    p = jnp.exp2(t - (m_new - shift))
                    l = alpha * l + jnp.sum(p, axis=0, keepdims=True)
                    acc = alpha * acc + jnp.dot(vt_ref[0, 0, :, rows], p.astype(BF16), preferred_element_type=F32)
                    m = m_new
                m_ref[mi, :, cols] = m
                l_ref[mi, :, cols] = l
                acc_ref[mi, :, cols] = acc

    @pl.when(i * tq >= (j + 1) * tk)
    def _():
        sweep(lambda s, soff: (s - bm_ref[...], -soff))

    @pl.when((i + 1) * tq <= j * tk)
    def _():
        sweep(lambda s, soff: (s + bm_ref[...], soff))

    @pl.when((i * tq < (j + 1) * tk) & ((i + 1) * tq > j * tk))
    def _():
        sweep(lambda s, soff: (s - jnp.abs(bm_ref[...] + soff), 0.0))

    @pl.when(j == nk - 1)
    def _():
        lv = lam_ref[...]
        lam = (jnp.exp(jnp.sum(lv[0:1] * lv[1:2], axis=-1, keepdims=True))
               - jnp.exp(jnp.sum(lv[2:3] * lv[3:4], axis=-1, keepdims=True)) + lam_init)
        ot = acc_ref[0] / l_ref[0] - lam * (acc_ref[1] / l_ref[1])
        o = ot.T
        r = lax.rsqrt(jnp.mean(o * o, axis=-1, keepdims=True) + RMS_EPS)
        o_ref[0] = (o * r * g_ref[...] * (1.0 - lam_init)).astype(o_ref.dtype)


def _diff_attention(proj, slopes, lamvec, subln_g, lam_init):
    bsz, seq, _ = proj.shape
    tq = _pick(seq, (1024, 512))
    tk = _pick(seq, (2048, 1024, 512))
    cq = 256
    nk = seq // tk
    qt = proj[:, :, 1536:2560].reshape(bsz, seq, DIFF_HEADS, 2, HEAD_DIM).transpose(0, 2, 3, 4, 1)
    vt = proj[:, :, 3584:4608].reshape(bsz, seq, DIFF_HEADS, DIFF_V_DIM).transpose(0, 2, 3, 1)
    kcol = 2560 // HEAD_DIM
    smem = pl.BlockSpec(memory_space=pltpu.SMEM)

    def kspec(off):
        return pl.BlockSpec((1, tk, HEAD_DIM), lambda b, h, i, j: (b, j, kcol + 2 * h + off))

    return pl.pallas_call(
        functools.partial(_diff_kernel, tq=tq, tk=tk, cq=cq, nk=nk, lam_init=lam_init),
        grid=(bsz, DIFF_HEADS, seq // tq, nk),
        in_specs=[smem,
                  pl.BlockSpec((1, 1, 2, HEAD_DIM, tq), lambda b, h, i, j: (b, h, 0, 0, i)),
                  kspec(0), kspec(1),
                  pl.BlockSpec((1, 1, DIFF_V_DIM, tk), lambda b, h, i, j: (b, h, 0, j)),
                  pl.BlockSpec((4, HEAD_DIM), lambda b, h, i, j: (0, 0)),
                  pl.BlockSpec((1, DIFF_V_DIM), lambda b, h, i, j: (0, 0))],
        out_specs=pl.BlockSpec((1, tq, DIFF_V_DIM), lambda b, h, i, j: (b, i, h)),
        out_shape=jax.ShapeDtypeStruct((bsz, seq, DIFF_HEADS * DIFF_V_DIM), BF16),
        scratch_shapes=[pltpu.VMEM((DIFF_KEY_SUB, cq), F32),
                        pltpu.VMEM((2, 1, tq), F32), pltpu.VMEM((2, 1, tq), F32),
                        pltpu.VMEM((2, DIFF_V_DIM, tq), F32)],
        compiler_params=_cparams(("parallel", "parallel", "parallel", "arbitrary")),
        name="diff_attn",
    )(slopes, qt, proj, proj, vt, lamvec, subln_g)


def _normrope_kernel(x_ref, cos_ref, sin_ref, gq_ref, gk_ref, o_ref, *, tm):
    j = pl.program_id(2)
    is_q = j < AX_Q_HEADS // 2
    g = jnp.where(is_q, gq_ref[...], gk_ref[...])
    post = jnp.where(is_q, ATTN_SCALE * LOG2E, 1.0).astype(F32)
    lane = lax.broadcasted_iota(jnp.int32, (tm, HEAD_DIM), 1)
    first = (lane % 64) < 32
    cos = cos_ref[...]
    sin = sin_ref[...]
    for hh in range(2):
        x = x_ref[0, :, hh * HEAD_DIM:(hh + 1) * HEAD_DIM].astype(F32)
        xn = x * lax.rsqrt(jnp.mean(x * x, axis=-1, keepdims=True) + RMS_EPS) * g
        partner = jnp.where(first, pltpu.roll(xn, 96, 1), pltpu.roll(xn, 32, 1))
        y = (xn * cos + partner * sin) * post
        o_ref[0, :, hh * HEAD_DIM:(hh + 1) * HEAD_DIM] = y.astype(o_ref.dtype)


def _rope_tables(seq):
    half = HEAD_DIM // 2
    inv = jnp.asarray(1.0 / (ROPE_THETA ** (np.arange(0, half, 2) / half)), dtype=F32)
    t = jnp.arange(seq)
    ang_r = (t // GRID_W).astype(F32)[:, None] * inv[None, :]
    ang_c = (t % GRID_W).astype(F32)[:, None] * inv[None, :]
    cr, sr, cc, sn = jnp.cos(ang_r), jnp.sin(ang_r), jnp.cos(ang_c), jnp.sin(ang_c)
    return (jnp.concatenate([cr, cr, cc, cc], axis=-1), jnp.concatenate([-sr, sr, -sn, sn], axis=-1))


def _normrope(proj, qnorm_g, knorm_g):
    bsz, seq, _ = proj.shape
    tm = _pick(seq, (2048, 1024, 512))
    cos, sin = _rope_tables(seq)
    nw = (AX_Q_HEADS + AX_KV_HEADS) * HEAD_DIM
    return pl.pallas_call(
        functools.partial(_normrope_kernel, tm=tm),
        grid=(bsz, seq // tm, nw // 256),
        in_specs=[pl.BlockSpec((1, tm, 256), lambda b, i, j: (b, i, j)),
                  pl.BlockSpec((tm, HEAD_DIM), lambda b, i, j: (i, 0)),
                  pl.BlockSpec((tm, HEAD_DIM), lambda b, i, j: (i, 0)),
                  pl.BlockSpec((1, HEAD_DIM), lambda b, i, j: (0, 0)),
                  pl.BlockSpec((1, HEAD_DIM), lambda b, i, j: (0, 0))],
        out_specs=pl.BlockSpec((1, tm, 256), lambda b, i, j: (b, i, j)),
        out_shape=jax.ShapeDtypeStruct((bsz, seq, nw), BF16),
        compiler_params=_cparams(("parallel", "parallel", "parallel")),
        name="normrope",
    )(proj, cos, sin, qnorm_g.reshape(1, HEAD_DIM), knorm_g.reshape(1, HEAD_DIM))


def _gqa_kernel(q_ref, k_ref, vt_ref, o_ref, m_ref, l_ref, acc_ref, *, tq, cq, nk):
    j = pl.program_id(3)

    @pl.when(j == 0)
    def _():
        m_ref[...] = jnp.full_like(m_ref, NEG_INF)
        l_ref[...] = jnp.zeros_like(l_ref)
        acc_ref[...] = jnp.zeros_like(acc_ref)

    tk = k_ref.shape[1]
    per = tq // cq
    for c in range(GQA_GROUP * per):
        g, sb = divmod(c, per)
        cols = slice(c * cq, (c + 1) * cq)
        qs = q_ref[0, g, :, sb * cq:(sb + 1) * cq]
        m = m_ref[:, cols]
        l = l_ref[:, cols]
        acc = acc_ref[:, cols]
        for kb in range(tk // GQA_KEY_SUB):
            rows = slice(kb * GQA_KEY_SUB, (kb + 1) * GQA_KEY_SUB)
            s = jnp.dot(k_ref[0, rows, :], qs, preferred_element_type=F32)
            m_new = jnp.maximum(m, jnp.max(s, axis=0, keepdims=True))
            alpha = jnp.exp2(m - m_new)
            p = jnp.exp2(s - m_new)
            l = alpha * l + jnp.sum(p, axis=0, keepdims=True)
            acc = alpha * acc + jnp.dot(vt_ref[0, 0, :, rows], p.astype(BF16), preferred_element_type=F32)
            m = m_new
        m_ref[:, cols] = m
        l_ref[:, cols] = l
        acc_ref[:, cols] = acc

    @pl.when(j == nk - 1)
    def _():
        ot = acc_ref[...] / l_ref[...]
        for g in range(GQA_GROUP):
            o_ref[0, :, g * HEAD_DIM:(g + 1) * HEAD_DIM] = ot[:, g * tq:(g + 1) * tq].T.astype(o_ref.dtype)


def _gqa_attention(qk, proj):
    bsz, seq, _ = qk.shape
    tq = _pick(seq, (1024, 512))
    tk = _pick(seq, (2048, 1024, 512))
    cq = 128
    nk = seq // tk
    gw = GQA_GROUP * HEAD_DIM
    qt = qk[:, :, :AX_Q_HEADS * HEAD_DIM].reshape(bsz, seq, AX_Q_HEADS, HEAD_DIM).transpose(0, 2, 3, 1)
    vt = proj[:, :, 1280:1536].reshape(bsz, seq, AX_KV_HEADS, HEAD_DIM).transpose(0, 2, 3, 1)
    return pl.pallas_call(
        functools.partial(_gqa_kernel, tq=tq, cq=cq, nk=nk),
        grid=(bsz, AX_KV_HEADS, seq // tq, nk),
        in_specs=[pl.BlockSpec((1, GQA_GROUP, HEAD_DIM, tq), lambda b, h, i, j: (b, h, 0, i)),
                  pl.BlockSpec((1, tk, HEAD_DIM), lambda b, h, i, j: (b, j, AX_Q_HEADS + h)),
                  pl.BlockSpec((1, 1, HEAD_DIM, tk), lambda b, h, i, j: (b, h, 0, j))],
        out_specs=pl.BlockSpec((1, tq, gw), lambda b, h, i, j: (b, i, h)),
        out_shape=jax.ShapeDtypeStruct((bsz, seq, AX_Q_HEADS * HEAD_DIM), BF16),
        scratch_shapes=[pltpu.VMEM((1, GQA_GROUP * tq), F32), pltpu.VMEM((1, GQA_GROUP * tq), F32),
                        pltpu.VMEM((HEAD_DIM, GQA_GROUP * tq), F32)],
        compiler_params=_cparams(("parallel", "parallel", "parallel", "arbitrary")),
        name="gqa_attn",
    )(qt, qk, vt)


def _na_bias_table(rpb):
    qi = np.arange(NA_ROWS)[:, None, None, None]
    qc = np.arange(GRID_W)[None, :, None, None]
    kj = np.arange(2 * NA_ROWS)[None, None, :, None]
    kc = np.arange(GRID_W)[None, None, None, :]
    c_start = np.clip(qc - NA_COLS // 2, 0, GRID_W - NA_COLS)
    col_ok = (kc >= c_start) & (kc < c_start + NA_COLS)
    rel_c = np.clip(kc - qc + NA_COLS - 1, 0, 2 * NA_COLS - 2)
    onehot_c = jnp.asarray(rel_c[0, :, 0, :, None] == np.arange(2 * NA_COLS - 1), dtype=F32)
    tabs = []
    for off, clamp in ((0, np.maximum), (-NA_ROWS // 2, None), (-NA_ROWS, np.minimum)):
        rs = qi - NA_ROWS // 2
        if clamp is not None:
            rs = clamp(rs, 0)
        kr = off + kj
        ok = (kr >= rs) & (kr < rs + NA_ROWS) & col_ok
        rel_r = np.clip(kr - qi + NA_ROWS - 1, 0, 2 * NA_ROWS - 2)[:, 0, :, 0]
        onehot_r = jnp.asarray(rel_r[:, :, None] == np.arange(2 * NA_ROWS - 1), dtype=F32)
        bias = jnp.einsum("ija,hab,cdb->hicjd", onehot_r, rpb.astype(F32), onehot_c, precision=HIGHEST)
        tabs.append(jnp.where(ok[None], bias, NEG_INF).reshape(NA_HEADS, NA_TQ, NA_WIN))
    return jnp.stack(tabs)


def _na_kernel(q_ref, k_ref, v_ref, bias_ref, o_ref, *, rows):
    i = pl.program_id(2)
    ws = jnp.clip(i * NA_ROWS - NA_ROWS // 2, 0, rows - 2 * NA_ROWS)
    start = pl.multiple_of(ws * GRID_W, GRID_W)
    k = k_ref[0, pl.ds(start, NA_WIN), :]
    v = v_ref[0, pl.ds(start, NA_WIN), :]
    s = lax.dot_general(q_ref[0], k, (((1,), (1,)), ((), ())), preferred_element_type=F32) + bias_ref[0, 0]
    m = jnp.max(s, axis=-1, keepdims=True)
    p = jnp.exp(s - m)
    l = jnp.sum(p, axis=-1, keepdims=True)
    o_ref[0] = (jnp.dot(p.astype(BF16), v, preferred_element_type=F32) / l).astype(o_ref.dtype)


def _na_attention(proj, bias_tab):
    bsz, seq, _ = proj.shape
    rows = seq // GRID_W
    nq = seq // NA_TQ
    assert rows >= 3 * NA_ROWS and seq % NA_TQ == 0
    qcol, kcol, vcol = 1536 // HEAD_DIM, 2560 // HEAD_DIM, 3584 // HEAD_DIM

    def case(i):
        return jnp.where(i == 0, 0, jnp.where(i == nq - 1, 2, 1))

    return pl.pallas_call(
        functools.partial(_na_kernel, rows=rows),
        grid=(bsz, NA_HEADS, nq),
        in_specs=[pl.BlockSpec((1, NA_TQ, HEAD_DIM), lambda b, h, i: (b, i, qcol + h)),
                  pl.BlockSpec((1, seq, HEAD_DIM), lambda b, h, i: (b, 0, kcol + h)),
                  pl.BlockSpec((1, seq, HEAD_DIM), lambda b, h, i: (b, 0, vcol + h)),
                  pl.BlockSpec((1, 1, NA_TQ, NA_WIN), lambda b, h, i: (case(i), h, 0, 0))],
        out_specs=pl.BlockSpec((1, NA_TQ, HEAD_DIM), lambda b, h, i: (b, i, h)),
        out_shape=jax.ShapeDtypeStruct((bsz, seq, NA_HEADS * HEAD_DIM), BF16),
        compiler_params=_cparams(("parallel", "parallel", "arbitrary")),
        name="na_attn",
    )(proj, proj, proj, bias_tab)


def _outproj_kernel(oa_ref, ob_ref, wa_ref, wb_ref, x_ref, g_ref, lng_ref, lnb_ref, o_ref):
    m = (jnp.dot(oa_ref[0], wa_ref[...], preferred_element_type=F32)
         + jnp.dot(ob_ref[0], wb_ref[...], preferred_element_type=F32))
    y = DEEPNORM_ALPHA * x_ref[0] + g_ref[0] * m
    o_ref[0] = _layer_norm(y, lng_ref[...], lnb_ref[...])


def _outproj_ln(oa, ob, w_out, x, gate, ln_g, ln_b):
    bsz, seq, d = x.shape
    half = oa.shape[-1]
    tm = _pick(seq, (512, 256))
    vec = pl.BlockSpec((1, d), lambda b, i: (0, 0))
    return pl.pallas_call(
        _outproj_kernel,
        grid=(bsz, seq // tm),
        in_specs=[pl.BlockSpec((1, tm, half), lambda b, i: (b, i, 0)),
                  pl.BlockSpec((1, tm, half), lambda b, i: (b, i, 0)),
                  pl.BlockSpec((half, d), lambda b, i: (0, 0)),
                  pl.BlockSpec((half, d), lambda b, i: (1, 0)),
                  pl.BlockSpec((1, tm, d), lambda b, i: (b, i, 0)),
                  pl.BlockSpec((1, 1, d), lambda b, i: (b, 0, 0)),
                  vec, vec],
        out_specs=pl.BlockSpec((1, tm, d), lambda b, i: (b, i, 0)),
        out_shape=jax.ShapeDtypeStruct((bsz, seq, d), F32),
        compiler_params=_cparams(("parallel", "parallel")),
        name="outproj_ln",
    )(oa, ob, w_out, w_out, x, gate, ln_g.reshape(1, d), ln_b.reshape(1, d))


def _route(logits):
    lane = lax.broadcasted_iota(jnp.int32, logits.shape, 1)
    big = jnp.int32(2 ** 30)
    is_g = lane < N_GROUPS
    mg = jnp.max(jnp.where(is_g, logits, -jnp.inf), axis=-1, keepdims=True)
    g_idx = jnp.min(jnp.where(is_g & (logits == mg), lane, big), axis=-1, keepdims=True)
    p_g = 1.0 / jnp.sum(jnp.where(is_g, jnp.exp(logits - mg), 0.0), axis=-1, keepdims=True)
    lo = N_GROUPS + EXPERTS_PER_GROUP * g_idx
    in_grp = (lane >= lo) & (lane < lo + EXPERTS_PER_GROUP)
    v1 = jnp.max(jnp.where(in_grp, logits, -jnp.inf), axis=-1, keepdims=True)
    i1 = jnp.min(jnp.where(in_grp & (logits == v1), lane, big), axis=-1, keepdims=True)
    rest = in_grp & (lane != i1)
    v2 = jnp.max(jnp.where(rest, logits, -jnp.inf), axis=-1, keepdims=True)
    i2 = jnp.min(jnp.where(rest & (logits == v2), lane, big), axis=-1, keepdims=True)
    e2 = jnp.exp(v2 - v1)
    w1 = p_g / (1.0 + e2)
    w2 = p_g * e2 / (1.0 + e2)
    gates = jnp.where(lane == i1, w1, 0.0) + jnp.where(lane == i2, w2, 0.0)
    return jnp.where(lane == 0, g_idx.astype(F32), gates)


def _moe_route_kernel(x_ref, sc_ref, sh_ref, wr_ref, br_ref, h_ref, gate_ref):
    h = x_ref[0] * (1.0 + sc_ref[0]) + sh_ref[0]
    h_ref[0] = h.astype(BF16)
    logits = jnp.dot(h, wr_ref[...], precision=HIGHEST, preferred_element_type=F32) + br_ref[...]
    gate_ref[0] = _route(logits)


def _moe_route(x, sc, sh, w_router, b_router):
    bsz, seq, d = x.shape
    tm = _pick(seq, (512, 256))
    mod = pl.BlockSpec((1, 1, d), lambda b, i: (b, 0, 0))
    return pl.pallas_call(
        _moe_route_kernel,
        grid=(bsz, seq // tm),
        in_specs=[pl.BlockSpec((1, tm, d), lambda b, i: (b, i, 0)), mod, mod,
                  pl.BlockSpec((d, ROUTER_LANES), lambda b, i: (0, 0)),
                  pl.BlockSpec((1, ROUTER_LANES), lambda b, i: (0, 0))],
        out_specs=[pl.BlockSpec((1, tm, d), lambda b, i: (b, i, 0)),
                   pl.BlockSpec((1, tm, ROUTER_LANES), lambda b, i: (b, i, 0))],
        out_shape=[jax.ShapeDtypeStruct((bsz, seq, d), BF16),
                   jax.ShapeDtypeStruct((bsz, seq, ROUTER_LANES), F32)],
        compiler_params=_cparams(("parallel", "parallel")),
        name="moe_route",
    )(x, sc, sh, w_router, b_router)


def _moe_expert_kernel(h_ref, gate_ref, wg_ref, wu_ref, wd_ref, o_ref, pmt_ref, hs_ref, gs_ref, ys_ref, bounds_ref,
                       *, tm, chunk):
    e = pl.program_id(2)

    @pl.when(e == 0)
    def _():
        rec = gate_ref[0]
        lane = lax.broadcasted_iota(jnp.int32, (tm, ROUTER_LANES), 1)
        onehot = jnp.where((lane < N_GROUPS) & (lane == rec[:, 0:1].astype(jnp.int32)), 1.0, 0.0)
        below = (lax.broadcasted_iota(jnp.int32, (tm, tm), 1)
                 < lax.broadcasted_iota(jnp.int32, (tm, tm), 0)).astype(BF16)
        rank = jnp.dot(below, onehot.astype(BF16), preferred_element_type=F32)
        cnt = jnp.sum(onehot, axis=0, keepdims=True)
        padded = jnp.floor((cnt + (SORT_ALIGN - 1)) * (1.0 / SORT_ALIGN)) * SORT_ALIGN
        lane1 = lax.broadcasted_iota(jnp.int32, (1, ROUTER_LANES), 1)
        starts = jnp.zeros((1, ROUTER_LANES), F32)
        for g in range(N_GROUPS):
            first = jnp.sum(jnp.where(lane1 < g, padded, 0.0))
            size = jnp.sum(jnp.where(lane1 == g, cnt, 0.0))
            starts = starts + jnp.where(lane1 == g, first, 0.0)
            bounds_ref[2 * g] = first.astype(jnp.int32)
            bounds_ref[2 * g + 1] = (size.astype(jnp.int32) + chunk - 1) // chunk
        pos = jnp.sum(onehot * (rank + starts), axis=1, keepdims=True).astype(jnp.int32)
        pmt = (lax.broadcasted_iota(jnp.int32, pmt_ref.shape, 1) == pos).astype(BF16)
        pmt_ref[...] = pmt
        dn = (((0,), (0,)), ((), ()))
        hs_ref[...] = lax.dot_general(pmt, h_ref[0], dn, preferred_element_type=F32).astype(BF16)
        gs = jnp.zeros(gs_ref.shape, F32)
        for piece in _split3(rec):
            gs = gs + lax.dot_general(pmt, piece, dn, preferred_element_type=F32)
        gs_ref[...] = gs
        ys_ref[...] = jnp.zeros_like(ys_ref)

    grp = e // EXPERTS_PER_GROUP
    lane_c = lax.broadcasted_iota(jnp.int32, (chunk, ROUTER_LANES), 1)
    first_row = bounds_ref[2 * grp]

    def body(c, carry):
        rows = pl.ds(pl.multiple_of(first_row + c * chunk, SORT_ALIGN), chunk)
        hb = hs_ref[rows, :]
        a = jnp.dot(hb, wg_ref[0], preferred_element_type=F32)
        u = jnp.dot(hb, wu_ref[0], preferred_element_type=F32)
        ge = jnp.sum(jnp.where(lane_c == e + N_GROUPS, gs_ref[rows, :], 0.0), axis=-1, keepdims=True)
        act = (_silu(a) * u * ge).astype(BF16)
        ys_ref[rows, :] += jnp.dot(act, wd_ref[0], preferred_element_type=F32)
        return carry

    lax.fori_loop(0, bounds_ref[2 * grp + 1], body, 0)

    @pl.when(e == N_EXPERTS - 1)
    def _():
        o_ref[0] = jnp.dot(pmt_ref[...], ys_ref[...].astype(BF16), preferred_element_type=F32).astype(o_ref.dtype)


def _moe_experts(h, rec, w_gate, w_up, w_down):
    bsz, seq, d = h.shape
    tm = _pick(seq, (1024, 512))
    chunk = 128
    ps = tm + 2 * chunk
    assert N_GROUPS * (SORT_ALIGN - 1) + chunk <= 2 * chunk
    return pl.pallas_call(
        functools.partial(_moe_expert_kernel, tm=tm, chunk=chunk),
        grid=(bsz, seq // tm, N_EXPERTS),
        in_specs=[pl.BlockSpec((1, tm, d), lambda b, i, e: (b, i, 0)),
                  pl.BlockSpec((1, tm, ROUTER_LANES), lambda b, i, e: (b, i, 0)),
                  pl.BlockSpec((1, d, D_EXPERT), lambda b, i, e: (e, 0, 0)),
                  pl.BlockSpec((1, d, D_EXPERT), lambda b, i, e: (e, 0, 0)),
                  pl.BlockSpec((1, D_EXPERT, d), lambda b, i, e: (e, 0, 0))],
        out_specs=pl.BlockSpec((1, tm, d), lambda b, i, e: (b, i, 0)),
        out_shape=jax.ShapeDtypeStruct((bsz, seq, d), BF16),
        scratch_shapes=[pltpu.VMEM((tm, ps), BF16), pltpu.VMEM((ps, d), BF16), pltpu.VMEM((ps, ROUTER_LANES), F32),
                        pltpu.VMEM((ps, d), F32), pltpu.SMEM((2 * N_GROUPS,), jnp.int32)],
        compiler_params=_cparams(("parallel", "parallel", "arbitrary")),
        name="moe_experts",
    )(h, rec, w_gate, w_up, w_down)


def _resid_ln_kernel(x_ref, f_ref, g_ref, lng_ref, lnb_ref, o_ref):
    y = DEEPNORM_ALPHA * x_ref[0] + g_ref[0] * f_ref[0].astype(F32)
    o_ref[0] = _layer_norm(y, lng_ref[...], lnb_ref[...])


def _resid_ln(x, f, gate, ln_g, ln_b):
    bsz, seq, d = x.shape
    tm = _pick(seq, (512, 256))
    vec = pl.BlockSpec((1, d), lambda b, i: (0, 0))
    tile = pl.BlockSpec((1, tm, d), lambda b, i: (b, i, 0))
    return pl.pallas_call(
        _resid_ln_kernel,
        grid=(bsz, seq // tm),
        in_specs=[tile, tile, pl.BlockSpec((1, 1, d), lambda b, i: (b, 0, 0)), vec, vec],
        out_specs=tile,
        out_shape=jax.ShapeDtypeStruct((bsz, seq, d), F32),
        compiler_params=_cparams(("parallel", "parallel")),
        name="resid_ln",
    )(x, f, gate, ln_g.reshape(1, d), ln_b.reshape(1, d))


def _moe_ln(x, sc, sh, gate, w_router, b_router, w_gate, w_up, w_down, ln_g, ln_b):
    h, rec = _moe_route(x, sc, sh, w_router, b_router)
    f = _moe_experts(h, rec, w_gate, w_up, w_down)
    return _resid_ln(x, f, gate, ln_g, ln_b)


def _alibi_slopes(n):
    return jnp.asarray(2.0 ** (-8.0 * np.arange(1, n + 1) / n), dtype=F32)


def _colscale(q_ranges):
    cs = np.ones((1, PROJ_N), np.float32)
    for lo, hi, scale in q_ranges:
        cs[:, lo:hi] = scale
    return jnp.asarray(cs)


def _prep_params(p):
    q = dict(p)
    for name in ("w_in_ab", "w_out_ab", "w_in_cd", "w_out_cd", "w_gate", "w_up", "w_down"):
        q[name] = p[name].astype(BF16)
    pad = ROUTER_LANES - N_GROUPS - N_EXPERTS
    q["w_router"] = jnp.pad(jnp.concatenate([p["w_rg"], p["w_re"]], axis=-1), ((0, 0), (0, 0), (0, pad)))
    q["b_router"] = jnp.pad(jnp.concatenate([p["b_rg"], p["b_re"]], axis=-1), ((0, 0), (0, pad)))[:, None, :]
    q["na_bias"] = [_na_bias_table(p["rpb_d"][i]) for i in range(p["rpb_d"].shape[0])]
    return q


def _trunk(x, ada, p):
    for l in range(DEPTH):
        sh1, sc1, g1, sh2, sc2, g2 = jnp.split(ada[l][:, None, :], 6, axis=-1)
        i = l // 2
        if l % 2 == 0:
            lam_init = 0.8 - 0.6 * math.exp(-0.3 * l)
            proj = _proj(x, sc1, sh1, p["w_in_ab"][i], _colscale(((0, 1024, ATTN_SCALE), (1536, 2560, ATTN_SCALE * LOG2E))))
            oa = _win_attention(proj, p["sink_a"][i], _alibi_slopes(WIN_Q_HEADS))
            lamvec = jnp.stack([p["lam_q1"][i], p["lam_k1"][i], p["lam_q2"][i], p["lam_k2"][i]])
            ob = _diff_attention(proj, _alibi_slopes(DIFF_HEADS), lamvec, p["subln_g"][i][None, :], lam_init)
            w_out = p["w_out_ab"][i]
        else:
            proj = _proj(x, sc1, sh1, p["w_in_cd"][i], _colscale(((1536, 2560, ATTN_SCALE),)))
            qk = _normrope(proj, p["qnorm_c"][i], p["knorm_c"][i])
            oa = _gqa_attention(qk, proj)
            ob = _na_attention(proj, p["na_bias"][i])
            w_out = p["w_out_cd"][i]
        x = _outproj_ln(oa, ob, w_out, x, g1, p["ln_g"][l, 0], p["ln_b"][l, 0])
        x = _moe_ln(x, sc2, sh2, g2, p["w_router"][l], p["b_router"][l],
                    p["w_gate"][l], p["w_up"][l], p["w_down"][l], p["ln_g"][l, 1], p["ln_b"][l, 1])
    return x


def kernel(x_prompt, x_sample, c_prompt, c_sample, w_ada, b_ada, ln_g, ln_b, w_in_ab, w_out_ab, sink_a,
           lam_q1, lam_k1, lam_q2, lam_k2, subln_g, w_in_cd, w_out_cd, qnorm_c, knorm_c, rpb_d,
           w_rg, b_rg, w_re, b_re, w_gate, w_up, w_down):
    p = _prep_params(dict(
        ln_g=ln_g, ln_b=ln_b, w_in_ab=w_in_ab, w_out_ab=w_out_ab, sink_a=sink_a,
        lam_q1=lam_q1, lam_k1=lam_k1, lam_q2=lam_q2, lam_k2=lam_k2, subln_g=subln_g,
        w_in_cd=w_in_cd, w_out_cd=w_out_cd, qnorm_c=qnorm_c, knorm_c=knorm_c, rpb_d=rpb_d,
        w_rg=w_rg, b_rg=b_rg, w_re=w_re, b_re=b_re, w_gate=w_gate, w_up=w_up, w_down=w_down))
    nb_p, nb_s = c_prompt.shape[0], c_sample.shape[0]
    c_all = jnp.concatenate([c_prompt, c_sample, jnp.zeros((8 - nb_p - nb_s, D_MODEL), F32)], axis=0)
    ada = _ada(c_all, w_ada, b_ada)
    y_prompt = _trunk(x_prompt, ada[:, :nb_p], p)
    y_sample = _trunk(x_sample, ada[:, nb_p:nb_p + nb_s], p)
    return (y_prompt, y_sample)
```

```python
import functools
import math

import numpy as np
import jax
import jax.numpy as jnp
from jax import lax
from jax.experimental import pallas as pl
from jax.experimental.pallas import tpu as pltpu

F32 = jnp.float32
BF16 = jnp.bfloat16
HIGHEST = lax.Precision.HIGHEST

D_MODEL = 2048
DEPTH = 2
HEAD_DIM = 128
WINDOW = 128
GRID_W = 64
WIN_Q_HEADS = 8
WIN_KV_HEADS = 2
GQA_GROUP = 4
DIFF_HEADS = 4
DIFF_V_DIM = 256
AX_Q_HEADS = 8
AX_KV_HEADS = 2
NA_HEADS = 8
NA_ROWS = 8
NA_COLS = 16
ROPE_THETA = 10000.0
N_GROUPS = 4
EXPERTS_PER_GROUP = 4
N_EXPERTS = 16
D_EXPERT = 512
LN_EPS = 1e-5
RMS_EPS = 1e-6
NEG_INF = -1e30
DEEPNORM_ALPHA = (2 * DEPTH) ** 0.25
PROJ_N = 4608
ATTN_SCALE = HEAD_DIM ** -0.5
LOG2E = math.log2(math.e)
LANES = 128
ROUTER_LANES = LANES
SORT_ALIGN = 16
VMEM_LIMIT = 56 * 1024 * 1024

GQA_KEY_SUB = 256
DIFF_KEY_SUB = 512
NA_TQ = NA_ROWS * GRID_W
NA_WIN = 2 * NA_ROWS * GRID_W


def _cparams(sem):
    return pltpu.CompilerParams(dimension_semantics=sem, vmem_limit_bytes=VMEM_LIMIT)


def _pick(n, cands):
    for c in cands:
        if n % c == 0:
            return c
    raise ValueError(f"no tile in {cands} divides {n}")


def _layer_norm(y, g, b):
    mu = jnp.mean(y, axis=-1, keepdims=True)
    yc = y - mu
    var = jnp.mean(yc * yc, axis=-1, keepdims=True)
    return yc * lax.rsqrt(var + LN_EPS) * g + b


def _silu(a):
    return a / (1.0 + jnp.exp(-a))


def _ada_kernel(c_ref, w_ref, b_ref, o_ref):
    a = _silu(c_ref[...])
    o_ref[0] = jnp.dot(a, w_ref[0], precision=HIGHEST, preferred_element_type=F32) + b_ref[0]


def _ada(c_all, w_ada, b_ada):
    n = 6 * D_MODEL
    tn = 1024
    return pl.pallas_call(
        _ada_kernel,
        grid=(DEPTH, n // tn),
        in_specs=[pl.BlockSpec((8, D_MODEL), lambda l, j: (0, 0)),
                  pl.BlockSpec((1, D_MODEL, tn), lambda l, j: (l, 0, j)),
                  pl.BlockSpec((1, 1, tn), lambda l, j: (l, 0, j))],
        out_specs=pl.BlockSpec((1, 8, tn), lambda l, j: (l, 0, j)),
        out_shape=jax.ShapeDtypeStruct((DEPTH, 8, n), F32),
        compiler_params=_cparams(("parallel", "parallel")),
        name="ada",
    )(c_all, w_ada, b_ada.reshape(DEPTH, 1, n))


def _proj_kernel(x_ref, sc_ref, sh_ref, w_ref, cs_ref, o_ref, h_ref):
    @pl.when(pl.program_id(2) == 0)
    def _():
        h_ref[...] = (x_ref[0] * (1.0 + sc_ref[0]) + sh_ref[0]).astype(BF16)

    acc = jnp.dot(h_ref[...], w_ref[...], preferred_element_type=F32)
    o_ref[0] = (acc * cs_ref[...]).astype(o_ref.dtype)


def _proj(x, sc, sh, w, colscale):
    bsz, seq, d = x.shape
    n = w.shape[1]
    tm = _pick(seq, (1024, 512, 256))
    tn = 2304
    return pl.pallas_call(
        _proj_kernel,
        grid=(bsz, seq // tm, n // tn),
        in_specs=[pl.BlockSpec((1, tm, d), lambda b, i, j: (b, i, 0)),
                  pl.BlockSpec((1, 1, d), lambda b, i, j: (b, 0, 0)),
                  pl.BlockSpec((1, 1, d), lambda b, i, j: (b, 0, 0)),
                  pl.BlockSpec((d, tn), lambda b, i, j: (0, j)),
                  pl.BlockSpec((1, tn), lambda b, i, j: (0, j))],
        out_specs=pl.BlockSpec((1, tm, tn), lambda b, i, j: (b, i, j)),
        out_shape=jax.ShapeDtypeStruct((bsz, seq, n), BF16),
        scratch_shapes=[pltpu.VMEM((tm, d), BF16)],
        compiler_params=_cparams(("parallel", "parallel", "arbitrary")),
        name="proj",
    )(x, sc, sh, w, colscale)


def _win_kernel(sink_ref, slope_ref, q_ref, kp_ref, km_ref, kn_ref, vp_ref, vm_ref, vn_ref, o_ref, *, tq, seq):
    i = pl.program_id(1)
    kvh = pl.program_id(2)
    k = jnp.concatenate([kp_ref[0], km_ref[0], kn_ref[0]], axis=0)
    v = jnp.concatenate([vp_ref[0], vm_ref[0], vn_ref[0]], axis=0)
    w = tq + 2 * WINDOW
    qrel = lax.broadcasted_iota(jnp.int32, (tq, w), 0)
    krel = lax.broadcasted_iota(jnp.int32, (tq, w), 1) - WINDOW
    dist = jnp.abs(qrel - krel)
    kabs = krel + i * tq
    valid = (dist <= WINDOW) & (kabs >= 0) & (kabs < seq)
    distf = dist.astype(F32)
    for g in range(GQA_GROUP):
        h = kvh * GQA_GROUP + g
        qg = q_ref[0, :, g * HEAD_DIM:(g + 1) * HEAD_DIM]
        s = lax.dot_general(qg, k, (((1,), (1,)), ((), ())), preferred_element_type=F32)
        s = jnp.where(valid, s - slope_ref[h] * distf, NEG_INF)
        snk = sink_ref[h]
        m = jnp.maximum(jnp.max(s, axis=-1, keepdims=True), snk)
        p = jnp.exp(s - m)
        l = jnp.sum(p, axis=-1, keepdims=True) + jnp.exp(snk - m)
        o = jnp.dot(p.astype(BF16), v, preferred_element_type=F32) / l
        o_ref[0, :, g * HEAD_DIM:(g + 1) * HEAD_DIM] = o.astype(o_ref.dtype)


def _win_attention(proj, sink, slopes):
    bsz, seq, _ = proj.shape
    tq = 256
    r = tq // WINDOW
    nb = seq // WINDOW
    kcol = WIN_Q_HEADS
    vcol = WIN_Q_HEADS + WIN_KV_HEADS

    def prev(col):
        return pl.BlockSpec((1, WINDOW, HEAD_DIM), lambda b, i, k: (b, jnp.maximum(i * r - 1, 0), col + k))

    def main(col):
        return pl.BlockSpec((1, tq, HEAD_DIM), lambda b, i, k: (b, i, col + k))

    def nxt(col):
        return pl.BlockSpec((1, WINDOW, HEAD_DIM), lambda b, i, k: (b, jnp.minimum((i + 1) * r, nb - 1), col + k))

    smem = pl.BlockSpec(memory_space=pltpu.SMEM)
    return pl.pallas_call(
        functools.partial(_win_kernel, tq=tq, seq=seq),
        grid=(bsz, seq // tq, WIN_KV_HEADS),
        in_specs=[smem, smem,
                  pl.BlockSpec((1, tq, GQA_GROUP * HEAD_DIM), lambda b, i, k: (b, i, k)),
                  prev(kcol), main(kcol), nxt(kcol), prev(vcol), main(vcol), nxt(vcol)],
        out_specs=pl.BlockSpec((1, tq, GQA_GROUP * HEAD_DIM), lambda b, i, k: (b, i, k)),
        out_shape=jax.ShapeDtypeStruct((bsz, seq, WIN_Q_HEADS * HEAD_DIM), BF16),
        compiler_params=_cparams(("parallel", "parallel", "parallel")),
        name="win_attn",
    )(sink, slopes, proj, proj, proj, proj, proj, proj, proj)


def _split3(x):
    p1 = x.astype(BF16)
    r1 = x - p1.astype(F32)
    p2 = r1.astype(BF16)
    return p1, p2, (r1 - p2.astype(F32)).astype(BF16)


def _diff_kernel(slope_ref, q_ref, k1_ref, k2_ref, vt_ref, lam_ref, g_ref, o_ref, bm_ref, m_ref, l_ref, acc_ref,
                 *, tq, tk, cq, nk, lam_init):
    h = pl.program_id(1)
    i = pl.program_id(2)
    j = pl.program_id(3)
    sigma = slope_ref[h] * LOG2E

    @pl.when(j == 0)
    def _():
        m_ref[...] = jnp.full_like(m_ref, NEG_INF)
        l_ref[...] = jnp.zeros_like(l_ref)
        acc_ref[...] = jnp.zeros_like(acc_ref)
        dbase = (lax.broadcasted_iota(jnp.int32, (DIFF_KEY_SUB, cq), 1)
                 - lax.broadcasted_iota(jnp.int32, (DIFF_KEY_SUB, cq), 0)).astype(F32)
        bm_ref[...] = dbase * sigma

    def sweep(frame):
        for c in range(tq // cq):
            cols = slice(c * cq, (c + 1) * cq)
            for mi, k_ref in enumerate((k1_ref, k2_ref)):
                qs = q_ref[0, 0, mi, :, cols]
                m = m_ref[mi, :, cols]
                l = l_ref[mi, :, cols]
                acc = acc_ref[mi, :, cols]
                for kb in range(tk // DIFF_KEY_SUB):
                    rows = slice(kb * DIFF_KEY_SUB, (kb + 1) * DIFF_KEY_SUB)
                    soff = sigma * (i * tq + c * cq - j * tk - kb * DIFF_KEY_SUB).astype(F32)
                    s = jnp.dot(k_ref[0, rows, :], qs, preferred_element_type=F32)
                    t, shift = frame(s, soff)
                    m_new = jnp.maximum(m, jnp.max(t, axis=0, keepdims=True) + shift)
                    alpha = jnp.exp2(m - m_new)
                    p = jnp.exp2(t - (m_new - shift))
                    l = alpha * l + jnp.sum(p, axis=0, keepdims=True)
                    acc = alpha * acc + jnp.dot(vt_ref[0, 0, :, rows], p.astype(BF16), preferred_element_type=F32)
                    m = m_new
                m_ref[mi, :, cols] = m
                l_ref[mi, :, cols] = l
                acc_ref[mi, :, cols] = acc

    @pl.when(i * tq >= (j + 1) * tk)
    def _():
        sweep(lambda s, soff: (s - bm_ref[...], -soff))

    @pl.when((i + 1) * tq <= j * tk)
    def _():
        sweep(lambda s, soff: (s + bm_ref[...], soff))

    @pl.when((i * tq < (j + 1) * tk) & ((i + 1) * tq > j * tk))
    def _():
        sweep(lambda s, soff: (s - jnp.abs(bm_ref[...] + soff), 0.0))

    @pl.when(j == nk - 1)
    def _():
        lv = lam_ref[...]
        lam = (jnp.exp(jnp.sum(lv[0:1] * lv[1:2], axis=-1, keepdims=True))
               - jnp.exp(jnp.sum(lv[2:3] * lv[3:4], axis=-1, keepdims=True)) + lam_init)
        ot = acc_ref[0] / l_ref[0] - lam * (acc_ref[1] / l_ref[1])
        o = ot.T
        r = lax.rsqrt(jnp.mean(o * o, axis=-1, keepdims=True) + RMS_EPS)
        o_ref[0] = (o * r * g_ref[...] * (1.0 - lam_init)).astype(o_ref.dtype)


def _diff_attention(proj, slopes, lamvec, subln_g, lam_init):
    bsz, seq, _ = proj.shape
    tq = _pick(seq, (1024, 512))
    tk = _pick(seq, (2048, 1024, 512))
    cq = 256
    nk = seq // tk
    qt = proj[:, :, 1536:2560].reshape(bsz, seq, DIFF_HEADS, 2, HEAD_DIM).transpose(0, 2, 3, 4, 1)
    vt = proj[:, :, 3584:4608].reshape(bsz, seq, DIFF_HEADS, DIFF_V_DIM).transpose(0, 2, 3, 1)
    kcol = 2560 // HEAD_DIM
    smem = pl.BlockSpec(memory_space=pltpu.SMEM)

    def kspec(off):
        return pl.BlockSpec((1, tk, HEAD_DIM), lambda b, h, i, j: (b, j, kcol + 2 * h + off))

    return pl.pallas_call(
        functools.partial(_diff_kernel, tq=tq, tk=tk, cq=cq, nk=nk, lam_init=lam_init),
        grid=(bsz, DIFF_HEADS, seq // tq, nk),
        in_specs=[smem,
                  pl.BlockSpec((1, 1, 2, HEAD_DIM, tq), lambda b, h, i, j: (b, h, 0, 0, i)),
                  kspec(0), kspec(1),
                  pl.BlockSpec((1, 1, DIFF_V_DIM, tk), lambda b, h, i, j: (b, h, 0, j)),
                  pl.BlockSpec((4, HEAD_DIM), lambda b, h, i, j: (0, 0)),
                  pl.BlockSpec((1, DIFF_V_DIM), lambda b, h, i, j: (0, 0))],
        out_specs=pl.BlockSpec((1, tq, DIFF_V_DIM), lambda b, h, i, j: (b, i, h)),
        out_shape=jax.ShapeDtypeStruct((bsz, seq, DIFF_HEADS * DIFF_V_DIM), BF16),
        scratch_shapes=[pltpu.VMEM((DIFF_KEY_SUB, cq), F32),
                        pltpu.VMEM((2, 1, tq), F32), pltpu.VMEM((2, 1, tq), F32),
                        pltpu.VMEM((2, DIFF_V_DIM, tq), F32)],
        compiler_params=_cparams(("parallel", "parallel", "parallel", "arbitrary")),
        name="diff_attn",
    )(slopes, qt, proj, proj, vt, lamvec, subln_g)


def _normrope_kernel(x_ref, cos_ref, sin_ref, gq_ref, gk_ref, o_ref, *, tm):
    j = pl.program_id(2)
    is_q = j < AX_Q_HEADS // 2
    g = jnp.where(is_q, gq_ref[...], gk_ref[...])
    post = jnp.where(is_q, ATTN_SCALE * LOG2E, 1.0).astype(F32)
    lane = lax.broadcasted_iota(jnp.int32, (tm, HEAD_DIM), 1)
    first = (lane % 64) < 32
    cos = cos_ref[...]
    sin = sin_ref[...]
    for hh in range(2):
        x = x_ref[0, :, hh * HEAD_DIM:(hh + 1) * HEAD_DIM].astype(F32)
        xn = x * lax.rsqrt(jnp.mean(x * x, axis=-1, keepdims=True) + RMS_EPS) * g
        partner = jnp.where(first, pltpu.roll(xn, 96, 1), pltpu.roll(xn, 32, 1))
        y = (xn * cos + partner * sin) * post
        o_ref[0, :, hh * HEAD_DIM:(hh + 1) * HEAD_DIM] = y.astype(o_ref.dtype)


def _rope_tables(seq):
    half = HEAD_DIM // 2
    inv = jnp.asarray(1.0 / (ROPE_THETA ** (np.arange(0, half, 2) / half)), dtype=F32)
    t = jnp.arange(seq)
    ang_r = (t // GRID_W).astype(F32)[:, None] * inv[None, :]
    ang_c = (t % GRID_W).astype(F32)[:, None] * inv[None, :]
    cr, sr, cc, sn = jnp.cos(ang_r), jnp.sin(ang_r), jnp.cos(ang_c), jnp.sin(ang_c)
    return (jnp.concatenate([cr, cr, cc, cc], axis=-1), jnp.concatenate([-sr, sr, -sn, sn], axis=-1))


def _normrope(proj, qnorm_g, knorm_g):
    bsz, seq, _ = proj.shape
    tm = _pick(seq, (2048, 1024, 512))
    cos, sin = _rope_tables(seq)
    nw = (AX_Q_HEADS + AX_KV_HEADS) * HEAD_DIM
    return pl.pallas_call(
        functools.partial(_normrope_kernel, tm=tm),
        grid=(bsz, seq // tm, nw // 256),
        in_specs=[pl.BlockSpec((1, tm, 256), lambda b, i, j: (b, i, j)),
                  pl.BlockSpec((tm, HEAD_DIM), lambda b, i, j: (i, 0)),
                  pl.BlockSpec((tm, HEAD_DIM), lambda b, i, j: (i, 0)),
                  pl.BlockSpec((1, HEAD_DIM), lambda b, i, j: (0, 0)),
                  pl.BlockSpec((1, HEAD_DIM), lambda b, i, j: (0, 0))],
        out_specs=pl.BlockSpec((1, tm, 256), lambda b, i, j: (b, i, j)),
        out_shape=jax.ShapeDtypeStruct((bsz, seq, nw), BF16),
        compiler_params=_cparams(("parallel", "parallel", "parallel")),
        name="normrope",
    )(proj, cos, sin, qnorm_g.reshape(1, HEAD_DIM), knorm_g.reshape(1, HEAD_DIM))


def _gqa_kernel(q_ref, k_ref, vt_ref, o_ref, m_ref, l_ref, acc_ref, *, tq, cq, nk):
    j = pl.program_id(3)

    @pl.when(j == 0)
    def _():
        m_ref[...] = jnp.full_like(m_ref, NEG_INF)
        l_ref[...] = jnp.zeros_like(l_ref)
        acc_ref[...] = jnp.zeros_like(acc_ref)

    tk = k_ref.shape[1]
    per = tq // cq
    for c in range(GQA_GROUP * per):
        g, sb = divmod(c, per)
        cols = slice(c * cq, (c + 1) * cq)
        qs = q_ref[0, g, :, sb * cq:(sb + 1) * cq]
        m = m_ref[:, cols]
        l = l_ref[:, cols]
        acc = acc_ref[:, cols]
        for kb in range(tk // GQA_KEY_SUB):
            rows = slice(kb * GQA_KEY_SUB, (kb + 1) * GQA_KEY_SUB)
            s = jnp.dot(k_ref[0, rows, :], qs, preferred_element_type=F32)
            m_new = jnp.maximum(m, jnp.max(s, axis=0, keepdims=True))
            alpha = jnp.exp2(m - m_new)
            p = jnp.exp2(s - m_new)
            l = alpha * l + jnp.sum(p, axis=0, keepdims=True)
            acc = alpha * acc + jnp.dot(vt_ref[0, 0, :, rows], p.astype(BF16), preferred_element_type=F32)
            m = m_new
        m_ref[:, cols] = m
        l_ref[:, cols] = l
        acc_ref[:, cols] = acc

    @pl.when(j == nk - 1)
    def _():
        ot = acc_ref[...] / l_ref[...]
        for g in range(GQA_GROUP):
            o_ref[0, :, g * HEAD_DIM:(g + 1) * HEAD_DIM] = ot[:, g * tq:(g + 1) * tq].T.astype(o_ref.dtype)


def _gqa_attention(qk, proj):
    bsz, seq, _ = qk.shape
    tq = _pick(seq, (1024, 512))
    tk = _pick(seq, (2048, 1024, 512))
    cq = 128
    nk = seq // tk
    gw = GQA_GROUP * HEAD_DIM
    qt = qk[:, :, :AX_Q_HEADS * HEAD_DIM].reshape(bsz, seq, AX_Q_HEADS, HEAD_DIM).transpose(0, 2, 3, 1)
    vt = proj[:, :, 1280:1536].reshape(bsz, seq, AX_KV_HEADS, HEAD_DIM).transpose(0, 2, 3, 1)
    return pl.pallas_call(
        functools.partial(_gqa_kernel, tq=tq, cq=cq, nk=nk),
        grid=(bsz, AX_KV_HEADS, seq // tq, nk),
        in_specs=[pl.BlockSpec((1, GQA_GROUP, HEAD_DIM, tq), lambda b, h, i, j: (b, h, 0, i)),
                  pl.BlockSpec((1, tk, HEAD_DIM), lambda b, h, i, j: (b, j, AX_Q_HEADS + h)),
                  pl.BlockSpec((1, 1, HEAD_DIM, tk), lambda b, h, i, j: (b, h, 0, j))],
        out_specs=pl.BlockSpec((1, tq, gw), lambda b, h, i, j: (b, i, h)),
        out_shape=jax.ShapeDtypeStruct((bsz, seq, AX_Q_HEADS * HEAD_DIM), BF16),
        scratch_shapes=[pltpu.VMEM((1, GQA_GROUP * tq), F32), pltpu.VMEM((1, GQA_GROUP * tq), F32),
                        pltpu.VMEM((HEAD_DIM, GQA_GROUP * tq), F32)],
        compiler_params=_cparams(("parallel", "parallel", "parallel", "arbitrary")),
        name="gqa_attn",
    )(qt, qk, vt)


def _na_bias_table(rpb):
    qi = np.arange(NA_ROWS)[:, None, None, None]
    qc = np.arange(GRID_W)[None, :, None, None]
    kj = np.arange(2 * NA_ROWS)[None, None, :, None]
    kc = np.arange(GRID_W)[None, None, None, :]
    c_start = np.clip(qc - NA_COLS // 2, 0, GRID_W - NA_COLS)
    col_ok = (kc >= c_start) & (kc < c_start + NA_COLS)
    rel_c = np.clip(kc - qc + NA_COLS - 1, 0, 2 * NA_COLS - 2)
    onehot_c = jnp.asarray(rel_c[0, :, 0, :, None] == np.arange(2 * NA_COLS - 1), dtype=F32)
    tabs = []
    for off, clamp in ((0, np.maximum), (-NA_ROWS // 2, None), (-NA_ROWS, np.minimum)):
        rs = qi - NA_ROWS // 2
        if clamp is not None:
            rs = clamp(rs, 0)
        kr = off + kj
        ok = (kr >= rs) & (kr < rs + NA_ROWS) & col_ok
        rel_r = np.clip(kr - qi + NA_ROWS - 1, 0, 2 * NA_ROWS - 2)[:, 0, :, 0]
        onehot_r = jnp.asarray(rel_r[:, :, None] == np.arange(2 * NA_ROWS - 1), dtype=F32)
        bias = jnp.einsum("ija,hab,cdb->hicjd", onehot_r, rpb.astype(F32), onehot_c, precision=HIGHEST)
        tabs.append(jnp.where(ok[None], bias, NEG_INF).reshape(NA_HEADS, NA_TQ, NA_WIN))
    return jnp.stack(tabs)


def _na_kernel(q_ref, k_ref, v_ref, bias_ref, o_ref, *, rows):
    i = pl.program_id(2)
    ws = jnp.clip(i * NA_ROWS - NA_ROWS // 2, 0, rows - 2 * NA_ROWS)
    start = pl.multiple_of(ws * GRID_W, GRID_W)
    k = k_ref[0, pl.ds(start, NA_WIN), :]
    v = v_ref[0, pl.ds(start, NA_WIN), :]
    s = lax.dot_general(q_ref[0], k, (((1,), (1,)), ((), ())), preferred_element_type=F32) + bias_ref[0, 0]
    m = jnp.max(s, axis=-1, keepdims=True)
    p = jnp.exp(s - m)
    l = jnp.sum(p, axis=-1, keepdims=True)
    o_ref[0] = (jnp.dot(p.astype(BF16), v, preferred_element_type=F32) / l).astype(o_ref.dtype)


def _na_attention(proj, bias_tab):
    bsz, seq, _ = proj.shape
    rows = seq // GRID_W
    nq = seq // NA_TQ
    assert rows >= 3 * NA_ROWS and seq % NA_TQ == 0
    qcol, kcol, vcol = 1536 // HEAD_DIM, 2560 // HEAD_DIM, 3584 // HEAD_DIM

    def case(i):
        return jnp.where(i == 0, 0, jnp.where(i == nq - 1, 2, 1))

    return pl.pallas_call(
        functools.partial(_na_kernel, rows=rows),
        grid=(bsz, NA_HEADS, nq),
        in_specs=[pl.BlockSpec((1, NA_TQ, HEAD_DIM), lambda b, h, i: (b, i, qcol + h)),
                  pl.BlockSpec((1, seq, HEAD_DIM), lambda b, h, i: (b, 0, kcol + h)),
                  pl.BlockSpec((1, seq, HEAD_DIM), lambda b, h, i: (b, 0, vcol + h)),
                  pl.BlockSpec((1, 1, NA_TQ, NA_WIN), lambda b, h, i: (case(i), h, 0, 0))],
        out_specs=pl.BlockSpec((1, NA_TQ, HEAD_DIM), lambda b, h, i: (b, i, h)),
        out_shape=jax.ShapeDtypeStruct((bsz, seq, NA_HEADS * HEAD_DIM), BF16),
        compiler_params=_cparams(("parallel", "parallel", "arbitrary")),
        name="na_attn",
    )(proj, proj, proj, bias_tab)


def _outproj_kernel(oa_ref, ob_ref, wa_ref, wb_ref, x_ref, g_ref, lng_ref, lnb_ref, o_ref):
    m = (jnp.dot(oa_ref[0], wa_ref[...], preferred_element_type=F32)
         + jnp.dot(ob_ref[0], wb_ref[...], preferred_element_type=F32))
    y = DEEPNORM_ALPHA * x_ref[0] + g_ref[0] * m
    o_ref[0] = _layer_norm(y, lng_ref[...], lnb_ref[...])


def _outproj_ln(oa, ob, w_out, x, gate, ln_g, ln_b):
    bsz, seq, d = x.shape
    half = oa.shape[-1]
    tm = _pick(seq, (512, 256))
    vec = pl.BlockSpec((1, d), lambda b, i: (0, 0))
    return pl.pallas_call(
        _outproj_kernel,
        grid=(bsz, seq // tm),
        in_specs=[pl.BlockSpec((1, tm, half), lambda b, i: (b, i, 0)),
                  pl.BlockSpec((1, tm, half), lambda b, i: (b, i, 0)),
                  pl.BlockSpec((half, d), lambda b, i: (0, 0)),
                  pl.BlockSpec((half, d), lambda b, i: (1, 0)),
                  pl.BlockSpec((1, tm, d), lambda b, i: (b, i, 0)),
                  pl.BlockSpec((1, 1, d), lambda b, i: (b, 0, 0)),
                  vec, vec],
        out_specs=pl.BlockSpec((1, tm, d), lambda b, i: (b, i, 0)),
        out_shape=jax.ShapeDtypeStruct((bsz, seq, d), F32),
        compiler_params=_cparams(("parallel", "parallel")),
        name="outproj_ln",
    )(oa, ob, w_out, w_out, x, gate, ln_g.reshape(1, d), ln_b.reshape(1, d))


def _route(logits):
    lane = lax.broadcasted_iota(jnp.int32, logits.shape, 1)
    big = jnp.int32(2 ** 30)
    is_g = lane < N_GROUPS
    mg = jnp.max(jnp.where(is_g, logits, -jnp.inf), axis=-1, keepdims=True)
    g_idx = jnp.min(jnp.where(is_g & (logits == mg), lane, big), axis=-1, keepdims=True)
    p_g = 1.0 / jnp.sum(jnp.where(is_g, jnp.exp(logits - mg), 0.0), axis=-1, keepdims=True)
    lo = N_GROUPS + EXPERTS_PER_GROUP * g_idx
    in_grp = (lane >= lo) & (lane < lo + EXPERTS_PER_GROUP)
    v1 = jnp.max(jnp.where(in_grp, logits, -jnp.inf), axis=-1, keepdims=True)
    i1 = jnp.min(jnp.where(in_grp & (logits == v1), lane, big), axis=-1, keepdims=True)
    rest = in_grp & (lane != i1)
    v2 = jnp.max(jnp.where(rest, logits, -jnp.inf), axis=-1, keepdims=True)
    i2 = jnp.min(jnp.where(rest & (logits == v2), lane, big), axis=-1, keepdims=True)
    e2 = jnp.exp(v2 - v1)
    w1 = p_g / (1.0 + e2)
    w2 = p_g * e2 / (1.0 + e2)
    gates = jnp.where(lane == i1, w1, 0.0) + jnp.where(lane == i2, w2, 0.0)
    return jnp.where(lane == 0, g_idx.astype(F32), gates)


def _moe_route_kernel(x_ref, sc_ref, sh_ref, wr_ref, br_ref, h_ref, gate_ref):
    h = x_ref[0] * (1.0 + sc_ref[0]) + sh_ref[0]
    h_hi = h.astype(BF16)
    h_ref[0] = h_hi
    h_lo = (h - h_hi.astype(F32)).astype(BF16)
    logits = (jnp.dot(h_hi, wr_ref[0], preferred_element_type=F32)
              + jnp.dot(h_hi, wr_ref[1], preferred_element_type=F32)
              + jnp.dot(h_lo, wr_ref[0], preferred_element_type=F32)) + br_ref[...]
    gate_ref[0] = _route(logits)


def _moe_route(x, sc, sh, w_router, b_router):
    bsz, seq, d = x.shape
    tm = _pick(seq, (512, 256))
    mod = pl.BlockSpec((1, 1, d), lambda b, i: (b, 0, 0))
    return pl.pallas_call(
        _moe_route_kernel,
        grid=(bsz, seq // tm),
        in_specs=[pl.BlockSpec((1, tm, d), lambda b, i: (b, i, 0)), mod, mod,
                  pl.BlockSpec((2, d, ROUTER_LANES), lambda b, i: (0, 0, 0)),
                  pl.BlockSpec((1, ROUTER_LANES), lambda b, i: (0, 0))],
        out_specs=[pl.BlockSpec((1, tm, d), lambda b, i: (b, i, 0)),
                   pl.BlockSpec((1, tm, ROUTER_LANES), lambda b, i: (b, i, 0))],
        out_shape=[jax.ShapeDtypeStruct((bsz, seq, d), BF16),
                   jax.ShapeDtypeStruct((bsz, seq, ROUTER_LANES), F32)],
        compiler_params=_cparams(("parallel", "parallel")),
        name="moe_route",
    )(x, sc, sh, w_router, b_router)


def _moe_expert_kernel(h_ref, gate_ref, wg_ref, wu_ref, wd_ref, o_ref, pmt_ref, hs_ref, gs_ref, ys_ref, bounds_ref,
                       *, tm, chunk):
    e = pl.program_id(2)

    @pl.when(e == 0)
    def _():
        rec = gate_ref[0]
        lane = lax.broadcasted_iota(jnp.int32, (tm, ROUTER_LANES), 1)
        onehot = jnp.where((lane < N_GROUPS) & (lane == rec[:, 0:1].astype(jnp.int32)), 1.0, 0.0)
        below = (lax.broadcasted_iota(jnp.int32, (tm, tm), 1)
                 < lax.broadcasted_iota(jnp.int32, (tm, tm), 0)).astype(BF16)
        rank = jnp.dot(below, onehot.astype(BF16), preferred_element_type=F32)
        cnt = jnp.sum(onehot, axis=0, keepdims=True)
        padded = jnp.floor((cnt + (SORT_ALIGN - 1)) * (1.0 / SORT_ALIGN)) * SORT_ALIGN
        lane1 = lax.broadcasted_iota(jnp.int32, (1, ROUTER_LANES), 1)
        starts = jnp.zeros((1, ROUTER_LANES), F32)
        for g in range(N_GROUPS):
            first = jnp.sum(jnp.where(lane1 < g, padded, 0.0))
            size = jnp.sum(jnp.where(lane1 == g, cnt, 0.0))
            starts = starts + jnp.where(lane1 == g, first, 0.0)
            bounds_ref[2 * g] = first.astype(jnp.int32)
            bounds_ref[2 * g + 1] = (size.astype(jnp.int32) + chunk - 1) // chunk
        pos = jnp.sum(onehot * (rank + starts), axis=1, keepdims=True).astype(jnp.int32)
        pmt = (lax.broadcasted_iota(jnp.int32, pmt_ref.shape, 1) == pos).astype(BF16)
        pmt_ref[...] = pmt
        dn = (((0,), (0,)), ((), ()))
        hs_ref[...] = lax.dot_general(pmt, h_ref[0], dn, preferred_element_type=F32).astype(BF16)
        gs = jnp.zeros(gs_ref.shape, F32)
        for piece in _split3(rec):
            gs = gs + lax.dot_general(pmt, piece, dn, preferred_element_type=F32)
        gs_ref[...] = gs
        ys_ref[...] = jnp.zeros_like(ys_ref)

    grp = e // EXPERTS_PER_GROUP
    lane_c = lax.broadcasted_iota(jnp.int32, (chunk, ROUTER_LANES), 1)
    first_row = bounds_ref[2 * grp]

    def body(c, carry):
        rows = pl.ds(pl.multiple_of(first_row + c * chunk, SORT_ALIGN), chunk)
        hb = hs_ref[rows, :]
        a = jnp.dot(hb, wg_ref[0], preferred_element_type=F32)
        u = jnp.dot(hb, wu_ref[0], preferred_element_type=F32)
        ge = jnp.sum(jnp.where(lane_c == e + N_GROUPS, gs_ref[rows, :], 0.0), axis=-1, keepdims=True)
        act = (_silu(a) * u * ge).astype(BF16)
        ys_ref[rows, :] += jnp.dot(act, wd_ref[0], preferred_element_type=F32)
        return carry

    lax.fori_loop(0, bounds_ref[2 * grp + 1], body, 0)

    @pl.when(e == N_EXPERTS - 1)
    def _():
        o_ref[0] = jnp.dot(pmt_ref[...], ys_ref[...].astype(BF16), preferred_element_type=F32).astype(o_ref.dtype)


def _moe_experts(h, rec, w_gate, w_up, w_down):
    bsz, seq, d = h.shape
    tm = _pick(seq, (1024, 512))
    chunk = 128
    ps = tm + 2 * chunk
    assert N_GROUPS * (SORT_ALIGN - 1) + chunk <= 2 * chunk
    return pl.pallas_call(
        functools.partial(_moe_expert_kernel, tm=tm, chunk=chunk),
        grid=(bsz, seq // tm, N_EXPERTS),
        in_specs=[pl.BlockSpec((1, tm, d), lambda b, i, e: (b, i, 0)),
                  pl.BlockSpec((1, tm, ROUTER_LANES), lambda b, i, e: (b, i, 0)),
                  pl.BlockSpec((1, d, D_EXPERT), lambda b, i, e: (e, 0, 0)),
                  pl.BlockSpec((1, d, D_EXPERT), lambda b, i, e: (e, 0, 0)),
                  pl.BlockSpec((1, D_EXPERT, d), lambda b, i, e: (e, 0, 0))],
        out_specs=pl.BlockSpec((1, tm, d), lambda b, i, e: (b, i, 0)),
        out_shape=jax.ShapeDtypeStruct((bsz, seq, d), BF16),
        scratch_shapes=[pltpu.VMEM((tm, ps), BF16), pltpu.VMEM((ps, d), BF16), pltpu.VMEM((ps, ROUTER_LANES), F32),
                        pltpu.VMEM((ps, d), F32), pltpu.SMEM((2 * N_GROUPS,), jnp.int32)],
        compiler_params=_cparams(("parallel", "parallel", "arbitrary")),
        name="moe_experts",
    )(h, rec, w_gate, w_up, w_down)


def _resid_ln_kernel(x_ref, f_ref, g_ref, lng_ref, lnb_ref, o_ref):
    y = DEEPNORM_ALPHA * x_ref[0] + g_ref[0] * f_ref[0].astype(F32)
    o_ref[0] = _layer_norm(y, lng_ref[...], lnb_ref[...])


def _resid_ln(x, f, gate, ln_g, ln_b):
    bsz, seq, d = x.shape
    tm = _pick(seq, (512, 256))
    vec = pl.BlockSpec((1, d), lambda b, i: (0, 0))
    tile = pl.BlockSpec((1, tm, d), lambda b, i: (b, i, 0))
    return pl.pallas_call(
        _resid_ln_kernel,
        grid=(bsz, seq // tm),
        in_specs=[tile, tile, pl.BlockSpec((1, 1, d), lambda b, i: (b, 0, 0)), vec, vec],
        out_specs=tile,
        out_shape=jax.ShapeDtypeStruct((bsz, seq, d), F32),
        compiler_params=_cparams(("parallel", "parallel")),
        name="resid_ln",
    )(x, f, gate, ln_g.reshape(1, d), ln_b.reshape(1, d))


def _moe_ln(x, sc, sh, gate, w_router, b_router, w_gate, w_up, w_down, ln_g, ln_b):
    h, rec = _moe_route(x, sc, sh, w_router, b_router)
    f = _moe_experts(h, rec, w_gate, w_up, w_down)
    return _resid_ln(x, f, gate, ln_g, ln_b)


def _alibi_slopes(n):
    return jnp.asarray(2.0 ** (-8.0 * np.arange(1, n + 1) / n), dtype=F32)


def _colscale(q_ranges):
    cs = np.ones((1, PROJ_N), np.float32)
    for lo, hi, scale in q_ranges:
        cs[:, lo:hi] = scale
    return jnp.asarray(cs)


def _prep_params(p):
    q = dict(p)
    for name in ("w_in_ab", "w_out_ab", "w_in_cd", "w_out_cd", "w_gate", "w_up", "w_down"):
        q[name] = p[name].astype(BF16)
    pad = ROUTER_LANES - N_GROUPS - N_EXPERTS
    w_router = jnp.pad(jnp.concatenate([p["w_rg"], p["w_re"]], axis=-1), ((0, 0), (0, 0), (0, pad)))
    w_hi = w_router.astype(BF16)
    w_lo = (w_router - w_hi.astype(F32)).astype(BF16)
    q["w_router"] = jnp.stack([w_hi, w_lo], axis=1)
    q["b_router"] = jnp.pad(jnp.concatenate([p["b_rg"], p["b_re"]], axis=-1), ((0, 0), (0, pad)))[:, None, :]
    q["na_bias"] = [_na_bias_table(p["rpb_d"][i]) for i in range(p["rpb_d"].shape[0])]
    return q


def _trunk(x, ada, p):
    for l in range(DEPTH):
        sh1, sc1, g1, sh2, sc2, g2 = jnp.split(ada[l][:, None, :], 6, axis=-1)
        i = l // 2
        if l % 2 == 0:
            lam_init = 0.8 - 0.6 * math.exp(-0.3 * l)
            proj = _proj(x, sc1, sh1, p["w_in_ab"][i], _colscale(((0, 1024, ATTN_SCALE), (1536, 2560, ATTN_SCALE * LOG2E))))
            oa = _win_attention(proj, p["sink_a"][i], _alibi_slopes(WIN_Q_HEADS))
            lamvec = jnp.stack([p["lam_q1"][i], p["lam_k1"][i], p["lam_q2"][i], p["lam_k2"][i]])
            ob = _diff_attention(proj, _alibi_slopes(DIFF_HEADS), lamvec, p["subln_g"][i][None, :], lam_init)
            w_out = p["w_out_ab"][i]
        else:
            proj = _proj(x, sc1, sh1, p["w_in_cd"][i], _colscale(((1536, 2560, ATTN_SCALE),)))
            qk = _normrope(proj, p["qnorm_c"][i], p["knorm_c"][i])
            oa = _gqa_attention(qk, proj)
            ob = _na_attention(proj, p["na_bias"][i])
            w_out = p["w_out_cd"][i]
        x = _outproj_ln(oa, ob, w_out, x, g1, p["ln_g"][l, 0], p["ln_b"][l, 0])
        x = _moe_ln(x, sc2, sh2, g2, p["w_router"][l], p["b_router"][l],
                    p["w_gate"][l], p["w_up"][l], p["w_down"][l], p["ln_g"][l, 1], p["ln_b"][l, 1])
    return x


def kernel(x_prompt, x_sample, c_prompt, c_sample, w_ada, b_ada, ln_g, ln_b, w_in_ab, w_out_ab, sink_a,
           lam_q1, lam_k1, lam_q2, lam_k2, subln_g, w_in_cd, w_out_cd, qnorm_c, knorm_c, rpb_d,
           w_rg, b_rg, w_re, b_re, w_gate, w_up, w_down):
    p = _prep_params(dict(
        ln_g=ln_g, ln_b=ln_b, w_in_ab=w_in_ab, w_out_ab=w_out_ab, sink_a=sink_a,
        lam_q1=lam_q1, lam_k1=lam_k1, lam_q2=lam_q2, lam_k2=lam_k2, subln_g=subln_g,
        w_in_cd=w_in_cd, w_out_cd=w_out_cd, qnorm_c=qnorm_c, knorm_c=knorm_c, rpb_d=rpb_d,
        w_rg=w_rg, b_rg=b_rg, w_re=w_re, b_re=b_re, w_gate=w_gate, w_up=w_up, w_down=w_down))
    nb_p, nb_s = c_prompt.shape[0], c_sample.shape[0]
    c_all = jnp.concatenate([c_prompt, c_sample, jnp.zeros((8 - nb_p - nb_s, D_MODEL), F32)], axis=0)
    ada = _ada(c_all, w_ada, b_ada)
    y_prompt = _trunk(x_prompt, ada[:, :nb_p], p)
    y_sample = _trunk(x_sample, ada[:, nb_p:nb_p + nb_s], p)
    return (y_prompt, y_sample)
```
